```python
import math
import jax, jax.numpy as jnp
from jax import lax
import numpy as np

D_MODEL = 2048
BATCH = 1
SEQ = 8192
DEPTH = 2
DEC_BATCH = 32
DEC_SEQ = 1
PAST_LEN = 8192
PAGE_SIZE = 128

N_EVEN = (DEPTH + 1) // 2
N_ODD = DEPTH // 2

HD_A = 64
H_A = 16
D_A = H_A * HD_A
LORA_DECAY = 96
LORA_AAA = 96
LORA_GATE = 256
GN_EPS = 64e-5

POOL_WINDOWS = (2, 4, 8, 16)
N_POOL = len(POOL_WINDOWS)
D_POOL = D_MODEL - D_A
POOL_GD = D_POOL // N_POOL
POOL_BUF = max(POOL_WINDOWS) - 1

ATTN_GROUPS = ((128, 1), (512, 4), (2048, 16))
N_GROUPS = len(ATTN_GROUPS)
H_C = 8
HD_C = 128
D_C = H_C * HD_C
Q_BLK = 128

N_KEYS = 128
N_EXPERTS = N_KEYS * N_KEYS
PEER_HEADS = 8
D_KEY = 256
PEER_TOPK = 16
PEER_BLK = 128

NORM_EPS = 1e-6

kernel_name = 'hybrid_rwkv7_pool_dilated_attn_peer_step'


def rmsnorm(x, g):
    xf = x.astype(jnp.float32)
    r = lax.rsqrt(jnp.mean(xf * xf, axis=-1, keepdims=True) + NORM_EPS)
    return (xf * r).astype(x.dtype) * g


def _wkv_step(S, inp):
    r, w, k, v, a, b = inp
    sa = jnp.einsum('bhvk,bhk->bhv', S, a)
    S = S * w[:, :, None, :] + sa[..., None] * b[:, :, None, :] + v[..., None] * k[:, :, None, :]
    return S, jnp.einsum('bhvk,bhk->bhv', S, r)


def rwkv7_mix(h, h_prev, wkv0, z, z_prev, mu_rkv, mu_wag, w0, w1, w2, a0, a1, a2,
              g1, g2, k_k, k_a, r_k, gn_w, gn_b):
    B, T, _ = h.shape
    f32 = jnp.float32
    dh = jnp.concatenate([h_prev[:, None], h[:, :-1]], axis=1) - h
    xw = h + dh * mu_wag[0]
    xa = h + dh * mu_wag[1]
    xg = h + dh * mu_wag[2]
    dz = jnp.concatenate([z_prev[:, None], z[:, :-1]], axis=1) - z
    r, k, v = jnp.split(z + dz * mu_rkv, 3, axis=-1)
    w_log = -jax.nn.softplus(-(w0 + jnp.tanh(xw @ w1) @ w2)) - 0.5
    decay = jnp.exp(-jnp.exp(w_log.astype(f32)))
    a = jax.nn.sigmoid(a0 + (xa @ a1) @ a2)
    g = jax.nn.sigmoid(xg @ g1) @ g2
    heads = lambda t: t.reshape(B, T, H_A, HD_A).astype(f32)
    kk = heads(k * k_k)
    kk = kk / jnp.maximum(jnp.sqrt(jnp.sum(kk * kk, axis=-1, keepdims=True)), 1e-12)
    k = k * (1 + (a - 1) * k_a)
    rh, kh, vh, ah, wh = heads(r), heads(k), heads(v), heads(a), heads(decay)
    to_seq = lambda t: jnp.swapaxes(t, 0, 1)
    S, o = lax.scan(_wkv_step, wkv0.astype(f32),
                    (to_seq(rh), to_seq(wh), to_seq(kh), to_seq(vh), to_seq(-kk), to_seq(kk * ah)))
    o = jnp.swapaxes(o, 0, 1)
    mu = jnp.mean(o, axis=-1, keepdims=True)
    var = jnp.mean(jnp.square(o - mu), axis=-1, keepdims=True)
    o = ((o - mu) * lax.rsqrt(var + GN_EPS)).reshape(B, T, D_A) * gn_w + gn_b
    bonus = jnp.sum(rh * kh * r_k, axis=-1, keepdims=True) * vh
    o = (o + bonus.reshape(B, T, D_A)) * g
    return o.astype(h.dtype), S.astype(wkv0.dtype)


def pool_mix(u, buf, pos0, w_pool, pool_scale):
    B, T, _ = u.shape
    f32 = jnp.float32
    ext = jnp.concatenate([buf.astype(u.dtype), u], axis=1)
    cs = jnp.cumsum(ext.astype(f32), axis=1)
    cs = jnp.concatenate([jnp.zeros((B, 1, D_POOL), f32), cs], axis=1)
    pos = pos0 + jnp.arange(T)
    means = []
    for gi, win in enumerate(POOL_WINDOWS):
        c = slice(gi * POOL_GD, (gi + 1) * POOL_GD)
        hi = cs[:, POOL_BUF + 1:POOL_BUF + 1 + T, c]
        lo = cs[:, POOL_BUF + 1 - win:POOL_BUF + 1 - win + T, c]
        cnt = jnp.minimum(pos + 1, win).astype(f32)[None, :, None]
        means.append((hi - lo) / cnt)
    p = (jnp.concatenate(means, axis=-1) - u.astype(f32)).reshape(B, T, N_POOL, POOL_GD)
    p = jnp.einsum('btgc,gcd->btgd', p.astype(u.dtype), w_pool).reshape(B, T, D_POOL) * pool_scale
    return p, ext[:, ext.shape[1] - POOL_BUF:]


def even_mixer(h, h_prev, wkv0, pool_buf, pos0, w_in, w_out, mu_rkv, mu_wag, w0, w1, w2,
               a0, a1, a2, g1, g2, k_k, k_a, r_k, gn_w, gn_b, w_pool, pool_scale):
    z = h @ w_in
    z_rkv, u = z[..., :3 * D_A], z[..., 3 * D_A:]
    z_prev = h_prev @ w_in[:, :3 * D_A]
    o_a, wkv = rwkv7_mix(h, h_prev, wkv0, z_rkv, z_prev, mu_rkv, mu_wag, w0, w1, w2,
                         a0, a1, a2, g1, g2, k_k, k_a, r_k, gn_w, gn_b)
    o_b, new_buf = pool_mix(u, pool_buf, pos0, w_pool, pool_scale)
    y = jnp.concatenate([o_a, o_b], axis=-1) @ w_out
    return y, h[:, -1], wkv, new_buf


def _softmax_stats(s):
    m = jnp.max(s, axis=-1, keepdims=True)
    p = jnp.exp(s - m)
    den = jnp.sum(p, axis=-1, keepdims=True)
    return p / den, (m + jnp.log(den))[..., 0]


def dilated_attn_prompt(q, k, v, window, dil):
    B, S, H, hd = q.shape
    L = S // dil
    J = window // dil
    nb = -(-L // Q_BLK)
    Lp = nb * Q_BLK
    N = B * dil

    def sub(t):
        t = t.reshape(B, L, dil, H, hd).transpose(0, 2, 1, 3, 4).reshape(N, L, H, hd)
        return jnp.pad(t, ((0, 0), (0, Lp - L), (0, 0), (0, 0)))

    qs, ks, vs = sub(q), sub(k), sub(v)
    qb = qs.reshape(N, nb, Q_BLK, H, hd)

    def band(t):
        tp = jnp.pad(t, ((0, 0), (Q_BLK, 0), (0, 0), (0, 0)))
        prev = tp[:, :Lp].reshape(N, nb, Q_BLK, H, hd)
        return jnp.concatenate([prev, t.reshape(N, nb, Q_BLK, H, hd)], axis=2)

    kb, vb = band(ks), band(vs)
    s = jnp.einsum('nbqhd,nbkhd->nbhqk', qb, kb).astype(jnp.float32) * (hd ** -0.5)
    r = jnp.arange(Q_BLK)[:, None]
    i = jnp.arange(2 * Q_BLK)[None, :]
    dist = Q_BLK + r - i
    kpos = jnp.arange(nb)[:, None, None] * Q_BLK + i[None] - Q_BLK
    mask = ((dist >= 0) & (dist <= J))[None] & (kpos >= 0)
    s = jnp.where(mask[None, :, None], s, -jnp.inf)
    p, lse = _softmax_stats(s)
    o = jnp.einsum('nbhqk,nbkhd->nbqhd', p.astype(v.dtype), vb)
    lse = jnp.swapaxes(lse, 2, 3)
    o = o.reshape(B, dil, Lp, H, hd)[:, :, :L].transpose(0, 2, 1, 3, 4).reshape(B, S, H, hd)
    lse = lse.reshape(B, dil, Lp, H)[:, :, :L].transpose(0, 2, 1, 3).reshape(B, S, H)
    return o, lse


def dilated_attn_sample(q, k, v, kv_buf, window, dil):
    B, T, H, hd = q.shape
    n = kv_buf.shape[1]
    kc = jnp.concatenate([kv_buf[:, :, 0].astype(k.dtype), k], axis=1)
    vc = jnp.concatenate([kv_buf[:, :, 1].astype(v.dtype), v], axis=1)
    J = window // dil
    idx = n + jnp.arange(T)[:, None] - jnp.arange(J + 1)[None, :] * dil
    valid = idx >= 0
    idx = jnp.maximum(idx, 0)
    kg, vg = kc[:, idx], vc[:, idx]
    s = jnp.einsum('bthd,btjhd->bthj', q, kg).astype(jnp.float32) * (hd ** -0.5)
    s = jnp.where(valid[None, :, None, :], s, -jnp.inf)
    p, lse = _softmax_stats(s)
    o = jnp.einsum('bthj,btjhd->bthd', p.astype(v.dtype), vg)
    return o, lse


def merge_groups(outs, lses):
    alpha = jax.nn.softmax(jnp.stack(lses, axis=0), axis=0)
    return jnp.einsum('gbth,gbthd->bthd', alpha.astype(outs[0].dtype), jnp.stack(outs, axis=0))


def odd_prompt(h, w_in, w_out):
    B, T, _ = h.shape
    z = (h @ w_in).reshape(B, T, N_GROUPS, 3, H_C, HD_C)
    outs, lses, kvs = [], [], []
    for gi, (win, dil) in enumerate(ATTN_GROUPS):
        q, k, v = z[:, :, gi, 0], z[:, :, gi, 1], z[:, :, gi, 2]
        o, lse = dilated_attn_prompt(q, k, v, win, dil)
        outs.append(o)
        lses.append(lse)
        n = min(win, T)
        kvs.append(jnp.stack([k[:, T - n:], v[:, T - n:]], axis=2))
    return merge_groups(outs, lses).reshape(B, T, D_C) @ w_out, kvs


def odd_sample(h, bufs, w_in, w_out):
    B, T, _ = h.shape
    z = (h @ w_in).reshape(B, T, N_GROUPS, 3, H_C, HD_C)
    outs, lses, rows = [], [], []
    for gi, (win, dil) in enumerate(ATTN_GROUPS):
        q, k, v = z[:, :, gi, 0], z[:, :, gi, 1], z[:, :, gi, 2]
        o, lse = dilated_attn_sample(q, k, v, bufs[gi], win, dil)
        outs.append(o)
        lses.append(lse)
        rows.append(jnp.stack([k, v], axis=2))
    return merge_groups(outs, lses).reshape(B, T, D_C) @ w_out, rows


def peer(h, w_q, sub_keys, u_tab, v_tab):
    B, T, D = h.shape
    n_tok = B * T
    xt = h.reshape(n_tok, D)
    q = (xt @ w_q).reshape(n_tok, PEER_HEADS, 2, D_KEY // 2)
    s = jnp.einsum('nhpc,hpkc->nhpk', q, sub_keys).astype(jnp.float32)
    s1, i1 = lax.top_k(s[:, :, 0], PEER_TOPK)
    s2, i2 = lax.top_k(s[:, :, 1], PEER_TOPK)
    cand = (s1[..., :, None] + s2[..., None, :]).reshape(n_tok, PEER_HEADS, PEER_TOPK * PEER_TOPK)
    cidx = (i1[..., :, None] * N_KEYS + i2[..., None, :]).reshape(n_tok, PEER_HEADS, PEER_TOPK * PEER_TOPK)
    top_s, top_pos = lax.top_k(cand, PEER_TOPK)
    eidx = jnp.take_along_axis(cidx, top_pos, axis=-1).reshape(n_tok, PEER_HEADS * PEER_TOPK)
    gate = jax.nn.softmax(top_s, axis=-1).reshape(n_tok, PEER_HEADS * PEER_TOPK).astype(h.dtype)
    blk = min(PEER_BLK, n_tok)
    nb = -(-n_tok // blk)
    pad = nb * blk - n_tok
    xb = jnp.pad(xt, ((0, pad), (0, 0))).reshape(nb, blk, D)
    eb = jnp.pad(eidx, ((0, pad), (0, 0))).reshape(nb, blk, -1)
    gb = jnp.pad(gate, ((0, pad), (0, 0))).reshape(nb, blk, -1)

    def one_block(args):
        x_b, e_b, g_b = args
        act = jax.nn.gelu(jnp.einsum('td,tkd->tk', x_b, u_tab[e_b]), approximate=False)
        return jnp.einsum('tk,tkd->td', act * g_b, v_tab[e_b])

    out = lax.map(one_block, (xb, eb, gb)).reshape(nb * blk, D)[:n_tok]
    return out.reshape(B, T, D)


def setup_inputs(seed: int = 0) -> dict:
    key = jax.random.key(seed)
    ks = iter(jax.random.split(key, 48))
    f32 = jnp.float32
    nrm = lambda shape, scale: jax.random.normal(next(ks), shape, f32) * scale
    unif = lambda shape, lo, hi: jax.random.uniform(next(ks), shape, f32, lo, hi)
    D = D_MODEL
    nbuf = [min(win, PAST_LEN) for win, _ in ATTN_GROUPS]
    return {
        'x_prompt': nrm((BATCH, SEQ, D), 1.0),
        'x_sample': nrm((DEC_BATCH, DEC_SEQ, D), 1.0),
        'state_shift': nrm((N_EVEN, DEC_BATCH, D), 1.0),
        'state_wkv': nrm((N_EVEN, DEC_BATCH, H_A, HD_A, HD_A), 0.3),
        'state_pool': nrm((N_EVEN, DEC_BATCH, POOL_BUF, D_POOL), 1.0),
        'cache_kv_w128': nrm((N_ODD, DEC_BATCH, nbuf[0], 2, H_C, HD_C), 1.0),
        'cache_kv_w512': nrm((N_ODD, DEC_BATCH, nbuf[1], 2, H_C, HD_C), 1.0),
        'cache_kv_w2048': nrm((N_ODD, DEC_BATCH, nbuf[2], 2, H_C, HD_C), 1.0),
        'norm_mix': 1.0 + nrm((DEPTH, D), 0.02),
        'norm_ffn': 1.0 + nrm((DEPTH, D), 0.02),
        'norm_final': 1.0 + nrm((D,), 0.02),
        'a_w_in': nrm((N_EVEN, D, 3 * D_A + D_POOL), D ** -0.5),
        'a_w_out': nrm((N_EVEN, D_A + D_POOL, D), (D_A + D_POOL) ** -0.5),
        'a_mu_rkv': unif((N_EVEN, 3 * D_A), 0.0, 1.0),
        'a_mu_wag': unif((N_EVEN, 3, D), 0.0, 1.0),
        'a_w0': unif((N_EVEN, D_A), -6.5, -1.5),
        'a_w1': nrm((N_EVEN, D, LORA_DECAY), D ** -0.5),
        'a_w2': nrm((N_EVEN, LORA_DECAY, D_A), 0.1 * LORA_DECAY ** -0.5),
        'a_a0': nrm((N_EVEN, D_A), 0.1),
        'a_a1': nrm((N_EVEN, D, LORA_AAA), D ** -0.5),
        'a_a2': nrm((N_EVEN, LORA_AAA, D_A), LORA_AAA ** -0.5),
        'a_g1': nrm((N_EVEN, D, LORA_GATE), D ** -0.5),
        'a_g2': nrm((N_EVEN, LORA_GATE, D_A), LORA_GATE ** -0.5),
        'a_k_k': 0.85 + nrm((N_EVEN, D_A), 0.02),
        'a_k_a': 1.0 + nrm((N_EVEN, D_A), 0.02),
        'a_r_k': nrm((N_EVEN, H_A, HD_A), 0.1),
        'a_gn_w': 1.0 + nrm((N_EVEN, D_A), 0.02),
        'a_gn_b': nrm((N_EVEN, D_A), 0.02),
        'b_w_pool': nrm((N_EVEN, N_POOL, POOL_GD, POOL_GD), POOL_GD ** -0.5),
        'b_scale': 1.0 + nrm((N_EVEN, D_POOL), 0.02),
        'c_w_in': nrm((N_ODD, D, N_GROUPS * 3 * D_C), D ** -0.5),
        'c_w_out': nrm((N_ODD, D_C, D), D_C ** -0.5),
        'p_w_q': nrm((DEPTH, D, PEER_HEADS * D_KEY), D ** -0.5),
        'p_sub_keys': nrm((DEPTH, PEER_HEADS, 2, N_KEYS, D_KEY // 2), (D_KEY // 2) ** -0.5),
        'p_u': nrm((DEPTH, N_EXPERTS, D), D ** -0.5),
        'p_v': nrm((DEPTH, N_EXPERTS, D), 0.3),
    }


def reference(x_prompt, x_sample, state_shift, state_wkv, state_pool,
              cache_kv_w128, cache_kv_w512, cache_kv_w2048,
              norm_mix, norm_ffn, norm_final,
              a_w_in, a_w_out, a_mu_rkv, a_mu_wag, a_w0, a_w1, a_w2,
              a_a0, a_a1, a_a2, a_g1, a_g2, a_k_k, a_k_a, a_r_k, a_gn_w, a_gn_b,
              b_w_pool, b_scale, c_w_in, c_w_out,
              p_w_q, p_sub_keys, p_u, p_v):
    xp, xs = x_prompt, x_sample
    bp = xp.shape[0]
    caches = (cache_kv_w128, cache_kv_w512, cache_kv_w2048)
    sh_p, sh_s, wkv_p, wkv_s, pl_p, pl_s = [], [], [], [], [], []
    kv_p = [[] for _ in ATTN_GROUPS]
    kv_s = [[] for _ in ATTN_GROUPS]
    for layer in range(DEPTH):
        i = layer // 2
        hp = rmsnorm(xp, norm_mix[layer])
        hs = rmsnorm(xs, norm_mix[layer])
        if layer % 2 == 0:
            prm = (a_w_in[i], a_w_out[i], a_mu_rkv[i], a_mu_wag[i], a_w0[i], a_w1[i], a_w2[i],
                   a_a0[i], a_a1[i], a_a2[i], a_g1[i], a_g2[i], a_k_k[i], a_k_a[i], a_r_k[i],
                   a_gn_w[i], a_gn_b[i], b_w_pool[i], b_scale[i])
            yp, s1, s2, s3 = even_mixer(hp, jnp.zeros((bp, D_MODEL), hp.dtype),
                                        jnp.zeros((bp, H_A, HD_A, HD_A), state_wkv.dtype),
                                        jnp.zeros((bp, POOL_BUF, D_POOL), hp.dtype), 0, *prm)
            ys, t1, t2, t3 = even_mixer(hs, state_shift[i], state_wkv[i], state_pool[i], PAST_LEN, *prm)
            sh_p.append(s1); wkv_p.append(s2); pl_p.append(s3)
            sh_s.append(t1); wkv_s.append(t2); pl_s.append(t3)
        else:
            yp, kvp = odd_prompt(hp, c_w_in[i], c_w_out[i])
            ys, kvn = odd_sample(hs, tuple(c[i] for c in caches), c_w_in[i], c_w_out[i])
            for g in range(N_GROUPS):
                kv_p[g].append(kvp[g])
                kv_s[g].append(kvn[g])
        xp = xp + yp
        xs = xs + ys
        xp = xp + peer(rmsnorm(xp, norm_ffn[layer]), p_w_q[layer], p_sub_keys[layer], p_u[layer], p_v[layer])
        xs = xs + peer(rmsnorm(xs, norm_ffn[layer]), p_w_q[layer], p_sub_keys[layer], p_u[layer], p_v[layer])
    y_prompt = rmsnorm(xp, norm_final)
    y_sample = rmsnorm(xs, norm_final)
    return (y_prompt, y_sample,
            jnp.stack(sh_p, 0), jnp.stack(sh_s, 0),
            jnp.stack(wkv_p, 0), jnp.stack(wkv_s, 0),
            jnp.stack(pl_p, 0), jnp.stack(pl_s, 0),
            jnp.stack(kv_p[0], 0), jnp.stack(kv_s[0], 0),
            jnp.stack(kv_p[1], 0), jnp.stack(kv_s[1], 0),
            jnp.stack(kv_p[2], 0), jnp.stack(kv_s[2], 0))
```

```python
import functools
import math

import jax
import jax.numpy as jnp
from jax import lax
from jax.experimental import pallas as pl
from jax.experimental.pallas import tpu as pltpu

f32 = jnp.float32
bf16 = jnp.bfloat16

D_MODEL = 2048
H_A, HD_A = 16, 64
D_A = H_A * HD_A
D_POOL = D_MODEL - D_A
POOL_WINDOWS = (2, 4, 8, 16)
POOL_GD = D_POOL // len(POOL_WINDOWS)
POOL_BUF = max(POOL_WINDOWS) - 1
GN_EPS = 64e-5
NORM_EPS = 1e-6
ATTN_GROUPS = ((128, 1), (512, 4), (2048, 16))
H_C, HD_C = 8, 128
D_C = H_C * HD_C
N_KEYS = 128
N_EXPERTS = N_KEYS * N_KEYS
PEER_HEADS = 8
PEER_TOPK = 16

VMEM_LIMIT = 56 * 1024 * 1024
WKV_CHUNK = 64
ATTN_BLK = 2048
NEG_INF = float("-inf")


def _params(sem, vmem=VMEM_LIMIT):
    return pltpu.CompilerParams(dimension_semantics=sem, vmem_limit_bytes=vmem)


def _dot(a, b):
    return jnp.dot(a, b, preferred_element_type=f32)


def _dot_nt(a, b):
    return lax.dot_general(a, b, (((1,), (1,)), ((), ())), preferred_element_type=f32)


def _dot_tn(a, b):
    return lax.dot_general(a, b, (((0,), (0,)), ((), ())), preferred_element_type=f32)


def _split(x):
    hi = x.astype(bf16)
    lo = (x - hi.astype(f32)).astype(bf16)
    return hi, lo


def _dot3(a, b, dot=_dot):
    ah, al = _split(a)
    bh, bl = _split(b)
    return dot(ah, bh) + (dot(ah, bl) + dot(al, bh))


def _head_sum(x, bd):
    hi, lo = _split(x)
    return _dot(hi, bd) + _dot(lo, bd)


def _gelu(x):
    return 0.5 * x * (1.0 + lax.erf(x * 0.7071067811865476))


def _mm_kernel(*refs, has_norm, has_res, emit_h):
    it = iter(refs)
    x_ref, w_ref = next(it), next(it)
    g_ref = next(it) if has_norm else None
    res_ref = next(it) if has_res else None
    o_ref = next(it)
    h_ref = next(it) if emit_h else None
    xb_ref = next(it)

    @pl.when(pl.program_id(1) == 0)
    def _():
        x = x_ref[...]
        if has_norm:
            ms = jnp.mean(x * x, axis=-1, keepdims=True)
            x = (x * lax.rsqrt(ms + NORM_EPS)) * g_ref[...]
            if emit_h:
                h_ref[...] = x.astype(h_ref.dtype)
        xb_ref[...] = x.astype(bf16)

    acc = _dot(xb_ref[...], w_ref[...])
    if has_res:
        acc = acc + res_ref[...]
    o_ref[...] = acc


def _mm(x, w, *, norm_w=None, res=None, emit_h=None, tm, tn, row0=0, rows=None):
    K = x.shape[1]
    N = w.shape[1]
    rows = x.shape[0] if rows is None else rows
    assert rows % tm == 0 and N % tn == 0
    in_specs = [pl.BlockSpec((tm, K), lambda i, j: (i + row0, 0)),
                pl.BlockSpec((K, tn), lambda i, j: (0, j))]
    args = [x, w]
    if norm_w is not None:
        in_specs.append(pl.BlockSpec((1, K), lambda i, j: (0, 0)))
        args.append(norm_w.reshape(1, K))
    if res is not None:
        in_specs.append(pl.BlockSpec((tm, tn), lambda i, j: (i + row0, j)))
        args.append(res)
    out_shape = [jax.ShapeDtypeStruct((rows, N), f32)]
    out_specs = [pl.BlockSpec((tm, tn), lambda i, j: (i, j))]
    if emit_h is not None:
        out_shape.append(jax.ShapeDtypeStruct((rows, K), emit_h))
        out_specs.append(pl.BlockSpec((tm, K), lambda i, j: (i, 0)))
    outs = pl.pallas_call(
        functools.partial(_mm_kernel, has_norm=norm_w is not None, has_res=res is not None,
                          emit_h=emit_h is not None),
        grid=(rows // tm, N // tn),
        in_specs=in_specs, out_specs=out_specs, out_shape=out_shape,
        scratch_shapes=[pltpu.VMEM((tm, K), bf16)],
        compiler_params=_params(("parallel", "arbitrary")),
    )(*args)
    return outs if emit_h is not None else outs[0]


def _rms_kernel(x_ref, g_ref, o_ref):
    x = x_ref[...]
    ms = jnp.mean(x * x, axis=-1, keepdims=True)
    o_ref[...] = (x * lax.rsqrt(ms + NORM_EPS)) * g_ref[...]


def _rmsnorm(x, g, *, tm, row0, rows):
    K = x.shape[1]
    return pl.pallas_call(
        _rms_kernel, grid=(rows // tm,),
        in_specs=[pl.BlockSpec((tm, K), lambda i: (i + row0, 0)), pl.BlockSpec((1, K), lambda i: (0, 0))],
        out_specs=pl.BlockSpec((tm, K), lambda i: (i, 0)),
        out_shape=jax.ShapeDtypeStruct((rows, K), f32),
        compiler_params=_params(("parallel",)),
    )(x, g.reshape(1, K))


def _even_token_math(h, hs, zc, zs, P):
    dh = hs - h
    mu = P["mu_wag"]
    xw = (h + dh * mu[0:1]).astype(bf16)
    xa = (h + dh * mu[1:2]).astype(bf16)
    xg = (h + dh * mu[2:3]).astype(bf16)
    tw = jnp.tanh(_dot(xw, P["w1"])).astype(bf16)
    wl = P["w0"] + _dot(tw, P["w2"])
    w_log = -jax.nn.softplus(-wl) - 0.5
    lw = -jnp.exp(w_log)
    a = jax.nn.sigmoid(P["a0"] + _dot(_dot(xa, P["a1"]).astype(bf16), P["a2"]))
    g = _dot(jax.nn.sigmoid(_dot(xg, P["g1"])).astype(bf16), P["g2"])
    rkv = zc + (zs - zc) * P["mu_rkv"]
    r, k, v = rkv[:, :D_A], rkv[:, D_A:2 * D_A], rkv[:, 2 * D_A:]
    kk = k * P["k_k"]
    nrm = jnp.sqrt(_head_sum(kk * kk, P["bd"]))
    kkn = kk / jnp.maximum(nrm, 1e-12)
    k2 = k * (1.0 + (a - 1.0) * P["k_a"])
    return r, lw, k2, v, kkn, a, g


def _pool_project(pm, P):
    outs = []
    for gi in range(len(POOL_WINDOWS)):
        c = slice(gi * POOL_GD, (gi + 1) * POOL_GD)
        outs.append(_dot(pm[:, c].astype(bf16), P["w_pool"][gi]))
    return jnp.concatenate(outs, axis=-1) * P["pool_scale"]


_EVEN_PARAM_NAMES = ("mu_wag", "mu_rkv", "w0", "w1", "w2", "a0", "a1", "a2", "g1", "g2", "k_k", "k_a",
                     "w_pool", "pool_scale", "bd")


def _load_params(refs):
    return {n: r[...] for n, r in zip(_EVEN_PARAM_NAMES, refs)}


def _even_mid_prompt_kernel(h_ref, hp_ref, z_ref, zp_ref, *rest, tm):
    prefs, outs = rest[:len(_EVEN_PARAM_NAMES)], rest[len(_EVEN_PARAM_NAMES):]
    P = _load_params(prefs)
    first = pl.program_id(0) == 0
    h = h_ref[...]
    z = z_ref[...]
    zc, u = z[:, :3 * D_A], z[:, 3 * D_A:]
    hprev = jnp.where(first, 0.0, hp_ref[15:16, :])
    zprev = jnp.where(first, 0.0, zp_ref[15:16, :3 * D_A])
    row = lax.broadcasted_iota(jnp.int32, (tm, 1), 0)
    hs = jnp.where(row == 0, hprev, pltpu.roll(h, 1, axis=0))
    zs = jnp.where(row == 0, zprev, pltpu.roll(zc, 1, axis=0))
    prow = lax.broadcasted_iota(jnp.int32, (tm, POOL_GD), 0)
    r, lw, k2, v, kkn, a, g = _even_token_math(h, hs, zc, zs, P)

    uprev = jnp.where(first, 0.0, zp_ref[:, 3 * D_A:])
    pos = pl.program_id(0) * tm + prow
    means = []
    for gi, win in enumerate(POOL_WINDOWS):
        c = slice(gi * POOL_GD, (gi + 1) * POOL_GD)
        s = jnp.concatenate([uprev[:, c], u[:, c]], axis=0)
        sh = 1
        while sh < win:
            s = s + pltpu.roll(s, sh, axis=0)
            sh *= 2
        cnt = jnp.minimum(pos + 1, win).astype(f32)
        means.append(s[16:, :] / cnt)
    ob = _pool_project(jnp.concatenate(means, axis=-1) - u, P)
    for o_ref, val in zip(outs, (r, lw, k2, v, kkn, a, g, ob)):
        o_ref[...] = val


def _even_mid_sample_kernel(h_ref, hs_ref, z_ref, zs_ref, buf_ref, *rest):
    prefs, outs = rest[:len(_EVEN_PARAM_NAMES)], rest[len(_EVEN_PARAM_NAMES):]
    P = _load_params(prefs)
    h = h_ref[...]
    z = z_ref[...]
    zc, u = z[:, :3 * D_A], z[:, 3 * D_A:]
    r, lw, k2, v, kkn, a, g = _even_token_math(h, hs_ref[...], zc, zs_ref[...], P)
    means = []
    for gi, win in enumerate(POOL_WINDOWS):
        c = slice(gi * POOL_GD, (gi + 1) * POOL_GD)
        s = u[:, c]
        for j in range(POOL_BUF - (win - 1), POOL_BUF):
            s = s + buf_ref[j, :, c]
        means.append(s / float(win))
    ob = _pool_project(jnp.concatenate(means, axis=-1) - u, P)
    for o_ref, val in zip(outs, (r, lw, k2, v, kkn, a, g, ob)):
        o_ref[...] = val


def _const_spec(shape):
    nd = len(shape)
    return pl.BlockSpec(shape, lambda i, _nd=nd: (0,) * _nd, pipeline_mode=pl.Buffered(1))


def _even_mid_prompt(h_all, z_all, plist, *, T, tm):
    nb = tm // 16
    in_specs = [pl.BlockSpec((tm, D_MODEL), lambda i: (i, 0)),
                pl.BlockSpec((16, D_MODEL), lambda i: (jnp.maximum(i * nb - 1, 0), 0)),
                pl.BlockSpec((tm, 4 * D_A), lambda i: (i, 0)),
                pl.BlockSpec((16, 4 * D_A), lambda i: (jnp.maximum(i * nb - 1, 0), 0))]
    in_specs += [_const_spec(p.shape) for p in plist]
    return pl.pallas_call(
        functools.partial(_even_mid_prompt_kernel, tm=tm),
        grid=(T // tm,), in_specs=in_specs,
        out_specs=[pl.BlockSpec((tm, D_A), lambda i: (i, 0))] * 8,
        out_shape=[jax.ShapeDtypeStruct((T, D_A), f32)] * 8,
        compiler_params=_params(("parallel",)),
    )(h_all, h_all, z_all, z_all, *plist)


def _even_mid_sample(h_all, hs, z_all, zs, buf_t, plist, *, row0, n):
    in_specs = [pl.BlockSpec((n, D_MODEL), lambda i: (row0, 0)),
                pl.BlockSpec((n, D_MODEL), lambda i: (0, 0)),
                pl.BlockSpec((n, 4 * D_A), lambda i: (row0, 0)),
                pl.BlockSpec((n, 3 * D_A), lambda i: (0, 0)),
                pl.BlockSpec((POOL_BUF, n, D_POOL), lambda i: (0, 0, 0))]
    in_specs += [_const_spec(p.shape) for p in plist]
    return pl.pallas_call(
        _even_mid_sample_kernel, grid=(1,), in_specs=in_specs,
        out_specs=[pl.BlockSpec((n, D_A), lambda i: (0, 0))] * 8,
        out_shape=[jax.ShapeDtypeStruct((n, D_A), f32)] * 8,
        compiler_params=_params(("arbitrary",)),
    )(h_all, hs, z_all, zs, buf_t, *plist)


def _wkv_prompt_kernel(r_ref, lw_ref, k_ref, v_ref, kk_ref, a_ref, o_ref, s_out_ref, S_ref, *, tb):
    C = WKV_CHUNK
    t = pl.program_id(1)

    @pl.when(t == 0)
    def _():
        S_ref[...] = jnp.zeros_like(S_ref)

    ri = lax.broadcasted_iota(jnp.int32, (C, C), 0)
    ci = lax.broadcasted_iota(jnp.int32, (C, C), 1)
    strict = ri > ci
    incl = ri >= ci
    eye = (ri == ci).astype(f32)
    rows = lax.broadcasted_iota(jnp.int32, (C, 2 * HD_A), 0)

    def chunk(c, carry):
        off = pl.multiple_of(c * C, C)
        sl = pl.ds(off, C)
        r, lw, k, v, kk, a = (x[sl, :] for x in (r_ref, lw_ref, k_ref, v_ref, kk_ref, a_ref))
        cum = lw
        sh = 1
        while sh < C:
            cum = cum + jnp.where(rows >= sh, pltpu.roll(cum, sh, axis=0), 0.0)
            sh *= 2
        cum_c = cum[C - 1:C, :]
        b = kk * a
        e_neg = jnp.exp(-cum)
        at = -kk * jnp.exp(cum - lw)
        rt = r * jnp.exp(cum)
        bt = b * e_neg
        kt = k * e_neg
        e_rem = jnp.exp(cum_c - cum)
        bh = b * e_rem
        kh = k * e_rem
        w_c = jnp.exp(cum_c)
        o_parts = []
        for hh in range(2):
            hs = slice(hh * HD_A, (hh + 1) * HD_A)
            ar = jnp.concatenate([at[:, hs], rt[:, hs]], axis=0)
            bk = jnp.concatenate([bt[:, hs], kt[:, hs]], axis=0)
            m4 = _dot3(ar, bk, _dot_nt)
            a_ab = jnp.where(strict, m4[:C, :C], 0.0)
            a_ak = jnp.where(strict, m4[:C, C:], 0.0)
            a_rb = jnp.where(incl, m4[C:, :C], 0.0)
            a_rk = jnp.where(incl, m4[C:, C:], 0.0)
            tinv = eye + a_ab
            pw = _dot3(a_ab, a_ab)
            n_sq = 2
            while n_sq * 2 < C:
                x = _dot3(pw, jnp.concatenate([tinv, pw], axis=1))
                tinv = tinv + x[:, :C]
                pw = x[:, C:]
                n_sq *= 2
            tinv = tinv + _dot3(pw, tinv)
            S = S_ref[hh]
            vh = v[:, hs]
            sar = _dot3(ar, S, _dot_nt)
            av = _dot3(jnp.concatenate([a_ak, a_rk], axis=0), vh)
            u = _dot3(tinv, sar[:C] + av[:C])
            o_parts.append(sar[C:] + av[C:] + _dot3(a_rb, u))
            uv = jnp.concatenate([u, vh], axis=0)
            bkh = jnp.concatenate([bh[:, hs], kh[:, hs]], axis=0)
            S_ref[hh] = S * w_c[:, hs] + _dot3(uv, bkh, _dot_tn)
        o_ref[sl, :] = jnp.concatenate(o_parts, axis=1)
        return carry

    lax.fori_loop(0, tb // C, chunk, 0)

    @pl.when(t == pl.num_programs(1) - 1)
    def _():
        s_out_ref[...] = S_ref[...]


def _wkv_prompt(r, lw, k2, v, kkn, a, *, T, tb):
    spec = pl.BlockSpec((tb, 2 * HD_A), lambda j, t: (t, j))
    return pl.pallas_call(
        functools.partial(_wkv_prompt_kernel, tb=tb),
        grid=(H_A // 2, T // tb),
        in_specs=[spec] * 6,
        out_specs=[spec, pl.BlockSpec((2, HD_A, HD_A), lambda j, t: (j, 0, 0))],
        out_shape=[jax.ShapeDtypeStruct((T, D_A), f32), jax.ShapeDtypeStruct((H_A, HD_A, HD_A), f32)],
        scratch_shapes=[pltpu.VMEM((2, HD_A, HD_A), f32)],
        compiler_params=_params(("parallel", "arbitrary")),
    )(r, lw, k2, v, kkn, a)


def _wkv_sample_kernel(s_ref, r_ref, lw_ref, k_ref, kk_ref, a_ref, v_ref, o_ref, s_out_ref):
    S = s_ref[0]
    kk = kk_ref[0]
    sa = jnp.sum(S * (-kk), axis=-1, keepdims=True)
    S2 = S * jnp.exp(lw_ref[0]) + sa * (kk * a_ref[0]) + v_ref[0] * k_ref[0]
    s_out_ref[0] = S2
    o_ref[0] = jnp.sum(S2 * r_ref[0], axis=-1, keepdims=True)


def _wkv_sample(S0, r, lw, k2, v, kkn, a):
    n = S0.shape[0]
    row = lambda x: x.reshape(n, H_A, 1, HD_A)
    rspec = pl.BlockSpec((1, H_A, 1, HD_A), lambda b: (b, 0, 0, 0))
    cspec = pl.BlockSpec((1, H_A, HD_A, 1), lambda b: (b, 0, 0, 0))
    sspec = pl.BlockSpec((1, H_A, HD_A, HD_A), lambda b: (b, 0, 0, 0))
    o, S = pl.pallas_call(
        _wkv_sample_kernel, grid=(n,),
        in_specs=[sspec, rspec, rspec, rspec, rspec, rspec, cspec],
        out_specs=[cspec, sspec],
        out_shape=[jax.ShapeDtypeStruct((n, H_A, HD_A, 1), f32), jax.ShapeDtypeStruct(S0.shape, f32)],
        compiler_params=_params(("parallel",)),
    )(S0, row(r), row(lw), row(k2), row(kkn), row(a), v.reshape(n, H_A, HD_A, 1))
    return o.reshape(n, D_A), S


def _even_post_kernel(x_ref, o_ref, r_ref, k_ref, v_ref, g_ref, ob_ref, wo_ref, rk_ref, gnw_ref, gnb_ref, bd_ref,
                      out_ref):
    bd = bd_ref[...]
    o = o_ref[...]
    inv = 1.0 / HD_A
    mu = _head_sum(o, bd) * inv
    d = o - mu
    var = _head_sum(d * d, bd) * inv
    on = d * lax.rsqrt(var + GN_EPS) * gnw_ref[...] + gnb_ref[...]
    v = v_ref[...]
    bonus = _head_sum(r_ref[...] * k_ref[...] * rk_ref[...], bd) * v
    oa = ((on + bonus) * g_ref[...]).astype(bf16)
    y = _dot(oa, wo_ref[:D_A, :]) + _dot(ob_ref[...].astype(bf16), wo_ref[D_A:, :])
    out_ref[...] = x_ref[...] + y


def _even_post(x_all, o, r, k2, v, g, ob, wo, rk, gnw, gnb, bd, *, row0, rows, tm):
    a_spec = pl.BlockSpec((tm, D_A), lambda i: (i, 0))
    x_spec = pl.BlockSpec((tm, D_MODEL), lambda i: (i + row0, 0))
    consts = [wo, rk, gnw, gnb, bd]
    return pl.pallas_call(
        _even_post_kernel, grid=(rows // tm,),
        in_specs=[x_spec] + [a_spec] * 6 + [_const_spec(c.shape) for c in consts],
        out_specs=x_spec,
        out_shape=jax.ShapeDtypeStruct(x_all.shape, f32),
        input_output_aliases={0: 0},
        compiler_params=_params(("parallel",)),
    )(x_all, o, r, k2, v, g, ob, *consts)


def _attn_prompt_kernel(*refs):
    in_refs, out_ref, og_ref, lg_ref = refs[:15], refs[15], refs[16], refs[17]
    n = pl.program_id(1)
    has_prev = n > 0
    scale = HD_C ** -0.5
    Q = 128
    ri = lax.broadcasted_iota(jnp.int32, (Q, Q), 0)
    ci = lax.broadcasted_iota(jnp.int32, (Q, Q), 1)
    mask_prev0 = ci >= ri
    mask_cur = ci <= ri

    for g, (_, dil) in enumerate(ATTN_GROUPS):
        q_ref, kp_ref, k_ref, vp_ref, v_ref = in_refs[5 * g:5 * g + 5]
        nblk = ATTN_BLK // (Q * dil)

        def unit(u, carry, q_ref=q_ref, kp_ref=kp_ref, k_ref=k_ref, vp_ref=vp_ref, v_ref=v_ref,
                 dil=dil, nblk=nblk, g=g):
            c = u // nblk
            m = u % nblk
            start = m * (Q * dil) + c
            rows = pl.ds(start, Q, stride=dil)
            q = q_ref[rows, :].astype(bf16)
            kc = k_ref[rows, :].astype(bf16)
            vc = v_ref[rows, :].astype(bf16)
            in_blk = m > 0
            st_a = jnp.maximum(m - 1, 0) * (Q * dil) + c
            st_b = (nblk - 1) * (Q * dil) + c
            rows_a = pl.ds(st_a, Q, stride=dil)
            rows_b = pl.ds(st_b, Q, stride=dil)
            kp = jnp.where(in_blk, k_ref[rows_a, :], kp_ref[rows_b, :]).astype(bf16)
            vp = jnp.where(in_blk, v_ref[rows_a, :], vp_ref[rows_b, :]).astype(bf16)
            valid_prev = jnp.logical_or(in_blk, has_prev)
            s_p = _dot_nt(q, kp) * scale
            s_c = _dot_nt(q, kc) * scale
            s_p = jnp.where(jnp.logical_and(mask_prev0, valid_prev), s_p, NEG_INF)
            s_c = jnp.where(mask_cur, s_c, NEG_INF)
            mx = jnp.maximum(jnp.max(s_p, axis=1, keepdims=True), jnp.max(s_c, axis=1, keepdims=True))
            p_p = jnp.exp(s_p - mx)
            p_c = jnp.exp(s_c - mx)
            den = jnp.sum(p_p, axis=1, keepdims=True) + jnp.sum(p_c, axis=1, keepdims=True)
            o = (_dot(p_p.astype(bf16), vp) + _dot(p_c.astype(bf16), vc)) / den
            lse = mx + jnp.log(den)
            og_ref[g, rows, :] = o
            lg_ref[g, rows, :] = jnp.broadcast_to(lse, (Q, HD_C))
            return carry

        lax.fori_loop(0, dil * nblk, unit, 0)

    l0, l1, l2 = lg_ref[0], lg_ref[1], lg_ref[2]
    mx = jnp.maximum(jnp.maximum(l0, l1), l2)
    e0, e1, e2 = jnp.exp(l0 - mx), jnp.exp(l1 - mx), jnp.exp(l2 - mx)
    out_ref[...] = (e0 * og_ref[0] + e1 * og_ref[1] + e2 * og_ref[2]) / (e0 + e1 + e2)


def _attn_prompt(z, *, T):
    nb = T // ATTN_BLK
    in_specs = []
    for g in range(len(ATTN_GROUPS)):
        def col(j, g=g):
            return lambda h, n: (n, g * 3 * H_C + j * H_C + h)

        def col_prev(j, g=g):
            return lambda h, n: (jnp.maximum(n - 1, 0), g * 3 * H_C + j * H_C + h)
        blk = (ATTN_BLK, HD_C)
        in_specs += [pl.BlockSpec(blk, col(0)), pl.BlockSpec(blk, col_prev(1)), pl.BlockSpec(blk, col(1)),
                     pl.BlockSpec(blk, col_prev(2)), pl.BlockSpec(blk, col(2))]
    return pl.pallas_call(
        _attn_prompt_kernel, grid=(H_C, nb),
        in_specs=in_specs,
        out_specs=pl.BlockSpec((ATTN_BLK, HD_C), lambda h, n: (n, h)),
        out_shape=jax.ShapeDtypeStruct((T, D_C), f32),
        scratch_shapes=[pltpu.VMEM((3, ATTN_BLK, HD_C), f32), pltpu.VMEM((3, ATTN_BLK, HD_C), f32)],
        compiler_params=_params(("parallel", "arbitrary")),
    )(*([z] * 15))


def _attn_sample_kernel(z_ref, c0_ref, c1_ref, c2_ref, out_ref):
    scale = HD_C ** -0.5
    crefs = (c0_ref, c1_ref, c2_ref)
    for h in range(H_C):
        outs, lses = [], []
        for g in range(len(ATTN_GROUPS)):
            base = g * 3 * D_C + h * HD_C
            q = z_ref[0, :, base:base + HD_C]
            kn = z_ref[0, :, base + D_C:base + D_C + HD_C]
            vn = z_ref[0, :, base + 2 * D_C:base + 2 * D_C + HD_C]
            K = crefs[g][0, :, 0, 0, h, :]
            V = crefs[g][0, :, 0, 1, h, :]
            s_c = jnp.sum(K * q, axis=1, keepdims=True) * scale
            s_n = jnp.sum(kn * q, axis=1, keepdims=True) * scale
            mx = jnp.maximum(jnp.max(s_c, axis=0, keepdims=True), s_n)
            p_c = jnp.exp(s_c - mx)
            p_n = jnp.exp(s_n - mx)
            den = jnp.sum(p_c, axis=0, keepdims=True) + p_n
            outs.append((jnp.sum(p_c * V, axis=0, keepdims=True) + p_n * vn) / den)
            lses.append(mx + jnp.log(den))
        mx = jnp.maximum(jnp.maximum(lses[0], lses[1]), lses[2])
        es = [jnp.exp(l - mx) for l in lses]
        out_ref[0, :, h * HD_C:(h + 1) * HD_C] = (es[0] * outs[0] + es[1] * outs[1] + es[2] * outs[2]) / (
            es[0] + es[1] + es[2])


def _attn_sample(z_s, caches):
    n = z_s.shape[0]
    in_specs = [pl.BlockSpec((1, 1, 9 * D_C), lambda b: (b, 0, 0))]
    args = [z_s.reshape(n, 1, 9 * D_C)]
    for (win, dil), c in zip(ATTN_GROUPS, caches):
        assert c.shape[1] == win
        args.append(c.reshape(n, win // dil, dil, 2, H_C, HD_C))
        in_specs.append(pl.BlockSpec((1, win // dil, 1, 2, H_C, HD_C), lambda b: (b, 0, 0, 0, 0, 0)))
    out = pl.pallas_call(
        _attn_sample_kernel, grid=(n,), in_specs=in_specs,
        out_specs=pl.BlockSpec((1, 1, D_C), lambda b: (b, 0, 0)),
        out_shape=jax.ShapeDtypeStruct((n, 1, D_C), f32),
        compiler_params=_params(("parallel",)),
    )(*args)
    return out.reshape(n, D_C)


def _top16_rows(s):
    rows = lax.broadcasted_iota(jnp.int32, s.shape, 0)
    work = s
    vals = []
    for _ in range(PEER_TOPK):
        m = jnp.max(work, axis=0, keepdims=True)
        idx = jnp.min(jnp.where(work == m, rows, s.shape[0]), axis=0, keepdims=True)
        work = jnp.where(rows == idx, NEG_INF, work)
        vals.append(m)
    return vals, jnp.logical_and(work == NEG_INF, s != NEG_INF)


def _peer_route_kernel(q_ref, keys_ref, s1_ref, s2_ref, e1_ref, e2_ref, tau_ref):
    n = q_ref.shape[0]
    taus = []
    for h in range(PEER_HEADS):
        sm, vals = [], []
        for p in range(2):
            hp = 2 * h + p
            qs = q_ref[:, hp * N_KEYS:(hp + 1) * N_KEYS].astype(bf16)
            s = _dot_nt(keys_ref[hp], qs)
            v, member = _top16_rows(s)
            sm.append(jnp.where(member, s, NEG_INF))
            vals.append(v)
        pieces = []
        for r1 in range(PEER_TOPK):
            cnt = PEER_TOPK // (r1 + 1)
            pieces.append(vals[0][r1] + jnp.concatenate(vals[1][:cnt], axis=0))
        npad = (-sum(p.shape[0] for p in pieces)) % 8
        pieces.append(jnp.full((npad, n), NEG_INF, f32))
        top, _ = _top16_rows(jnp.concatenate(pieces, axis=0))
        m = top[0]
        zsum = sum(jnp.exp(t - m) for t in top)
        taus.append(top[-1])
        rows = slice(h * N_KEYS, (h + 1) * N_KEYS)
        s1_ref[rows, :] = sm[0]
        s2_ref[rows, :] = sm[1]
        e1_ref[rows, :] = jnp.exp(sm[0] - vals[0][0])
        e2_ref[rows, :] = jnp.exp(sm[1] - vals[1][0]) / zsum
    tau_ref[...] = jnp.concatenate(taus, axis=0)


def _peer_route(q, keys, *, tmr):
    M = q.shape[0]
    big = jax.ShapeDtypeStruct((PEER_HEADS * N_KEYS, M), f32)
    bspec = pl.BlockSpec((PEER_HEADS * N_KEYS, tmr), lambda i: (0, i))
    return pl.pallas_call(
        _peer_route_kernel, grid=(M // tmr,),
        in_specs=[pl.BlockSpec((tmr, 2 * PEER_HEADS * N_KEYS), lambda i: (i, 0)),
                  _const_spec(keys.shape)],
        out_specs=[bspec] * 4 + [pl.BlockSpec((PEER_HEADS, tmr), lambda i: (0, i))],
        out_shape=[big] * 4 + [jax.ShapeDtypeStruct((PEER_HEADS, M), f32)],
        compiler_params=_params(("parallel",)),
    )(q, keys)


def _peer_prep_kernel(u_ref, v_ref, ub_ref, vt_ref):
    ub_ref[...] = u_ref[...].astype(bf16)
    vt_ref[...] = v_ref[...].T.astype(bf16)


def _peer_prep(u_tab, v_tab, *, te):
    L, E, D = u_tab.shape
    return pl.pallas_call(
        _peer_prep_kernel, grid=(L, E // te),
        in_specs=[pl.BlockSpec((None, te, D), lambda l, j: (l, j, 0))] * 2,
        out_specs=[pl.BlockSpec((None, te, D), lambda l, j: (l, j, 0)),
                   pl.BlockSpec((None, D, te), lambda l, j: (l, 0, j))],
        out_shape=[jax.ShapeDtypeStruct((L, E, D), bf16), jax.ShapeDtypeStruct((L, D, E), bf16)],
        compiler_params=_params(("parallel", "parallel")),
    )(u_tab, v_tab)


PEER_RB = 32
PEER_LC = 256


def _peer_dense_kernel(xb_ref, u_ref, vt_ref, s1_ref, s2_ref, e1_ref, e2_ref, tau_ref, res_ref, out_ref,
                       act_ref, p_ref, acc_ref, *, tm, te):
    j = pl.program_id(1)
    n_i1 = te // N_KEYS

    @pl.when(j == 0)
    def _():
        acc_ref[...] = jnp.zeros_like(acc_ref)

    act_ref[...] = _dot_nt(u_ref[...], xb_ref[...])

    n_rb = N_KEYS // PEER_RB

    def step(it, carry):
        lo = pl.multiple_of((it // n_rb) * PEER_LC, PEER_LC)
        r0 = pl.multiple_of((it % n_rb) * PEER_RB, PEER_RB)
        lanes = pl.ds(lo, PEER_LC)
        gates = [jnp.zeros((PEER_RB, PEER_LC), f32) for _ in range(n_i1)]
        for h in range(PEER_HEADS):
            rows2 = pl.ds(h * N_KEYS + r0, PEER_RB)
            s2 = s2_ref[rows2, lanes]
            e2 = e2_ref[rows2, lanes]
            tau = tau_ref[h:h + 1, lanes]
            for ii in range(n_i1):
                row1 = pl.ds(h * N_KEYS + j * n_i1 + ii, 1)
                c = s2 + s1_ref[row1, lanes]
                gates[ii] = gates[ii] + jnp.where(c >= tau, e2 * e1_ref[row1, lanes], 0.0)
        for ii in range(n_i1):
            rows = pl.ds(ii * N_KEYS + r0, PEER_RB)
            p_ref[rows, lanes] = (_gelu(act_ref[rows, lanes]) * gates[ii]).astype(bf16)
        return carry

    lax.fori_loop(0, (tm // PEER_LC) * n_rb, step, 0)
    acc_ref[...] += _dot(vt_ref[...], p_ref[...])

    @pl.when(j == pl.num_programs(1) - 1)
    def _():
        out_ref[...] = acc_ref[...].T + res_ref[...]


def _peer_dense(xb, ub, vt, s1, s2, e1, e2, tau, res, *, tm, te):
    M, D = xb.shape
    E = ub.shape[0]
    once = dict(pipeline_mode=pl.Buffered(1))
    rspec = pl.BlockSpec((PEER_HEADS * N_KEYS, tm), lambda i, j: (0, i), **once)
    return pl.pallas_call(
        functools.partial(_peer_dense_kernel, tm=tm, te=te),
        grid=(M // tm, E // te),
        in_specs=[pl.BlockSpec((tm, D), lambda i, j: (i, 0), **once),
                  pl.BlockSpec((te, D), lambda i, j: (j, 0)),
                  pl.BlockSpec((D, te), lambda i, j: (0, j)),
                  rspec, rspec, rspec, rspec,
                  pl.BlockSpec((PEER_HEADS, tm), lambda i, j: (0, i), **once),
                  pl.BlockSpec((tm, D), lambda i, j: (i, 0), **once)],
        out_specs=pl.BlockSpec((tm, D), lambda i, j: (i, 0), **once),
        out_shape=jax.ShapeDtypeStruct((M, D), f32),
        scratch_shapes=[pltpu.VMEM((te, tm), f32), pltpu.VMEM((te, tm), bf16), pltpu.VMEM((D, tm), f32)],
        compiler_params=_params(("parallel", "arbitrary")),
    )(xb, ub, vt, s1, s2, e1, e2, tau, res)


def _peer(x_all, norm_w, wq, keys, ub, vt, *, tm_mm, tmr, tm, te):
    q, hb = _mm(x_all, wq, norm_w=norm_w, emit_h=bf16, tm=tm_mm, tn=1024)
    s1, s2, e1, e2, tau = _peer_route(q, keys, tmr=tmr)
    return _peer_dense(hb, ub, vt, s1, s2, e1, e2, tau, x_all, tm=tm, te=te)


def _block_diag_ones():
    i = jnp.arange(D_A) // HD_A
    return (i[:, None] == i[None, :]).astype(bf16)


LANE = 128


def _pad_lanes(w):
    return jnp.pad(w, ((0, 0), (0, (-w.shape[1]) % LANE)))


def _pad_rows(w):
    return jnp.pad(w, ((0, (-w.shape[0]) % LANE), (0, 0)))


def kernel(x_prompt, x_sample, state_shift, state_wkv, state_pool, cache_kv_w128, cache_kv_w512, cache_kv_w2048,
           norm_mix, norm_ffn, norm_final, a_w_in, a_w_out, a_mu_rkv, a_mu_wag, a_w0, a_w1, a_w2, a_a0, a_a1,
           a_a2, a_g1, a_g2, a_k_k, a_k_a, a_r_k, a_gn_w, a_gn_b, b_w_pool, b_scale, c_w_in, c_w_out, p_w_q,
           p_sub_keys, p_u, p_v):
    T = x_prompt.shape[1]
    NS = x_sample.shape[0]
    TM = 768
    M = -(-(T + NS) // TM) * TM
    S_BLK = T // NS
    assert T % NS == 0 and T % ATTN_BLK == 0
    cb = lambda w: w.astype(bf16)
    row = lambda w: w.reshape(1, -1)

    x0 = jnp.concatenate([x_prompt.reshape(T, D_MODEL), x_sample.reshape(NS, D_MODEL),
                          jnp.zeros((M - T - NS, D_MODEL), f32)], axis=0)
    ub, vt = _peer_prep(p_u, p_v, te=512)
    bd = _block_diag_ones()

    w_in = cb(a_w_in[0])
    z0, h0 = _mm(x0, w_in, norm_w=norm_mix[0], emit_h=f32, tm=TM, tn=1024)
    zs_prev = _mm(state_shift[0], w_in[:, :3 * D_A], tm=NS, tn=1024)
    plist = [a_mu_wag[0], row(a_mu_rkv[0]), row(a_w0[0]), _pad_lanes(cb(a_w1[0])), _pad_rows(cb(a_w2[0])),
             row(a_a0[0]), _pad_lanes(cb(a_a1[0])), _pad_rows(cb(a_a2[0])), cb(a_g1[0]), cb(a_g2[0]),
             row(a_k_k[0]), row(a_k_a[0]), cb(b_w_pool[0]), row(b_scale[0]), bd]
    rp, lwp, kp, vp, kkp, ap, gp, obp = _even_mid_prompt(h0, z0, plist, T=T, tm=256)
    rs, lws, ks, vs, kks, as_, gs, obs = _even_mid_sample(
        h0, state_shift[0], z0, zs_prev, jnp.swapaxes(state_pool[0], 0, 1), plist, row0=S_BLK, n=NS)
    o_p, wkv_p = _wkv_prompt(rp, lwp, kp, vp, kkp, ap, T=T, tb=512)
    o_s, wkv_s = _wkv_sample(state_wkv[0], rs, lws, ks, vs, kks, as_)
    post_c = (cb(a_w_out[0]), row(a_r_k[0]), row(a_gn_w[0]), row(a_gn_b[0]), bd)
    x1 = _even_post(x0, o_p, rp, kp, vp, gp, obp, *post_c, row0=0, rows=T, tm=256)
    x1 = _even_post(x1, o_s, rs, ks, vs, gs, obs, *post_c, row0=S_BLK, rows=NS, tm=NS)
    x1 = _peer(x1, norm_ffn[0], cb(p_w_q[0]), cb(p_sub_keys[0].reshape(2 * PEER_HEADS, N_KEYS, N_KEYS)),
               ub[0], vt[0], tm_mm=TM, tmr=256, tm=TM, te=512)

    z1 = _mm(x1, cb(c_w_in[0]), norm_w=norm_mix[1], tm=TM, tn=1024)
    att_p = _attn_prompt(z1, T=T)
    z1s = z1[T:T + NS]
    att_s = _attn_sample(z1s, (cache_kv_w128[0], cache_kv_w512[0], cache_kv_w2048[0]))
    att = jnp.concatenate([att_p, att_s, jnp.zeros((M - T - NS, D_C), f32)], axis=0)
    x2 = _mm(att, cb(c_w_out[0]), res=x1, tm=TM, tn=1024)
    x2 = _peer(x2, norm_ffn[1], cb(p_w_q[1]), cb(p_sub_keys[1].reshape(2 * PEER_HEADS, N_KEYS, N_KEYS)),
               ub[1], vt[1], tm_mm=TM, tmr=256, tm=TM, te=512)

    y_p = _rmsnorm(x2, norm_final, tm=512, row0=0, rows=T)
    y_s = _rmsnorm(x2, norm_final, tm=NS, row0=S_BLK, rows=NS)

    u_p = z0[T - POOL_BUF:T, 3 * D_A:]
    u_s = z0[T:T + NS, 3 * D_A:]
    pool_s = jnp.concatenate([state_pool[0][:, 1:], u_s[:, None, :]], axis=1)
    kv_p, kv_s = [], []
    for g, (win, _) in enumerate(ATTN_GROUPS):
        n = min(win, T)
        kcol = z1[:, g * 3 * D_C + D_C:g * 3 * D_C + 2 * D_C]
        vcol = z1[:, g * 3 * D_C + 2 * D_C:g * 3 * D_C + 3 * D_C]
        kv = jnp.stack([kcol, vcol], axis=1).reshape(M, 2, H_C, HD_C)
        kv_p.append(kv[T - n:T][None, None])
        kv_s.append(kv[T:T + NS][None, :, None])
    return (y_p[None], y_s[:, None, :],
            h0[T - 1][None, None], h0[T:T + NS][None],
            wkv_p[None, None], wkv_s[None],
            u_p[None, None], pool_s[None],
            kv_p[0], kv_s[0], kv_p[1], kv_s[1], kv_p[2], kv_s[2])
```

```python
import functools
import math

import jax
import jax.numpy as jnp
from jax import lax
from jax.experimental import pallas as pl
from jax.experimental.pallas import tpu as pltpu

f32 = jnp.float32
bf16 = jnp.bfloat16

D_MODEL = 2048
H_A, HD_A = 16, 64
D_A = H_A * HD_A
D_POOL = D_MODEL - D_A
POOL_WINDOWS = (2, 4, 8, 16)
POOL_GD = D_POOL // len(POOL_WINDOWS)
POOL_BUF = max(POOL_WINDOWS) - 1
GN_EPS = 64e-5
NORM_EPS = 1e-6
ATTN_GROUPS = ((128, 1), (512, 4), (2048, 16))
H_C, HD_C = 8, 128
D_C = H_C * HD_C
N_KEYS = 128
N_EXPERTS = N_KEYS * N_KEYS
PEER_HEADS = 8
PEER_TOPK = 16

VMEM_LIMIT = 56 * 1024 * 1024
WKV_CHUNK = 64
ATTN_BLK = 2048
NEG_INF = float("-inf")


def _params(sem, vmem=VMEM_LIMIT):
    return pltpu.CompilerParams(dimension_semantics=sem, vmem_limit_bytes=vmem)


def _dot(a, b):
    return jnp.dot(a, b, preferred_element_type=f32)


def _dot_nt(a, b):
    return lax.dot_general(a, b, (((1,), (1,)), ((), ())), preferred_element_type=f32)


def _dot_tn(a, b):
    return lax.dot_general(a, b, (((0,), (0,)), ((), ())), preferred_element_type=f32)


def _split(x):
    hi = x.astype(bf16)
    lo = (x - hi.astype(f32)).astype(bf16)
    return hi, lo


def _dot3(a, b, dot=_dot):
    ah, al = _split(a)
    bh, bl = _split(b)
    return dot(ah, bh) + (dot(ah, bl) + dot(al, bh))


def _head_sum(x, bd):
    hi, lo = _split(x)
    return _dot(hi, bd) + _dot(lo, bd)


def _gelu(x):
    return 0.5 * x * (1.0 + lax.erf(x * 0.7071067811865476))


def _mm_kernel(*refs, has_norm, has_res, emit_h, transpose_h):
    it = iter(refs)
    x_ref, w_ref = next(it), next(it)
    g_ref = next(it) if has_norm else None
    res_ref = next(it) if has_res else None
    o_ref = next(it)
    h_ref = next(it) if emit_h else None
    xb_ref = next(it)

    @pl.when(pl.program_id(1) == 0)
    def _():
        x = x_ref[...]
        if has_norm:
            ms = jnp.mean(x * x, axis=-1, keepdims=True)
            x = (x * lax.rsqrt(ms + NORM_EPS)) * g_ref[...]
            if emit_h:
                h_ref[...] = (x.T if transpose_h else x).astype(h_ref.dtype)
        xb_ref[...] = x.astype(bf16)

    acc = _dot(xb_ref[...], w_ref[...])
    if has_res:
        acc = acc + res_ref[...]
    o_ref[...] = acc


def _mm(x, w, *, norm_w=None, res=None, emit_h=None, transpose_h=False, tm, tn, row0=0, rows=None, name="proj"):
    K = x.shape[1]
    N = w.shape[1]
    rows = x.shape[0] if rows is None else rows
    assert rows % tm == 0 and N % tn == 0
    in_specs = [pl.BlockSpec((tm, K), lambda i, j: (i + row0, 0)),
                pl.BlockSpec((K, tn), lambda i, j: (0, j))]
    args = [x, w]
    if norm_w is not None:
        in_specs.append(pl.BlockSpec((1, K), lambda i, j: (0, 0)))
        args.append(norm_w.reshape(1, K))
    if res is not None:
        in_specs.append(pl.BlockSpec((tm, tn), lambda i, j: (i + row0, j)))
        args.append(res)
    out_shape = [jax.ShapeDtypeStruct((rows, N), f32)]
    out_specs = [pl.BlockSpec((tm, tn), lambda i, j: (i, j))]
    if emit_h is not None and transpose_h:
        out_shape.append(jax.ShapeDtypeStruct((K, rows), emit_h))
        out_specs.append(pl.BlockSpec((K, tm), lambda i, j: (0, i)))
    elif emit_h is not None:
        out_shape.append(jax.ShapeDtypeStruct((rows, K), emit_h))
        out_specs.append(pl.BlockSpec((tm, K), lambda i, j: (i, 0)))
    outs = pl.pallas_call(
        functools.partial(_mm_kernel, has_norm=norm_w is not None, has_res=res is not None,
                          emit_h=emit_h is not None, transpose_h=transpose_h),
        grid=(rows // tm, N // tn),
        in_specs=in_specs, out_specs=out_specs, out_shape=out_shape,
        scratch_shapes=[pltpu.VMEM((tm, K), bf16)],
        compiler_params=_params(("parallel", "arbitrary")),
        name=name,
    )(*args)
    return outs if emit_h is not None else outs[0]


def _rms_kernel(x_ref, g_ref, o_ref):
    x = x_ref[...]
    ms = jnp.mean(x * x, axis=-1, keepdims=True)
    o_ref[...] = (x * lax.rsqrt(ms + NORM_EPS)) * g_ref[...]


def _rmsnorm(x, g, *, tm, row0, rows):
    K = x.shape[1]
    return pl.pallas_call(
        _rms_kernel, grid=(rows // tm,),
        in_specs=[pl.BlockSpec((tm, K), lambda i: (i + row0, 0)), pl.BlockSpec((1, K), lambda i: (0, 0))],
        out_specs=pl.BlockSpec((tm, K), lambda i: (i, 0)),
        out_shape=jax.ShapeDtypeStruct((rows, K), f32),
        compiler_params=_params(("parallel",)),
        name="rmsnorm",
    )(x, g.reshape(1, K))


def _even_token_math(h, hs, zc, zs, P):
    dh = hs - h
    mu = P["mu_wag"]
    xw = (h + dh * mu[0:1]).astype(bf16)
    xa = (h + dh * mu[1:2]).astype(bf16)
    xg = (h + dh * mu[2:3]).astype(bf16)
    tw = jnp.tanh(_dot(xw, P["w1"])).astype(bf16)
    wl = P["w0"] + _dot(tw, P["w2"])
    w_log = -jax.nn.softplus(-wl) - 0.5
    lw = -jnp.exp(w_log)
    a = jax.nn.sigmoid(P["a0"] + _dot(_dot(xa, P["a1"]).astype(bf16), P["a2"]))
    g = _dot(jax.nn.sigmoid(_dot(xg, P["g1"])).astype(bf16), P["g2"])
    rkv = zc + (zs - zc) * P["mu_rkv"]
    r, k, v = rkv[:, :D_A], rkv[:, D_A:2 * D_A], rkv[:, 2 * D_A:]
    kk = k * P["k_k"]
    nrm = jnp.sqrt(_head_sum(kk * kk, P["bd"]))
    kkn = kk / jnp.maximum(nrm, 1e-12)
    k2 = k * (1.0 + (a - 1.0) * P["k_a"])
    return r, lw, k2, v, kkn, a, g


def _pool_project(pm, P):
    outs = []
    for gi in range(len(POOL_WINDOWS)):
        c = slice(gi * POOL_GD, (gi + 1) * POOL_GD)
        outs.append(_dot(pm[:, c].astype(bf16), P["w_pool"][gi]))
    return jnp.concatenate(outs, axis=-1) * P["pool_scale"]


_EVEN_PARAM_NAMES = ("mu_wag", "mu_rkv", "w0", "w1", "w2", "a0", "a1", "a2", "g1", "g2", "k_k", "k_a",
                     "w_pool", "pool_scale", "bd")


def _load_params(refs):
    return {n: r[...] for n, r in zip(_EVEN_PARAM_NAMES, refs)}


def _even_mid_prompt_kernel(h_ref, hp_ref, z_ref, zp_ref, *rest, tm):
    prefs, outs = rest[:len(_EVEN_PARAM_NAMES)], rest[len(_EVEN_PARAM_NAMES):]
    P = _load_params(prefs)
    first = pl.program_id(0) == 0
    h = h_ref[...]
    z = z_ref[...]
    zc, u = z[:, :3 * D_A], z[:, 3 * D_A:]
    hprev = jnp.where(first, 0.0, hp_ref[15:16, :])
    zprev = jnp.where(first, 0.0, zp_ref[15:16, :3 * D_A])
    row = lax.broadcasted_iota(jnp.int32, (tm, 1), 0)
    hs = jnp.where(row == 0, hprev, pltpu.roll(h, 1, axis=0))
    zs = jnp.where(row == 0, zprev, pltpu.roll(zc, 1, axis=0))
    prow = lax.broadcasted_iota(jnp.int32, (tm, POOL_GD), 0)
    r, lw, k2, v, kkn, a, g = _even_token_math(h, hs, zc, zs, P)

    uprev = jnp.where(first, 0.0, zp_ref[:, 3 * D_A:])
    pos = pl.program_id(0) * tm + prow
    means = []
    for gi, win in enumerate(POOL_WINDOWS):
        c = slice(gi * POOL_GD, (gi + 1) * POOL_GD)
        s = jnp.concatenate([uprev[:, c], u[:, c]], axis=0)
        sh = 1
        while sh < win:
            s = s + pltpu.roll(s, sh, axis=0)
            sh *= 2
        cnt = jnp.minimum(pos + 1, win).astype(f32)
        means.append(s[16:, :] / cnt)
    ob = _pool_project(jnp.concatenate(means, axis=-1) - u, P)
    for o_ref, val in zip(outs, (r, lw, k2, v, kkn, a, g, ob)):
        o_ref[...] = val


def _even_mid_sample_kernel(h_ref, hs_ref, z_ref, zs_ref, buf_ref, *rest):
    prefs, outs = rest[:len(_EVEN_PARAM_NAMES)], rest[len(_EVEN_PARAM_NAMES):]
    P = _load_params(prefs)
    h = h_ref[...]
    z = z_ref[...]
    zc, u = z[:, :3 * D_A], z[:, 3 * D_A:]
    r, lw, k2, v, kkn, a, g = _even_token_math(h, hs_ref[...], zc, zs_ref[...], P)
    means = []
    for gi, win in enumerate(POOL_WINDOWS):
        c = slice(gi * POOL_GD, (gi + 1) * POOL_GD)
        s = u[:, c]
        for j in range(POOL_BUF - (win - 1), POOL_BUF):
            s = s + buf_ref[j, :, c]
        means.append(s / float(win))
    ob = _pool_project(jnp.concatenate(means, axis=-1) - u, P)
    for o_ref, val in zip(outs, (r, lw, k2, v, kkn, a, g, ob)):
        o_ref[...] = val


def _const_spec(shape):
    nd = len(shape)
    return pl.BlockSpec(shape, lambda i, _nd=nd: (0,) * _nd, pipeline_mode=pl.Buffered(1))


def _even_mid_prompt(h_all, z_all, plist, *, T, tm):
    nb = tm // 16
    in_specs = [pl.BlockSpec((tm, D_MODEL), lambda i: (i, 0)),
                pl.BlockSpec((16, D_MODEL), lambda i: (jnp.maximum(i * nb - 1, 0), 0)),
                pl.BlockSpec((tm, 4 * D_A), lambda i: (i, 0)),
                pl.BlockSpec((16, 4 * D_A), lambda i: (jnp.maximum(i * nb - 1, 0), 0))]
    in_specs += [_const_spec(p.shape) for p in plist]
    return pl.pallas_call(
        functools.partial(_even_mid_prompt_kernel, tm=tm),
        grid=(T // tm,), in_specs=in_specs,
        out_specs=[pl.BlockSpec((tm, D_A), lambda i: (i, 0))] * 8,
        out_shape=[jax.ShapeDtypeStruct((T, D_A), f32)] * 8,
        compiler_params=_params(("parallel",)),
        name="even_mid_prompt",
    )(h_all, h_all, z_all, z_all, *plist)


def _even_mid_sample(h_all, hs, z_all, zs, buf_t, plist, *, row0, n):
    in_specs = [pl.BlockSpec((n, D_MODEL), lambda i: (row0, 0)),
                pl.BlockSpec((n, D_MODEL), lambda i: (0, 0)),
                pl.BlockSpec((n, 4 * D_A), lambda i: (row0, 0)),
                pl.BlockSpec((n, 3 * D_A), lambda i: (0, 0)),
                pl.BlockSpec((POOL_BUF, n, D_POOL), lambda i: (0, 0, 0))]
    in_specs += [_const_spec(p.shape) for p in plist]
    return pl.pallas_call(
        _even_mid_sample_kernel, grid=(1,), in_specs=in_specs,
        out_specs=[pl.BlockSpec((n, D_A), lambda i: (0, 0))] * 8,
        out_shape=[jax.ShapeDtypeStruct((n, D_A), f32)] * 8,
        compiler_params=_params(("arbitrary",)),
        name="even_mid_sample",
    )(h_all, hs, z_all, zs, buf_t, *plist)


def _wkv_prompt_kernel(r_ref, lw_ref, k_ref, v_ref, kk_ref, a_ref, o_ref, s_out_ref, S_ref, *, tb, npair):
    C = WKV_CHUNK
    t = pl.program_id(1)

    @pl.when(t == 0)
    def _():
        S_ref[...] = jnp.zeros_like(S_ref)

    ri = lax.broadcasted_iota(jnp.int32, (C, C), 0)
    ci = lax.broadcasted_iota(jnp.int32, (C, C), 1)
    strict = ri > ci
    incl = ri >= ci
    eye = (ri == ci).astype(f32)
    rows = lax.broadcasted_iota(jnp.int32, (C, 2 * HD_A), 0)

    nh = 2 * npair
    heads = range(nh)

    def prep(sl, lanes):
        r, lw, k, v, kk, a = (x[sl, lanes] for x in (r_ref, lw_ref, k_ref, v_ref, kk_ref, a_ref))
        cum = lw
        sh = 1
        while sh < C:
            cum = cum + jnp.where(rows >= sh, pltpu.roll(cum, sh, axis=0), 0.0)
            sh *= 2
        cum_c = cum[C - 1:C, :]
        b = kk * a
        e_neg = jnp.exp(-cum)
        e_rem = jnp.exp(cum_c - cum)
        return dict(at=-kk * jnp.exp(cum - lw), rt=r * jnp.exp(cum), bt=b * e_neg, kt=k * e_neg,
                    bh=b * e_rem, kh=k * e_rem, w_c=jnp.exp(cum_c), v=v)

    def chunk(c, carry):
        off = pl.multiple_of(c * C, C)
        sl = pl.ds(off, C)
        pairs = [prep(sl, slice(p * 2 * HD_A, (p + 1) * 2 * HD_A)) for p in range(npair)]
        S0 = [S_ref[h] for h in heads]

        def head(name, h):
            return pairs[h // 2][name][:, (h % 2) * HD_A:(h % 2 + 1) * HD_A]

        ar = [jnp.concatenate([head("at", h), head("rt", h)], axis=0) for h in heads]
        bk = [jnp.concatenate([head("bt", h), head("kt", h)], axis=0) for h in heads]
        vh = [head("v", h) for h in heads]
        m4 = [_dot3(ar[h], bk[h], _dot_nt) for h in heads]
        sar = [_dot3(ar[h], S0[h], _dot_nt) for h in heads]
        a_ab = [jnp.where(strict, m4[h][:C, :C], 0.0) for h in heads]
        a_ak = [jnp.where(strict, m4[h][:C, C:], 0.0) for h in heads]
        a_rb = [jnp.where(incl, m4[h][C:, :C], 0.0) for h in heads]
        a_rk = [jnp.where(incl, m4[h][C:, C:], 0.0) for h in heads]
        av = [_dot3(jnp.concatenate([a_ak[h], a_rk[h]], axis=0), vh[h]) for h in heads]
        tinv = [eye + a_ab[h] for h in heads]
        pw = [_dot3(a_ab[h], a_ab[h]) for h in heads]
        n_sq = 2
        while n_sq * 2 < C:
            x = [_dot3(pw[h], jnp.concatenate([tinv[h], pw[h]], axis=1)) for h in heads]
            tinv = [tinv[h] + x[h][:, :C] for h in heads]
            pw = [x[h][:, C:] for h in heads]
            n_sq *= 2
        tinv = [tinv[h] + _dot3(pw[h], tinv[h]) for h in heads]
        u = [_dot3(tinv[h], sar[h][:C] + av[h][:C]) for h in heads]
        o = [sar[h][C:] + av[h][C:] + _dot3(a_rb[h], u[h]) for h in heads]
        s_new = []
        for h in heads:
            uv = jnp.concatenate([u[h], vh[h]], axis=0)
            bkh = jnp.concatenate([head("bh", h), head("kh", h)], axis=0)
            s_new.append(S0[h] * head("w_c", h) + _dot3(uv, bkh, _dot_tn))
        for h in heads:
            S_ref[h] = s_new[h]
        for p in range(npair):
            o_ref[sl, p * 2 * HD_A:(p + 1) * 2 * HD_A] = jnp.concatenate([o[2 * p], o[2 * p + 1]], axis=1)
        return carry

    lax.fori_loop(0, tb // C, chunk, 0)

    @pl.when(t == pl.num_programs(1) - 1)
    def _():
        s_out_ref[...] = S_ref[...]


def _wkv_prompt(r, lw, k2, v, kkn, a, *, T, tb, npair):
    spec = pl.BlockSpec((tb, 2 * HD_A * npair), lambda j, t: (t, j))
    nh = 2 * npair
    return pl.pallas_call(
        functools.partial(_wkv_prompt_kernel, tb=tb, npair=npair),
        grid=(H_A // nh, T // tb),
        in_specs=[spec] * 6,
        out_specs=[spec, pl.BlockSpec((nh, HD_A, HD_A), lambda j, t: (j, 0, 0))],
        out_shape=[jax.ShapeDtypeStruct((T, D_A), f32), jax.ShapeDtypeStruct((H_A, HD_A, HD_A), f32)],
        scratch_shapes=[pltpu.VMEM((nh, HD_A, HD_A), f32)],
        compiler_params=_params(("parallel", "arbitrary")),
        name="wkv_prompt",
    )(r, lw, k2, v, kkn, a)


def _wkv_sample_kernel(s_ref, r_ref, lw_ref, k_ref, kk_ref, a_ref, v_ref, o_ref, s_out_ref):
    S = s_ref[0]
    kk = kk_ref[0]
    sa = jnp.sum(S * (-kk), axis=-1, keepdims=True)
    S2 = S * jnp.exp(lw_ref[0]) + sa * (kk * a_ref[0]) + v_ref[0] * k_ref[0]
    s_out_ref[0] = S2
    o_ref[0] = jnp.sum(S2 * r_ref[0], axis=-1, keepdims=True)


def _wkv_sample(S0, r, lw, k2, v, kkn, a):
    n = S0.shape[0]
    row = lambda x: x.reshape(n, H_A, 1, HD_A)
    rspec = pl.BlockSpec((1, H_A, 1, HD_A), lambda b: (b, 0, 0, 0))
    cspec = pl.BlockSpec((1, H_A, HD_A, 1), lambda b: (b, 0, 0, 0))
    sspec = pl.BlockSpec((1, H_A, HD_A, HD_A), lambda b: (b, 0, 0, 0))
    o, S = pl.pallas_call(
        _wkv_sample_kernel, grid=(n,),
        in_specs=[sspec, rspec, rspec, rspec, rspec, rspec, cspec],
        out_specs=[cspec, sspec],
        out_shape=[jax.ShapeDtypeStruct((n, H_A, HD_A, 1), f32), jax.ShapeDtypeStruct(S0.shape, f32)],
        compiler_params=_params(("parallel",)),
        name="wkv_sample",
    )(S0, row(r), row(lw), row(k2), row(kkn), row(a), v.reshape(n, H_A, HD_A, 1))
    return o.reshape(n, D_A), S


def _even_post_kernel(x_ref, o_ref, r_ref, k_ref, v_ref, g_ref, ob_ref, wo_ref, rk_ref, gnw_ref, gnb_ref, bd_ref,
                      out_ref):
    bd = bd_ref[...]
    o = o_ref[...]
    inv = 1.0 / HD_A
    mu = _head_sum(o, bd) * inv
    d = o - mu
    var = _head_sum(d * d, bd) * inv
    on = d * lax.rsqrt(var + GN_EPS) * gnw_ref[...] + gnb_ref[...]
    v = v_ref[...]
    bonus = _head_sum(r_ref[...] * k_ref[...] * rk_ref[...], bd) * v
    oa = ((on + bonus) * g_ref[...]).astype(bf16)
    y = _dot(oa, wo_ref[:D_A, :]) + _dot(ob_ref[...].astype(bf16), wo_ref[D_A:, :])
    out_ref[...] = x_ref[...] + y


def _even_post(x_all, o, r, k2, v, g, ob, wo, rk, gnw, gnb, bd, *, row0, rows, tm):
    a_spec = pl.BlockSpec((tm, D_A), lambda i: (i, 0))
    x_spec = pl.BlockSpec((tm, D_MODEL), lambda i: (i + row0, 0))
    consts = [wo, rk, gnw, gnb, bd]
    return pl.pallas_call(
        _even_post_kernel, grid=(rows // tm,),
        in_specs=[x_spec] + [a_spec] * 6 + [_const_spec(c.shape) for c in consts],
        out_specs=x_spec,
        out_shape=jax.ShapeDtypeStruct(x_all.shape, f32),
        input_output_aliases={0: 0},
        compiler_params=_params(("parallel",)),
        name="even_post",
    )(x_all, o, r, k2, v, g, ob, *consts)


def _attn_prompt_kernel(*refs):
    in_refs, out_ref, og_ref, lg_ref = refs[:15], refs[15], refs[16], refs[17]
    n = pl.program_id(1)
    has_prev = n > 0
    scale = HD_C ** -0.5
    Q = 128
    ri = lax.broadcasted_iota(jnp.int32, (Q, Q), 0)
    ci = lax.broadcasted_iota(jnp.int32, (Q, Q), 1)
    mask_prev0 = ci >= ri
    mask_cur = ci <= ri

    for g, (_, dil) in enumerate(ATTN_GROUPS):
        q_ref, kp_ref, k_ref, vp_ref, v_ref = in_refs[5 * g:5 * g + 5]
        nblk = ATTN_BLK // (Q * dil)

        def unit(u, carry, q_ref=q_ref, kp_ref=kp_ref, k_ref=k_ref, vp_ref=vp_ref, v_ref=v_ref,
                 dil=dil, nblk=nblk, g=g):
            c = u // nblk
            m = u % nblk
            start = m * (Q * dil) + c
            rows = pl.ds(start, Q, stride=dil)
            q = q_ref[rows, :].astype(bf16)
            kc = k_ref[rows, :].astype(bf16)
            vc = v_ref[rows, :].astype(bf16)
            in_blk = m > 0
            st_a = jnp.maximum(m - 1, 0) * (Q * dil) + c
            st_b = (nblk - 1) * (Q * dil) + c
            rows_a = pl.ds(st_a, Q, stride=dil)
            rows_b = pl.ds(st_b, Q, stride=dil)
            kp = jnp.where(in_blk, k_ref[rows_a, :], kp_ref[rows_b, :]).astype(bf16)
            vp = jnp.where(in_blk, v_ref[rows_a, :], vp_ref[rows_b, :]).astype(bf16)
            valid_prev = jnp.logical_or(in_blk, has_prev)
            s_p = _dot_nt(q, kp) * scale
            s_c = _dot_nt(q, kc) * scale
            s_p = jnp.where(jnp.logical_and(mask_prev0, valid_prev), s_p, NEG_INF)
            s_c = jnp.where(mask_cur, s_c, NEG_INF)
            mx = jnp.maximum(jnp.max(s_p, axis=1, keepdims=True), jnp.max(s_c, axis=1, keepdims=True))
            p_p = jnp.exp(s_p - mx)
            p_c = jnp.exp(s_c - mx)
            den = jnp.sum(p_p, axis=1, keepdims=True) + jnp.sum(p_c, axis=1, keepdims=True)
            o = (_dot(p_p.astype(bf16), vp) + _dot(p_c.astype(bf16), vc)) / den
            lse = mx + jnp.log(den)
            og_ref[g, rows, :] = o
            lg_ref[g, rows, :] = jnp.broadcast_to(lse, (Q, HD_C))
            return carry

        lax.fori_loop(0, dil * nblk, unit, 0)

    l0, l1, l2 = lg_ref[0], lg_ref[1], lg_ref[2]
    mx = jnp.maximum(jnp.maximum(l0, l1), l2)
    e0, e1, e2 = jnp.exp(l0 - mx), jnp.exp(l1 - mx), jnp.exp(l2 - mx)
    out_ref[...] = (e0 * og_ref[0] + e1 * og_ref[1] + e2 * og_ref[2]) / (e0 + e1 + e2)


def _attn_prompt(z, *, T):
    nb = T // ATTN_BLK
    in_specs = []
    for g in range(len(ATTN_GROUPS)):
        def col(j, g=g):
            return lambda h, n: (n, g * 3 * H_C + j * H_C + h)

        def col_prev(j, g=g):
            return lambda h, n: (jnp.maximum(n - 1, 0), g * 3 * H_C + j * H_C + h)
        blk = (ATTN_BLK, HD_C)
        in_specs += [pl.BlockSpec(blk, col(0)), pl.BlockSpec(blk, col_prev(1)), pl.BlockSpec(blk, col(1)),
                     pl.BlockSpec(blk, col_prev(2)), pl.BlockSpec(blk, col(2))]
    return pl.pallas_call(
        _attn_prompt_kernel, grid=(H_C, nb),
        in_specs=in_specs,
        out_specs=pl.BlockSpec((ATTN_BLK, HD_C), lambda h, n: (n, h)),
        out_shape=jax.ShapeDtypeStruct((T, D_C), f32),
        scratch_shapes=[pltpu.VMEM((3, ATTN_BLK, HD_C), f32), pltpu.VMEM((3, ATTN_BLK, HD_C), f32)],
        compiler_params=_params(("parallel", "arbitrary")),
        name="attn_prompt",
    )(*([z] * 15))


def _attn_sample_kernel(z_ref, c0_ref, c1_ref, c2_ref, out_ref):
    scale = HD_C ** -0.5
    crefs = (c0_ref, c1_ref, c2_ref)
    for h in range(H_C):
        outs, lses = [], []
        for g in range(len(ATTN_GROUPS)):
            base = g * 3 * D_C + h * HD_C
            q = z_ref[0, :, base:base + HD_C]
            kn = z_ref[0, :, base + D_C:base + D_C + HD_C]
            vn = z_ref[0, :, base + 2 * D_C:base + 2 * D_C + HD_C]
            K = crefs[g][0, :, 0, 0, h, :]
            V = crefs[g][0, :, 0, 1, h, :]
            s_c = jnp.sum(K * q, axis=1, keepdims=True) * scale
            s_n = jnp.sum(kn * q, axis=1, keepdims=True) * scale
            mx = jnp.maximum(jnp.max(s_c, axis=0, keepdims=True), s_n)
            p_c = jnp.exp(s_c - mx)
            p_n = jnp.exp(s_n - mx)
            den = jnp.sum(p_c, axis=0, keepdims=True) + p_n
            outs.append((jnp.sum(p_c * V, axis=0, keepdims=True) + p_n * vn) / den)
            lses.append(mx + jnp.log(den))
        mx = jnp.maximum(jnp.maximum(lses[0], lses[1]), lses[2])
        es = [jnp.exp(l - mx) for l in lses]
        out_ref[0, :, h * HD_C:(h + 1) * HD_C] = (es[0] * outs[0] + es[1] * outs[1] + es[2] * outs[2]) / (
            es[0] + es[1] + es[2])


def _attn_sample(z_s, caches):
    n = z_s.shape[0]
    in_specs = [pl.BlockSpec((1, 1, 9 * D_C), lambda b: (b, 0, 0))]
    args = [z_s.reshape(n, 1, 9 * D_C)]
    for (win, dil), c in zip(ATTN_GROUPS, caches):
        assert c.shape[1] == win
        args.append(c.reshape(n, win // dil, dil, 2, H_C, HD_C))
        in_specs.append(pl.BlockSpec((1, win // dil, 1, 2, H_C, HD_C), lambda b: (b, 0, 0, 0, 0, 0)))
    out = pl.pallas_call(
        _attn_sample_kernel, grid=(n,), in_specs=in_specs,
        out_specs=pl.BlockSpec((1, 1, D_C), lambda b: (b, 0, 0)),
        out_shape=jax.ShapeDtypeStruct((n, 1, D_C), f32),
        compiler_params=_params(("parallel",)),
        name="attn_sample",
    )(*args)
    return out.reshape(n, D_C)


def _top16_rows(s):
    rows = lax.broadcasted_iota(jnp.int32, s.shape, 0)
    work = s
    vals = []
    for _ in range(PEER_TOPK):
        m = jnp.max(work, axis=0, keepdims=True)
        idx = jnp.min(jnp.where(work == m, rows, s.shape[0]), axis=0, keepdims=True)
        work = jnp.where(rows == idx, NEG_INF, work)
        vals.append(m)
    return vals, jnp.logical_and(work == NEG_INF, s != NEG_INF)


def _top16_rows_distinct(ss):
    work = list(ss)
    vals = [[] for _ in ss]
    for _ in range(PEER_TOPK):
        for i in range(len(ss)):
            m = jnp.max(work[i], axis=0, keepdims=True)
            work[i] = jnp.where(work[i] == m, NEG_INF, work[i])
            vals[i].append(m)
    out = []
    for s, w, v in zip(ss, work, vals):
        member = jnp.logical_and(w == NEG_INF, s != NEG_INF)
        count = jnp.sum(member.astype(f32), axis=0, keepdims=True)
        out.append((v, member, count == float(PEER_TOPK)))
    return out


def _peer_route_kernel(q_ref, keys_ref, s1_ref, s2_ref, e1_ref, e2_ref, tau_ref):
    n = q_ref.shape[0]

    def route_head(h, exact):
        if exact:
            top16 = lambda ss: [_top16_rows(s) + (None,) for s in ss]
        else:
            top16 = _top16_rows_distinct
        scores = []
        for p in range(2):
            hp = 2 * h + p
            qs = q_ref[:, hp * N_KEYS:(hp + 1) * N_KEYS].astype(bf16)
            scores.append(_dot_nt(keys_ref[hp], qs))
        sm, vals, oks = [], [], []
        for s, (v, member, ok) in zip(scores, top16(scores)):
            sm.append(jnp.where(member, s, NEG_INF))
            vals.append(v)
            oks.append(ok)
        pieces = []
        for r1 in range(PEER_TOPK):
            cnt = PEER_TOPK // (r1 + 1)
            pieces.append(vals[0][r1] + jnp.concatenate(vals[1][:cnt], axis=0))
        npad = (-sum(p.shape[0] for p in pieces)) % 8
        pieces.append(jnp.full((npad, n), NEG_INF, f32))
        (top, _, ok), = top16([jnp.concatenate(pieces, axis=0)])
        oks.append(ok)
        m = top[0]
        zsum = sum(jnp.exp(t - m) for t in top)
        rows = slice(h * N_KEYS, (h + 1) * N_KEYS)
        s1_ref[rows, :] = sm[0]
        s2_ref[rows, :] = sm[1]
        e1_ref[rows, :] = jnp.exp(sm[0] - vals[0][0])
        e2_ref[rows, :] = jnp.exp(sm[1] - vals[1][0]) / zsum
        tau_ref[h:h + 1, :] = top[-1]
        if exact:
            return None
        all_ok = jnp.logical_and(jnp.logical_and(oks[0], oks[1]), oks[2])
        return jnp.min(all_ok.astype(f32)) > 0.5

    for h in range(PEER_HEADS):
        tie_free = route_head(h, exact=False)

        @pl.when(jnp.logical_not(tie_free))
        def _(h=h):
            route_head(h, exact=True)


def _peer_route(q, keys, *, tmr):
    M = q.shape[0]
    big = jax.ShapeDtypeStruct((PEER_HEADS * N_KEYS, M), f32)
    bspec = pl.BlockSpec((PEER_HEADS * N_KEYS, tmr), lambda i: (0, i))
    return pl.pallas_call(
        _peer_route_kernel, grid=(M // tmr,),
        in_specs=[pl.BlockSpec((tmr, 2 * PEER_HEADS * N_KEYS), lambda i: (i, 0)),
                  _const_spec(keys.shape)],
        out_specs=[bspec] * 4 + [pl.BlockSpec((PEER_HEADS, tmr), lambda i: (0, i))],
        out_shape=[big] * 4 + [jax.ShapeDtypeStruct((PEER_HEADS, M), f32)],
        compiler_params=_params(("parallel",)),
        name="peer_route",
    )(q, keys)


def _peer_prep_kernel(u_ref, v_ref, ub_ref, vt_ref):
    ub_ref[...] = u_ref[...].astype(bf16)
    vt_ref[...] = v_ref[...].T.astype(bf16)


def _peer_prep(u_tab, v_tab, *, te):
    L, E, D = u_tab.shape
    return pl.pallas_call(
        _peer_prep_kernel, grid=(L, E // te),
        in_specs=[pl.BlockSpec((None, te, D), lambda l, j: (l, j, 0))] * 2,
        out_specs=[pl.BlockSpec((None, te, D), lambda l, j: (l, j, 0)),
                   pl.BlockSpec((None, D, te), lambda l, j: (l, 0, j))],
        out_shape=[jax.ShapeDtypeStruct((L, E, D), bf16), jax.ShapeDtypeStruct((L, D, E), bf16)],
        compiler_params=_params(("parallel", "parallel")),
        name="peer_prep",
    )(u_tab, v_tab)


PEER_RB = 32
PEER_LC = 256


def _peer_dense_kernel(xt_ref, u_ref, vt_ref, s1_ref, s2_ref, e1_ref, e2_ref, tau_ref, res_ref, out_ref,
                       acc_ref, *, tm, te):
    j = pl.program_id(1)
    n_i1 = te // N_KEYS
    n_rb = N_KEYS // PEER_RB

    @pl.when(j == 0)
    def _():
        acc_ref[...] = jnp.zeros_like(acc_ref)

    chunks = [slice(c * PEER_LC, (c + 1) * PEER_LC) for c in range(tm // PEER_LC)]
    u = u_ref[...]
    acts = [_dot(u, xt_ref[:, lanes]) for lanes in chunks]
    s1_rows = [[pl.ds(h * N_KEYS + j * n_i1 + ii, 1) for ii in range(n_i1)] for h in range(PEER_HEADS)]
    for lanes, act in zip(chunks, acts):
        pieces = [[None] * n_rb for _ in range(n_i1)]
        for rb in range(n_rb):
            gates = [jnp.zeros((PEER_RB, PEER_LC), f32) for _ in range(n_i1)]
            for h in range(PEER_HEADS):
                rows2 = slice(h * N_KEYS + rb * PEER_RB, h * N_KEYS + (rb + 1) * PEER_RB)
                s2 = s2_ref[rows2, lanes]
                e2 = e2_ref[rows2, lanes]
                tau = tau_ref[h:h + 1, lanes]
                for ii in range(n_i1):
                    c = s2 + s1_ref[s1_rows[h][ii], lanes]
                    gates[ii] = gates[ii] + jnp.where(c >= tau, e2 * e1_ref[s1_rows[h][ii], lanes], 0.0)
            for ii in range(n_i1):
                a = act[ii * N_KEYS + rb * PEER_RB:ii * N_KEYS + (rb + 1) * PEER_RB, :]
                pieces[ii][rb] = (_gelu(a) * gates[ii]).astype(bf16)
        p = jnp.concatenate([pc for row in pieces for pc in row], axis=0)
        acc_ref[:, lanes] += _dot(vt_ref[...], p)

    @pl.when(j == pl.num_programs(1) - 1)
    def _():
        out_ref[...] = acc_ref[...].T + res_ref[...]


def _peer_dense(xt, ub, vt, s1, s2, e1, e2, tau, res, *, tm, te):
    D, M = xt.shape
    E = ub.shape[0]
    once = dict(pipeline_mode=pl.Buffered(1))
    rspec = pl.BlockSpec((PEER_HEADS * N_KEYS, tm), lambda i, j: (0, i), **once)
    return pl.pallas_call(
        functools.partial(_peer_dense_kernel, tm=tm, te=te),
        grid=(M // tm, E // te),
        in_specs=[pl.BlockSpec((D, tm), lambda i, j: (0, i), **once),
                  pl.BlockSpec((te, D), lambda i, j: (j, 0)),
                  pl.BlockSpec((D, te), lambda i, j: (0, j)),
                  rspec, rspec, rspec, rspec,
                  pl.BlockSpec((PEER_HEADS, tm), lambda i, j: (0, i), **once),
                  pl.BlockSpec((tm, D), lambda i, j: (i, 0), **once)],
        out_specs=pl.BlockSpec((tm, D), lambda i, j: (i, 0), **once),
        out_shape=jax.ShapeDtypeStruct((M, D), f32),
        scratch_shapes=[pltpu.VMEM((D, tm), f32)],
        compiler_params=_params(("parallel", "arbitrary")),
        name="peer_dense",
    )(xt, ub, vt, s1, s2, e1, e2, tau, res)


def _peer(x_all, norm_w, wq, keys, ub, vt, *, tm_mm, tmr, tm, te):
    q, ht = _mm(x_all, wq, norm_w=norm_w, emit_h=bf16, transpose_h=True, tm=tm_mm, tn=1024, name="peer_query")
    s1, s2, e1, e2, tau = _peer_route(q, keys, tmr=tmr)
    return _peer_dense(ht, ub, vt, s1, s2, e1, e2, tau, x_all, tm=tm, te=te)


def _block_diag_ones():
    i = jnp.arange(D_A) // HD_A
    return (i[:, None] == i[None, :]).astype(bf16)


LANE = 128


def _pad_lanes(w):
    return jnp.pad(w, ((0, 0), (0, (-w.shape[1]) % LANE)))


def _pad_rows(w):
    return jnp.pad(w, ((0, (-w.shape[0]) % LANE), (0, 0)))


def kernel(x_prompt, x_sample, state_shift, state_wkv, state_pool, cache_kv_w128, cache_kv_w512, cache_kv_w2048,
           norm_mix, norm_ffn, norm_final, a_w_in, a_w_out, a_mu_rkv, a_mu_wag, a_w0, a_w1, a_w2, a_a0, a_a1,
           a_a2, a_g1, a_g2, a_k_k, a_k_a, a_r_k, a_gn_w, a_gn_b, b_w_pool, b_scale, c_w_in, c_w_out, p_w_q,
           p_sub_keys, p_u, p_v):
    T = x_prompt.shape[1]
    NS = x_sample.shape[0]
    TM = 768
    M = -(-(T + NS) // TM) * TM
    S_BLK = T // NS
    assert T % NS == 0 and T % ATTN_BLK == 0
    cb = lambda w: w.astype(bf16)
    row = lambda w: w.reshape(1, -1)

    x0 = jnp.concatenate([x_prompt.reshape(T, D_MODEL), x_sample.reshape(NS, D_MODEL),
                          jnp.zeros((M - T - NS, D_MODEL), f32)], axis=0)
    ub, vt = _peer_prep(p_u, p_v, te=512)
    bd = _block_diag_ones()

    w_in = cb(a_w_in[0])
    z0, h0 = _mm(x0, w_in, norm_w=norm_mix[0], emit_h=f32, tm=TM, tn=1024, name="even_in_proj")
    zs_prev = _mm(state_shift[0], w_in[:, :3 * D_A], tm=NS, tn=1024, name="even_in_proj_state")
    plist = [a_mu_wag[0], row(a_mu_rkv[0]), row(a_w0[0]), _pad_lanes(cb(a_w1[0])), _pad_rows(cb(a_w2[0])),
             row(a_a0[0]), _pad_lanes(cb(a_a1[0])), _pad_rows(cb(a_a2[0])), cb(a_g1[0]), cb(a_g2[0]),
             row(a_k_k[0]), row(a_k_a[0]), cb(b_w_pool[0]), row(b_scale[0]), bd]
    rp, lwp, kp, vp, kkp, ap, gp, obp = _even_mid_prompt(h0, z0, plist, T=T, tm=256)
    rs, lws, ks, vs, kks, as_, gs, obs = _even_mid_sample(
        h0, state_shift[0], z0, zs_prev, jnp.swapaxes(state_pool[0], 0, 1), plist, row0=S_BLK, n=NS)
    o_p, wkv_p = _wkv_prompt(rp, lwp, kp, vp, kkp, ap, T=T, tb=512, npair=8)
    o_s, wkv_s = _wkv_sample(state_wkv[0], rs, lws, ks, vs, kks, as_)
    post_c = (cb(a_w_out[0]), row(a_r_k[0]), row(a_gn_w[0]), row(a_gn_b[0]), bd)
    x1 = _even_post(x0, o_p, rp, kp, vp, gp, obp, *post_c, row0=0, rows=T, tm=256)
    x1 = _even_post(x1, o_s, rs, ks, vs, gs, obs, *post_c, row0=S_BLK, rows=NS, tm=NS)
    x1 = _peer(x1, norm_ffn[0], cb(p_w_q[0]), cb(p_sub_keys[0].reshape(2 * PEER_HEADS, N_KEYS, N_KEYS)),
               ub[0], vt[0], tm_mm=TM, tmr=256, tm=TM, te=512)

    z1 = _mm(x1, cb(c_w_in[0]), norm_w=norm_mix[1], tm=TM, tn=1024, name="odd_in_proj")
    att_p = _attn_prompt(z1, T=T)
    z1s = z1[T:T + NS]
    att_s = _attn_sample(z1s, (cache_kv_w128[0], cache_kv_w512[0], cache_kv_w2048[0]))
    att = jnp.concatenate([att_p, att_s, jnp.zeros((M - T - NS, D_C), f32)], axis=0)
    x2 = _mm(att, cb(c_w_out[0]), res=x1, tm=TM, tn=1024, name="odd_out_proj")
    x2 = _peer(x2, norm_ffn[1], cb(p_w_q[1]), cb(p_sub_keys[1].reshape(2 * PEER_HEADS, N_KEYS, N_KEYS)),
               ub[1], vt[1], tm_mm=TM, tmr=256, tm=TM, te=512)

    y_p = _rmsnorm(x2, norm_final, tm=512, row0=0, rows=T)
    y_s = _rmsnorm(x2, norm_final, tm=NS, row0=S_BLK, rows=NS)

    u_p = z0[T - POOL_BUF:T, 3 * D_A:]
    u_s = z0[T:T + NS, 3 * D_A:]
    pool_s = jnp.concatenate([state_pool[0][:, 1:], u_s[:, None, :]], axis=1)
    kv_p, kv_s = [], []
    for g, (win, _) in enumerate(ATTN_GROUPS):
        n = min(win, T)
        kcol = z1[:, g * 3 * D_C + D_C:g * 3 * D_C + 2 * D_C]
        vcol = z1[:, g * 3 * D_C + 2 * D_C:g * 3 * D_C + 3 * D_C]
        kv = jnp.stack([kcol, vcol], axis=1).reshape(M, 2, H_C, HD_C)
        kv_p.append(kv[T - n:T][None, None])
        kv_s.append(kv[T:T + NS][None, :, None])
    return (y_p[None], y_s[:, None, :],
            h0[T - 1][None, None], h0[T:T + NS][None],
            wkv_p[None, None], wkv_s[None],
            u_p[None, None], pool_s[None],
            kv_p[0], kv_s[0], kv_p[1], kv_s[1], kv_p[2], kv_s[2])
```

```python
import functools
import math

import jax
import jax.numpy as jnp
from jax import lax
from jax.experimental import pallas as pl
from jax.experimental.pallas import tpu as pltpu

f32 = jnp.float32
bf16 = jnp.bfloat16

D_MODEL = 2048
H_A, HD_A = 16, 64
D_A = H_A * HD_A
D_POOL = D_MODEL - D_A
POOL_WINDOWS = (2, 4, 8, 16)
POOL_GD = D_POOL // len(POOL_WINDOWS)
POOL_BUF = max(POOL_WINDOWS) - 1
GN_EPS = 64e-5
NORM_EPS = 1e-6
ATTN_GROUPS = ((128, 1), (512, 4), (2048, 16))
H_C, HD_C = 8, 128
D_C = H_C * HD_C
N_KEYS = 128
N_EXPERTS = N_KEYS * N_KEYS
PEER_HEADS = 8
PEER_TOPK = 16

VMEM_LIMIT = 56 * 1024 * 1024
WKV_CHUNK = 64
ATTN_BLK = 2048
ATTN_UNROLL = 8
NEG_INF = float("-inf")


def _params(sem, vmem=VMEM_LIMIT, flags=None):
    return pltpu.CompilerParams(dimension_semantics=sem, vmem_limit_bytes=vmem, flags=flags)


def _dot(a, b):
    return jnp.dot(a, b, preferred_element_type=f32)


def _dot_nt(a, b):
    return lax.dot_general(a, b, (((1,), (1,)), ((), ())), preferred_element_type=f32)


def _dot_tn(a, b):
    return lax.dot_general(a, b, (((0,), (0,)), ((), ())), preferred_element_type=f32)


def _split(x):
    hi = x.astype(bf16)
    lo = (x - hi.astype(f32)).astype(bf16)
    return hi, lo


def _dot3(a, b, dot=_dot):
    ah, al = _split(a)
    bh, bl = _split(b)
    return dot(ah, bh) + (dot(ah, bl) + dot(al, bh))


def _head_sum(x, bd):
    hi, lo = _split(x)
    return _dot(hi, bd) + _dot(lo, bd)


def _gelu(x):
    return 0.5 * x * (1.0 + lax.erf(x * 0.7071067811865476))


def _mm_kernel(*refs, has_norm, has_res, emit_h, transpose_h):
    it = iter(refs)
    x_ref, w_ref = next(it), next(it)
    g_ref = next(it) if has_norm else None
    res_ref = next(it) if has_res else None
    o_ref = next(it)
    h_ref = next(it) if emit_h else None
    xb_ref = next(it)

    @pl.when(pl.program_id(1) == 0)
    def _():
        x = x_ref[...]
        if has_norm:
            ms = jnp.mean(x * x, axis=-1, keepdims=True)
            x = (x * lax.rsqrt(ms + NORM_EPS)) * g_ref[...]
            if emit_h:
                h_ref[...] = (x.T if transpose_h else x).astype(h_ref.dtype)
        xb_ref[...] = x.astype(bf16)

    acc = _dot(xb_ref[...], w_ref[...])
    if has_res:
        acc = acc + res_ref[...]
    o_ref[...] = acc


def _mm(x, w, *, norm_w=None, res=None, emit_h=None, transpose_h=False, tm, tn, row0=0, rows=None, name="proj"):
    K = x.shape[1]
    N = w.shape[1]
    rows = x.shape[0] if rows is None else rows
    assert rows % tm == 0 and N % tn == 0
    in_specs = [pl.BlockSpec((tm, K), lambda i, j: (i + row0, 0)),
                pl.BlockSpec((K, tn), lambda i, j: (0, j))]
    args = [x, w]
    if norm_w is not None:
        in_specs.append(pl.BlockSpec((1, K), lambda i, j: (0, 0)))
        args.append(norm_w.reshape(1, K))
    if res is not None:
        in_specs.append(pl.BlockSpec((tm, tn), lambda i, j: (i + row0, j)))
        args.append(res)
    out_shape = [jax.ShapeDtypeStruct((rows, N), f32)]
    out_specs = [pl.BlockSpec((tm, tn), lambda i, j: (i, j))]
    if emit_h is not None and transpose_h:
        out_shape.append(jax.ShapeDtypeStruct((K, rows), emit_h))
        out_specs.append(pl.BlockSpec((K, tm), lambda i, j: (0, i)))
    elif emit_h is not None:
        out_shape.append(jax.ShapeDtypeStruct((rows, K), emit_h))
        out_specs.append(pl.BlockSpec((tm, K), lambda i, j: (i, 0)))
    outs = pl.pallas_call(
        functools.partial(_mm_kernel, has_norm=norm_w is not None, has_res=res is not None,
                          emit_h=emit_h is not None, transpose_h=transpose_h),
        grid=(rows // tm, N // tn),
        in_specs=in_specs, out_specs=out_specs, out_shape=out_shape,
        scratch_shapes=[pltpu.VMEM((tm, K), bf16)],
        compiler_params=_params(("parallel", "arbitrary")),
        name=name,
    )(*args)
    return outs if emit_h is not None else outs[0]


def _rms_kernel(x_ref, g_ref, o_ref):
    x = x_ref[...]
    ms = jnp.mean(x * x, axis=-1, keepdims=True)
    o_ref[...] = (x * lax.rsqrt(ms + NORM_EPS)) * g_ref[...]


def _rmsnorm(x, g, *, tm, row0, rows):
    K = x.shape[1]
    return pl.pallas_call(
        _rms_kernel, grid=(rows // tm,),
        in_specs=[pl.BlockSpec((tm, K), lambda i: (i + row0, 0)), pl.BlockSpec((1, K), lambda i: (0, 0))],
        out_specs=pl.BlockSpec((tm, K), lambda i: (i, 0)),
        out_shape=jax.ShapeDtypeStruct((rows, K), f32),
        compiler_params=_params(("parallel",)),
        name="rmsnorm",
    )(x, g.reshape(1, K))


def _even_token_math(h, hs, zc, zs, P):
    dh = hs - h
    mu = P["mu_wag"]
    xw = (h + dh * mu[0:1]).astype(bf16)
    xa = (h + dh * mu[1:2]).astype(bf16)
    xg = (h + dh * mu[2:3]).astype(bf16)
    tw = jnp.tanh(_dot(xw, P["w1"])).astype(bf16)
    wl = P["w0"] + _dot(tw, P["w2"])
    w_log = -jax.nn.softplus(-wl) - 0.5
    lw = -jnp.exp(w_log)
    a = jax.nn.sigmoid(P["a0"] + _dot(_dot(xa, P["a1"]).astype(bf16), P["a2"]))
    g = _dot(jax.nn.sigmoid(_dot(xg, P["g1"])).astype(bf16), P["g2"])
    rkv = zc + (zs - zc) * P["mu_rkv"]
    r, k, v = rkv[:, :D_A], rkv[:, D_A:2 * D_A], rkv[:, 2 * D_A:]
    kk = k * P["k_k"]
    nrm = jnp.sqrt(_head_sum(kk * kk, P["bd"]))
    kkn = kk / jnp.maximum(nrm, 1e-12)
    k2 = k * (1.0 + (a - 1.0) * P["k_a"])
    return r, lw, k2, v, kkn, a, g


def _pool_project(pm, P):
    outs = []
    for gi in range(len(POOL_WINDOWS)):
        c = slice(gi * POOL_GD, (gi + 1) * POOL_GD)
        outs.append(_dot(pm[:, c].astype(bf16), P["w_pool"][gi]))
    return jnp.concatenate(outs, axis=-1) * P["pool_scale"]


_EVEN_PARAM_NAMES = ("mu_wag", "mu_rkv", "w0", "w1", "w2", "a0", "a1", "a2", "g1", "g2", "k_k", "k_a",
                     "w_pool", "pool_scale", "bd")


def _load_params(refs):
    return {n: r[...] for n, r in zip(_EVEN_PARAM_NAMES, refs)}


def _even_mid_prompt_kernel(h_ref, hp_ref, z_ref, zp_ref, *rest, tm):
    prefs, outs = rest[:len(_EVEN_PARAM_NAMES)], rest[len(_EVEN_PARAM_NAMES):]
    P = _load_params(prefs)
    first = pl.program_id(0) == 0
    h = h_ref[...]
    z = z_ref[...]
    zc, u = z[:, :3 * D_A], z[:, 3 * D_A:]
    hprev = jnp.where(first, 0.0, hp_ref[15:16, :])
    zprev = jnp.where(first, 0.0, zp_ref[15:16, :3 * D_A])
    row = lax.broadcasted_iota(jnp.int32, (tm, 1), 0)
    hs = jnp.where(row == 0, hprev, pltpu.roll(h, 1, axis=0))
    zs = jnp.where(row == 0, zprev, pltpu.roll(zc, 1, axis=0))
    prow = lax.broadcasted_iota(jnp.int32, (tm, POOL_GD), 0)
    r, lw, k2, v, kkn, a, g = _even_token_math(h, hs, zc, zs, P)

    uprev = jnp.where(first, 0.0, zp_ref[:, 3 * D_A:])
    pos = pl.program_id(0) * tm + prow
    means = []
    for gi, win in enumerate(POOL_WINDOWS):
        c = slice(gi * POOL_GD, (gi + 1) * POOL_GD)
        s = jnp.concatenate([uprev[:, c], u[:, c]], axis=0)
        sh = 1
        while sh < win:
            s = s + pltpu.roll(s, sh, axis=0)
            sh *= 2
        cnt = jnp.minimum(pos + 1, win).astype(f32)
        means.append(s[16:, :] / cnt)
    ob = _pool_project(jnp.concatenate(means, axis=-1) - u, P)
    for o_ref, val in zip(outs, (r, lw, k2, v, kkn, a, g, ob)):
        o_ref[...] = val


def _even_mid_sample_kernel(h_ref, hs_ref, z_ref, zs_ref, buf_ref, *rest):
    prefs, outs = rest[:len(_EVEN_PARAM_NAMES)], rest[len(_EVEN_PARAM_NAMES):]
    P = _load_params(prefs)
    h = h_ref[...]
    z = z_ref[...]
    zc, u = z[:, :3 * D_A], z[:, 3 * D_A:]
    r, lw, k2, v, kkn, a, g = _even_token_math(h, hs_ref[...], zc, zs_ref[...], P)
    means = []
    for gi, win in enumerate(POOL_WINDOWS):
        c = slice(gi * POOL_GD, (gi + 1) * POOL_GD)
        s = u[:, c]
        for j in range(POOL_BUF - (win - 1), POOL_BUF):
            s = s + buf_ref[j, :, c]
        means.append(s / float(win))
    ob = _pool_project(jnp.concatenate(means, axis=-1) - u, P)
    for o_ref, val in zip(outs, (r, lw, k2, v, kkn, a, g, ob)):
        o_ref[...] = val


def _const_spec(shape):
    nd = len(shape)
    return pl.BlockSpec(shape, lambda i, _nd=nd: (0,) * _nd, pipeline_mode=pl.Buffered(1))


def _even_mid_prompt(h_all, z_all, plist, *, T, tm):
    nb = tm // 16
    in_specs = [pl.BlockSpec((tm, D_MODEL), lambda i: (i, 0)),
                pl.BlockSpec((16, D_MODEL), lambda i: (jnp.maximum(i * nb - 1, 0), 0)),
                pl.BlockSpec((tm, 4 * D_A), lambda i: (i, 0)),
                pl.BlockSpec((16, 4 * D_A), lambda i: (jnp.maximum(i * nb - 1, 0), 0))]
    in_specs += [_const_spec(p.shape) for p in plist]
    return pl.pallas_call(
        functools.partial(_even_mid_prompt_kernel, tm=tm),
        grid=(T // tm,), in_specs=in_specs,
        out_specs=[pl.BlockSpec((tm, D_A), lambda i: (i, 0))] * 8,
        out_shape=[jax.ShapeDtypeStruct((T, D_A), f32)] * 8,
        compiler_params=_params(("parallel",)),
        name="even_mid_prompt",
    )(h_all, h_all, z_all, z_all, *plist)


def _even_mid_sample(h_all, hs, z_all, zs, buf_t, plist, *, row0, n):
    in_specs = [pl.BlockSpec((n, D_MODEL), lambda i: (row0, 0)),
                pl.BlockSpec((n, D_MODEL), lambda i: (0, 0)),
                pl.BlockSpec((n, 4 * D_A), lambda i: (row0, 0)),
                pl.BlockSpec((n, 3 * D_A), lambda i: (0, 0)),
                pl.BlockSpec((POOL_BUF, n, D_POOL), lambda i: (0, 0, 0))]
    in_specs += [_const_spec(p.shape) for p in plist]
    return pl.pallas_call(
        _even_mid_sample_kernel, grid=(1,), in_specs=in_specs,
        out_specs=[pl.BlockSpec((n, D_A), lambda i: (0, 0))] * 8,
        out_shape=[jax.ShapeDtypeStruct((n, D_A), f32)] * 8,
        compiler_params=_params(("arbitrary",)),
        name="even_mid_sample",
    )(h_all, hs, z_all, zs, buf_t, *plist)


def _wkv_prompt_kernel(r_ref, lw_ref, k_ref, v_ref, kk_ref, a_ref, o_ref, s_out_ref, S_ref, *, tb, npair):
    C = WKV_CHUNK
    t = pl.program_id(1)

    @pl.when(t == 0)
    def _():
        S_ref[...] = jnp.zeros_like(S_ref)

    ri = lax.broadcasted_iota(jnp.int32, (C, C), 0)
    ci = lax.broadcasted_iota(jnp.int32, (C, C), 1)
    strict = ri > ci
    incl = ri >= ci
    eye = (ri == ci).astype(f32)
    rows = lax.broadcasted_iota(jnp.int32, (C, 2 * HD_A), 0)

    nh = 2 * npair
    heads = range(nh)

    def prep(sl, lanes):
        r, lw, k, v, kk, a = (x[sl, lanes] for x in (r_ref, lw_ref, k_ref, v_ref, kk_ref, a_ref))
        cum = lw
        sh = 1
        while sh < C:
            cum = cum + jnp.where(rows >= sh, pltpu.roll(cum, sh, axis=0), 0.0)
            sh *= 2
        cum_c = cum[C - 1:C, :]
        b = kk * a
        e_neg = jnp.exp(-cum)
        e_rem = jnp.exp(cum_c - cum)
        return dict(at=-kk * jnp.exp(cum - lw), rt=r * jnp.exp(cum), bt=b * e_neg, kt=k * e_neg,
                    bh=b * e_rem, kh=k * e_rem, w_c=jnp.exp(cum_c), v=v)

    def chunk(c, carry):
        off = pl.multiple_of(c * C, C)
        sl = pl.ds(off, C)
        pairs = [prep(sl, slice(p * 2 * HD_A, (p + 1) * 2 * HD_A)) for p in range(npair)]
        S0 = [S_ref[h] for h in heads]

        def head(name, h):
            return pairs[h // 2][name][:, (h % 2) * HD_A:(h % 2 + 1) * HD_A]

        ar = [jnp.concatenate([head("at", h), head("rt", h)], axis=0) for h in heads]
        bk = [jnp.concatenate([head("bt", h), head("kt", h)], axis=0) for h in heads]
        vh = [head("v", h) for h in heads]
        m4 = [_dot3(ar[h], bk[h], _dot_nt) for h in heads]
        sar = [_dot3(ar[h], S0[h], _dot_nt) for h in heads]
        a_ab = [jnp.where(strict, m4[h][:C, :C], 0.0) for h in heads]
        a_ak = [jnp.where(strict, m4[h][:C, C:], 0.0) for h in heads]
        a_rb = [jnp.where(incl, m4[h][C:, :C], 0.0) for h in heads]
        a_rk = [jnp.where(incl, m4[h][C:, C:], 0.0) for h in heads]
        av = [_dot3(jnp.concatenate([a_ak[h], a_rk[h]], axis=0), vh[h]) for h in heads]
        tinv = [eye + a_ab[h] for h in heads]
        pw = [_dot3(a_ab[h], a_ab[h]) for h in heads]
        n_sq = 2
        while n_sq * 2 < C:
            x = [_dot3(pw[h], jnp.concatenate([tinv[h], pw[h]], axis=1)) for h in heads]
            tinv = [tinv[h] + x[h][:, :C] for h in heads]
            pw = [x[h][:, C:] for h in heads]
            n_sq *= 2
        tinv = [tinv[h] + _dot3(pw[h], tinv[h]) for h in heads]
        u = [_dot3(tinv[h], sar[h][:C] + av[h][:C]) for h in heads]
        o = [sar[h][C:] + av[h][C:] + _dot3(a_rb[h], u[h]) for h in heads]
        s_new = []
        for h in heads:
            uv = jnp.concatenate([u[h], vh[h]], axis=0)
            bkh = jnp.concatenate([head("bh", h), head("kh", h)], axis=0)
            s_new.append(S0[h] * head("w_c", h) + _dot3(uv, bkh, _dot_tn))
        for h in heads:
            S_ref[h] = s_new[h]
        for p in range(npair):
            o_ref[sl, p * 2 * HD_A:(p + 1) * 2 * HD_A] = jnp.concatenate([o[2 * p], o[2 * p + 1]], axis=1)
        return carry

    lax.fori_loop(0, tb // C, chunk, 0)

    @pl.when(t == pl.num_programs(1) - 1)
    def _():
        s_out_ref[...] = S_ref[...]


def _wkv_prompt(r, lw, k2, v, kkn, a, *, T, tb, npair):
    spec = pl.BlockSpec((tb, 2 * HD_A * npair), lambda j, t: (t, j))
    nh = 2 * npair
    return pl.pallas_call(
        functools.partial(_wkv_prompt_kernel, tb=tb, npair=npair),
        grid=(H_A // nh, T // tb),
        in_specs=[spec] * 6,
        out_specs=[spec, pl.BlockSpec((nh, HD_A, HD_A), lambda j, t: (j, 0, 0))],
        out_shape=[jax.ShapeDtypeStruct((T, D_A), f32), jax.ShapeDtypeStruct((H_A, HD_A, HD_A), f32)],
        scratch_shapes=[pltpu.VMEM((nh, HD_A, HD_A), f32)],
        compiler_params=_params(("parallel", "arbitrary")),
        name="wkv_prompt",
    )(r, lw, k2, v, kkn, a)


def _wkv_sample_kernel(s_ref, r_ref, lw_ref, k_ref, kk_ref, a_ref, v_ref, o_ref, s_out_ref):
    S = s_ref[0]
    kk = kk_ref[0]
    sa = jnp.sum(S * (-kk), axis=-1, keepdims=True)
    S2 = S * jnp.exp(lw_ref[0]) + sa * (kk * a_ref[0]) + v_ref[0] * k_ref[0]
    s_out_ref[0] = S2
    o_ref[0] = jnp.sum(S2 * r_ref[0], axis=-1, keepdims=True)


def _wkv_sample(S0, r, lw, k2, v, kkn, a):
    n = S0.shape[0]
    row = lambda x: x.reshape(n, H_A, 1, HD_A)
    rspec = pl.BlockSpec((1, H_A, 1, HD_A), lambda b: (b, 0, 0, 0))
    cspec = pl.BlockSpec((1, H_A, HD_A, 1), lambda b: (b, 0, 0, 0))
    sspec = pl.BlockSpec((1, H_A, HD_A, HD_A), lambda b: (b, 0, 0, 0))
    o, S = pl.pallas_call(
        _wkv_sample_kernel, grid=(n,),
        in_specs=[sspec, rspec, rspec, rspec, rspec, rspec, cspec],
        out_specs=[cspec, sspec],
        out_shape=[jax.ShapeDtypeStruct((n, H_A, HD_A, 1), f32), jax.ShapeDtypeStruct(S0.shape, f32)],
        compiler_params=_params(("parallel",)),
        name="wkv_sample",
    )(S0, row(r), row(lw), row(k2), row(kkn), row(a), v.reshape(n, H_A, HD_A, 1))
    return o.reshape(n, D_A), S


def _even_post_kernel(x_ref, o_ref, r_ref, k_ref, v_ref, g_ref, ob_ref, wo_ref, rk_ref, gnw_ref, gnb_ref, bd_ref,
                      out_ref):
    bd = bd_ref[...]
    o = o_ref[...]
    inv = 1.0 / HD_A
    mu = _head_sum(o, bd) * inv
    d = o - mu
    var = _head_sum(d * d, bd) * inv
    on = d * lax.rsqrt(var + GN_EPS) * gnw_ref[...] + gnb_ref[...]
    v = v_ref[...]
    bonus = _head_sum(r_ref[...] * k_ref[...] * rk_ref[...], bd) * v
    oa = ((on + bonus) * g_ref[...]).astype(bf16)
    y = _dot(oa, wo_ref[:D_A, :]) + _dot(ob_ref[...].astype(bf16), wo_ref[D_A:, :])
    out_ref[...] = x_ref[...] + y


def _even_post(x_all, o, r, k2, v, g, ob, wo, rk, gnw, gnb, bd, *, row0, rows, tm):
    a_spec = pl.BlockSpec((tm, D_A), lambda i: (i, 0))
    x_spec = pl.BlockSpec((tm, D_MODEL), lambda i: (i + row0, 0))
    consts = [wo, rk, gnw, gnb, bd]
    return pl.pallas_call(
        _even_post_kernel, grid=(rows // tm,),
        in_specs=[x_spec] + [a_spec] * 6 + [_const_spec(c.shape) for c in consts],
        out_specs=x_spec,
        out_shape=jax.ShapeDtypeStruct(x_all.shape, f32),
        input_output_aliases={0: 0},
        compiler_params=_params(("parallel",)),
        name="even_post",
    )(x_all, o, r, k2, v, g, ob, *consts)


def _attn_prompt_kernel(*refs):
    in_refs, out_ref, og_ref, lg_ref = refs[:15], refs[15], refs[16], refs[17]
    n = pl.program_id(1)
    has_prev = n > 0
    scale = HD_C ** -0.5
    Q = 128
    ri = lax.broadcasted_iota(jnp.int32, (Q, Q), 0)
    ci = lax.broadcasted_iota(jnp.int32, (Q, Q), 1)
    mask_prev0 = ci >= ri
    mask_cur = ci <= ri

    for g, (_, dil) in enumerate(ATTN_GROUPS):
        q_ref, kp_ref, k_ref, vp_ref, v_ref = in_refs[5 * g:5 * g + 5]
        nblk = ATTN_BLK // (Q * dil)

        def units(it, carry, q_ref=q_ref, kp_ref=kp_ref, k_ref=k_ref, vp_ref=vp_ref, v_ref=v_ref,
                  dil=dil, nblk=nblk, g=g):
            us = [it * ATTN_UNROLL + k for k in range(ATTN_UNROLL)]
            rows, q, kc, vc, kp, vp, valid_prev = [], [], [], [], [], [], []
            for u in us:
                c = u // nblk
                m = u % nblk
                r = pl.ds(m * (Q * dil) + c, Q, stride=dil)
                in_blk = m > 0
                r_a = pl.ds(jnp.maximum(m - 1, 0) * (Q * dil) + c, Q, stride=dil)
                r_b = pl.ds((nblk - 1) * (Q * dil) + c, Q, stride=dil)
                rows.append(r)
                q.append(q_ref[r, :].astype(bf16))
                kc.append(k_ref[r, :].astype(bf16))
                vc.append(v_ref[r, :].astype(bf16))
                kp.append(jnp.where(in_blk, k_ref[r_a, :], kp_ref[r_b, :]).astype(bf16))
                vp.append(jnp.where(in_blk, v_ref[r_a, :], vp_ref[r_b, :]).astype(bf16))
                valid_prev.append(jnp.logical_or(in_blk, has_prev))
            n = range(ATTN_UNROLL)
            s_p = [_dot_nt(q[i], kp[i]) * scale for i in n]
            s_c = [_dot_nt(q[i], kc[i]) * scale for i in n]
            s_p = [jnp.where(jnp.logical_and(mask_prev0, valid_prev[i]), s_p[i], NEG_INF) for i in n]
            s_c = [jnp.where(mask_cur, s_c[i], NEG_INF) for i in n]
            mx = [jnp.maximum(jnp.max(s_p[i], axis=1, keepdims=True), jnp.max(s_c[i], axis=1, keepdims=True))
                  for i in n]
            p_p = [jnp.exp(s_p[i] - mx[i]) for i in n]
            p_c = [jnp.exp(s_c[i] - mx[i]) for i in n]
            den = [jnp.sum(p_p[i], axis=1, keepdims=True) + jnp.sum(p_c[i], axis=1, keepdims=True) for i in n]
            o = [(_dot(p_p[i].astype(bf16), vp[i]) + _dot(p_c[i].astype(bf16), vc[i])) / den[i] for i in n]
            lse = [mx[i] + jnp.log(den[i]) for i in n]
            for i in n:
                og_ref[g, rows[i], :] = o[i]
                lg_ref[g, rows[i], :] = jnp.broadcast_to(lse[i], (Q, HD_C))
            return carry

        lax.fori_loop(0, dil * nblk // ATTN_UNROLL, units, 0)

    l0, l1, l2 = lg_ref[0], lg_ref[1], lg_ref[2]
    mx = jnp.maximum(jnp.maximum(l0, l1), l2)
    e0, e1, e2 = jnp.exp(l0 - mx), jnp.exp(l1 - mx), jnp.exp(l2 - mx)
    out_ref[...] = (e0 * og_ref[0] + e1 * og_ref[1] + e2 * og_ref[2]) / (e0 + e1 + e2)


def _attn_prompt(z, *, T):
    nb = T // ATTN_BLK
    in_specs = []
    for g in range(len(ATTN_GROUPS)):
        def col(j, g=g):
            return lambda h, n: (n, g * 3 * H_C + j * H_C + h)

        def col_prev(j, g=g):
            return lambda h, n: (jnp.maximum(n - 1, 0), g * 3 * H_C + j * H_C + h)
        blk = (ATTN_BLK, HD_C)
        in_specs += [pl.BlockSpec(blk, col(0)), pl.BlockSpec(blk, col_prev(1)), pl.BlockSpec(blk, col(1)),
                     pl.BlockSpec(blk, col_prev(2)), pl.BlockSpec(blk, col(2))]
    return pl.pallas_call(
        _attn_prompt_kernel, grid=(H_C, nb),
        in_specs=in_specs,
        out_specs=pl.BlockSpec((ATTN_BLK, HD_C), lambda h, n: (n, h)),
        out_shape=jax.ShapeDtypeStruct((T, D_C), f32),
        scratch_shapes=[pltpu.VMEM((3, ATTN_BLK, HD_C), f32), pltpu.VMEM((3, ATTN_BLK, HD_C), f32)],
        compiler_params=_params(("parallel", "arbitrary")),
        name="attn_prompt",
    )(*([z] * 15))


def _attn_sample_kernel(z_ref, c0_ref, c1_ref, c2_ref, out_ref):
    scale = HD_C ** -0.5
    crefs = (c0_ref, c1_ref, c2_ref)
    for h in range(H_C):
        outs, lses = [], []
        for g in range(len(ATTN_GROUPS)):
            base = g * 3 * D_C + h * HD_C
            q = z_ref[0, :, base:base + HD_C]
            kn = z_ref[0, :, base + D_C:base + D_C + HD_C]
            vn = z_ref[0, :, base + 2 * D_C:base + 2 * D_C + HD_C]
            K = crefs[g][0, :, 0, 0, h, :]
            V = crefs[g][0, :, 0, 1, h, :]
            s_c = jnp.sum(K * q, axis=1, keepdims=True) * scale
            s_n = jnp.sum(kn * q, axis=1, keepdims=True) * scale
            mx = jnp.maximum(jnp.max(s_c, axis=0, keepdims=True), s_n)
            p_c = jnp.exp(s_c - mx)
            p_n = jnp.exp(s_n - mx)
            den = jnp.sum(p_c, axis=0, keepdims=True) + p_n
            outs.append((jnp.sum(p_c * V, axis=0, keepdims=True) + p_n * vn) / den)
            lses.append(mx + jnp.log(den))
        mx = jnp.maximum(jnp.maximum(lses[0], lses[1]), lses[2])
        es = [jnp.exp(l - mx) for l in lses]
        out_ref[0, :, h * HD_C:(h + 1) * HD_C] = (es[0] * outs[0] + es[1] * outs[1] + es[2] * outs[2]) / (
            es[0] + es[1] + es[2])


def _attn_sample(z_s, caches):
    n = z_s.shape[0]
    in_specs = [pl.BlockSpec((1, 1, 9 * D_C), lambda b: (b, 0, 0))]
    args = [z_s.reshape(n, 1, 9 * D_C)]
    for (win, dil), c in zip(ATTN_GROUPS, caches):
        assert c.shape[1] == win
        args.append(c.reshape(n, win // dil, dil, 2, H_C, HD_C))
        in_specs.append(pl.BlockSpec((1, win // dil, 1, 2, H_C, HD_C), lambda b: (b, 0, 0, 0, 0, 0)))
    out = pl.pallas_call(
        _attn_sample_kernel, grid=(n,), in_specs=in_specs,
        out_specs=pl.BlockSpec((1, 1, D_C), lambda b: (b, 0, 0)),
        out_shape=jax.ShapeDtypeStruct((n, 1, D_C), f32),
        compiler_params=_params(("parallel",)),
        name="attn_sample",
    )(*args)
    return out.reshape(n, D_C)


def _top16_rows(s):
    rows = lax.broadcasted_iota(jnp.int32, s.shape, 0)
    work = s
    vals = []
    for _ in range(PEER_TOPK):
        m = jnp.max(work, axis=0, keepdims=True)
        idx = jnp.min(jnp.where(work == m, rows, s.shape[0]), axis=0, keepdims=True)
        work = jnp.where(rows == idx, NEG_INF, work)
        vals.append(m)
    return vals, jnp.logical_and(work == NEG_INF, s != NEG_INF)


def _top16_rows_distinct(ss):
    work = list(ss)
    vals = [[] for _ in ss]
    for _ in range(PEER_TOPK):
        for i in range(len(ss)):
            m = jnp.max(work[i], axis=0, keepdims=True)
            work[i] = jnp.where(work[i] == m, NEG_INF, work[i])
            vals[i].append(m)
    out = []
    for s, w, v in zip(ss, work, vals):
        member = jnp.logical_and(w == NEG_INF, s != NEG_INF)
        count = jnp.sum(member.astype(f32), axis=0, keepdims=True)
        out.append((v, member, count == float(PEER_TOPK)))
    return out


def _peer_route_kernel(q_ref, keys_ref, s1_ref, s2_ref, e1_ref, e2_ref, tau_ref):
    n = q_ref.shape[0]

    def route_head(h, exact):
        if exact:
            top16 = lambda ss: [_top16_rows(s) + (None,) for s in ss]
        else:
            top16 = _top16_rows_distinct
        scores = []
        for p in range(2):
            hp = 2 * h + p
            qs = q_ref[:, hp * N_KEYS:(hp + 1) * N_KEYS].astype(bf16)
            scores.append(_dot_nt(keys_ref[hp], qs))
        sm, vals, oks = [], [], []
        for s, (v, member, ok) in zip(scores, top16(scores)):
            sm.append(jnp.where(member, s, NEG_INF))
            vals.append(v)
            oks.append(ok)
        pieces = []
        for r1 in range(PEER_TOPK):
            cnt = PEER_TOPK // (r1 + 1)
            pieces.append(vals[0][r1] + jnp.concatenate(vals[1][:cnt], axis=0))
        npad = (-sum(p.shape[0] for p in pieces)) % 8
        pieces.append(jnp.full((npad, n), NEG_INF, f32))
        (top, _, ok), = top16([jnp.concatenate(pieces, axis=0)])
        oks.append(ok)
        m = top[0]
        zsum = sum(jnp.exp(t - m) for t in top)
        rows = slice(h * N_KEYS, (h + 1) * N_KEYS)
        s1_ref[rows, :] = sm[0]
        s2_ref[rows, :] = sm[1]
        e1_ref[rows, :] = jnp.exp(sm[0] - vals[0][0])
        e2_ref[rows, :] = jnp.exp(sm[1] - vals[1][0]) / zsum
        tau_ref[h:h + 1, :] = top[-1]
        if exact:
            return None
        all_ok = jnp.logical_and(jnp.logical_and(oks[0], oks[1]), oks[2])
        return jnp.min(all_ok.astype(f32)) > 0.5

    for h in range(PEER_HEADS):
        tie_free = route_head(h, exact=False)

        @pl.when(jnp.logical_not(tie_free))
        def _(h=h):
            route_head(h, exact=True)


def _peer_route(q, keys, *, tmr):
    M = q.shape[0]
    big = jax.ShapeDtypeStruct((PEER_HEADS * N_KEYS, M), f32)
    bspec = pl.BlockSpec((PEER_HEADS * N_KEYS, tmr), lambda i: (0, i))
    return pl.pallas_call(
        _peer_route_kernel, grid=(M // tmr,),
        in_specs=[pl.BlockSpec((tmr, 2 * PEER_HEADS * N_KEYS), lambda i: (i, 0)),
                  _const_spec(keys.shape)],
        out_specs=[bspec] * 4 + [pl.BlockSpec((PEER_HEADS, tmr), lambda i: (0, i))],
        out_shape=[big] * 4 + [jax.ShapeDtypeStruct((PEER_HEADS, M), f32)],
        compiler_params=_params(("parallel",)),
        name="peer_route",
    )(q, keys)


def _peer_prep_kernel(u_ref, v_ref, ub_ref, vt_ref):
    ub_ref[...] = u_ref[...].astype(bf16)
    vt_ref[...] = v_ref[...].T.astype(bf16)


def _peer_prep(u_tab, v_tab, *, te):
    L, E, D = u_tab.shape
    return pl.pallas_call(
        _peer_prep_kernel, grid=(L, E // te),
        in_specs=[pl.BlockSpec((None, te, D), lambda l, j: (l, j, 0))] * 2,
        out_specs=[pl.BlockSpec((None, te, D), lambda l, j: (l, j, 0)),
                   pl.BlockSpec((None, D, te), lambda l, j: (l, 0, j))],
        out_shape=[jax.ShapeDtypeStruct((L, E, D), bf16), jax.ShapeDtypeStruct((L, D, E), bf16)],
        compiler_params=_params(("parallel", "parallel")),
        name="peer_prep",
    )(u_tab, v_tab)


PEER_RB = 32
PEER_LC = 256


def _peer_dense_kernel(xt_ref, u_ref, vt_ref, s1_ref, s2_ref, e1_ref, e2_ref, tau_ref, res_ref, out_ref,
                       acc_ref, *, tm, te):
    j = pl.program_id(1)
    n_i1 = te // N_KEYS
    n_rb = N_KEYS // PEER_RB

    @pl.when(j == 0)
    def _():
        acc_ref[...] = jnp.zeros_like(acc_ref)

    chunks = [slice(c * PEER_LC, (c + 1) * PEER_LC) for c in range(tm // PEER_LC)]
    u = u_ref[...]
    acts = [_dot(u, xt_ref[:, lanes]) for lanes in chunks]
    s1_rows = [[pl.ds(h * N_KEYS + j * n_i1 + ii, 1) for ii in range(n_i1)] for h in range(PEER_HEADS)]
    for lanes, act in zip(chunks, acts):
        pieces = [[None] * n_rb for _ in range(n_i1)]
        for rb in range(n_rb):
            gates = [jnp.zeros((PEER_RB, PEER_LC), f32) for _ in range(n_i1)]
            for h in range(PEER_HEADS):
                rows2 = slice(h * N_KEYS + rb * PEER_RB, h * N_KEYS + (rb + 1) * PEER_RB)
                s2 = s2_ref[rows2, lanes]
                e2 = e2_ref[rows2, lanes]
                tau = tau_ref[h:h + 1, lanes]
                for ii in range(n_i1):
                    c = s2 + s1_ref[s1_rows[h][ii], lanes]
                    gates[ii] = gates[ii] + jnp.where(c >= tau, e2 * e1_ref[s1_rows[h][ii], lanes], 0.0)
            for ii in range(n_i1):
                a = act[ii * N_KEYS + rb * PEER_RB:ii * N_KEYS + (rb + 1) * PEER_RB, :]
                pieces[ii][rb] = (_gelu(a) * gates[ii]).astype(bf16)
        p = jnp.concatenate([pc for row in pieces for pc in row], axis=0)
        acc_ref[:, lanes] += _dot(vt_ref[...], p)

    @pl.when(j == pl.num_programs(1) - 1)
    def _():
        out_ref[...] = acc_ref[...].T + res_ref[...]


def _peer_dense(xt, ub, vt, s1, s2, e1, e2, tau, res, *, tm, te):
    D, M = xt.shape
    E = ub.shape[0]
    once = dict(pipeline_mode=pl.Buffered(1))
    rspec = pl.BlockSpec((PEER_HEADS * N_KEYS, tm), lambda i, j: (0, i), **once)
    return pl.pallas_call(
        functools.partial(_peer_dense_kernel, tm=tm, te=te),
        grid=(M // tm, E // te),
        in_specs=[pl.BlockSpec((D, tm), lambda i, j: (0, i), **once),
                  pl.BlockSpec((te, D), lambda i, j: (j, 0)),
                  pl.BlockSpec((D, te), lambda i, j: (0, j)),
                  rspec, rspec, rspec, rspec,
                  pl.BlockSpec((PEER_HEADS, tm), lambda i, j: (0, i), **once),
                  pl.BlockSpec((tm, D), lambda i, j: (i, 0), **once)],
        out_specs=pl.BlockSpec((tm, D), lambda i, j: (i, 0), **once),
        out_shape=jax.ShapeDtypeStruct((M, D), f32),
        scratch_shapes=[pltpu.VMEM((D, tm), f32)],
        compiler_params=_params(("parallel", "arbitrary")),
        name="peer_dense",
    )(xt, ub, vt, s1, s2, e1, e2, tau, res)


def _peer(x_all, norm_w, wq, keys, ub, vt, *, tm_mm, tmr, tm, te):
    q, ht = _mm(x_all, wq, norm_w=norm_w, emit_h=bf16, transpose_h=True, tm=tm_mm, tn=1024, name="peer_query")
    s1, s2, e1, e2, tau = _peer_route(q, keys, tmr=tmr)
    return _peer_dense(ht, ub, vt, s1, s2, e1, e2, tau, x_all, tm=tm, te=te)


def _block_diag_ones():
    i = jnp.arange(D_A) // HD_A
    return (i[:, None] == i[None, :]).astype(bf16)


LANE = 128


def _pad_lanes(w):
    return jnp.pad(w, ((0, 0), (0, (-w.shape[1]) % LANE)))


def _pad_rows(w):
    return jnp.pad(w, ((0, (-w.shape[0]) % LANE), (0, 0)))


def kernel(x_prompt, x_sample, state_shift, state_wkv, state_pool, cache_kv_w128, cache_kv_w512, cache_kv_w2048,
           norm_mix, norm_ffn, norm_final, a_w_in, a_w_out, a_mu_rkv, a_mu_wag, a_w0, a_w1, a_w2, a_a0, a_a1,
           a_a2, a_g1, a_g2, a_k_k, a_k_a, a_r_k, a_gn_w, a_gn_b, b_w_pool, b_scale, c_w_in, c_w_out, p_w_q,
           p_sub_keys, p_u, p_v):
    T = x_prompt.shape[1]
    NS = x_sample.shape[0]
    TM = 768
    M = -(-(T + NS) // TM) * TM
    S_BLK = T // NS
    assert T % NS == 0 and T % ATTN_BLK == 0
    cb = lambda w: w.astype(bf16)
    row = lambda w: w.reshape(1, -1)

    x0 = jnp.concatenate([x_prompt.reshape(T, D_MODEL), x_sample.reshape(NS, D_MODEL),
                          jnp.zeros((M - T - NS, D_MODEL), f32)], axis=0)
    ub, vt = _peer_prep(p_u, p_v, te=512)
    bd = _block_diag_ones()

    w_in = cb(a_w_in[0])
    z0, h0 = _mm(x0, w_in, norm_w=norm_mix[0], emit_h=f32, tm=TM, tn=1024, name="even_in_proj")
    zs_prev = _mm(state_shift[0], w_in[:, :3 * D_A], tm=NS, tn=1024, name="even_in_proj_state")
    plist = [a_mu_wag[0], row(a_mu_rkv[0]), row(a_w0[0]), _pad_lanes(cb(a_w1[0])), _pad_rows(cb(a_w2[0])),
             row(a_a0[0]), _pad_lanes(cb(a_a1[0])), _pad_rows(cb(a_a2[0])), cb(a_g1[0]), cb(a_g2[0]),
             row(a_k_k[0]), row(a_k_a[0]), cb(b_w_pool[0]), row(b_scale[0]), bd]
    rp, lwp, kp, vp, kkp, ap, gp, obp = _even_mid_prompt(h0, z0, plist, T=T, tm=256)
    rs, lws, ks, vs, kks, as_, gs, obs = _even_mid_sample(
        h0, state_shift[0], z0, zs_prev, jnp.swapaxes(state_pool[0], 0, 1), plist, row0=S_BLK, n=NS)
    o_p, wkv_p = _wkv_prompt(rp, lwp, kp, vp, kkp, ap, T=T, tb=512, npair=8)
    o_s, wkv_s = _wkv_sample(state_wkv[0], rs, lws, ks, vs, kks, as_)
    post_c = (cb(a_w_out[0]), row(a_r_k[0]), row(a_gn_w[0]), row(a_gn_b[0]), bd)
    x1 = _even_post(x0, o_p, rp, kp, vp, gp, obp, *post_c, row0=0, rows=T, tm=256)
    x1 = _even_post(x1, o_s, rs, ks, vs, gs, obs, *post_c, row0=S_BLK, rows=NS, tm=NS)
    x1 = _peer(x1, norm_ffn[0], cb(p_w_q[0]), cb(p_sub_keys[0].reshape(2 * PEER_HEADS, N_KEYS, N_KEYS)),
               ub[0], vt[0], tm_mm=TM, tmr=256, tm=TM, te=512)

    z1 = _mm(x1, cb(c_w_in[0]), norm_w=norm_mix[1], tm=TM, tn=1024, name="odd_in_proj")
    att_p = _attn_prompt(z1, T=T)
    z1s = z1[T:T + NS]
    att_s = _attn_sample(z1s, (cache_kv_w128[0], cache_kv_w512[0], cache_kv_w2048[0]))
    att = jnp.concatenate([att_p, att_s, jnp.zeros((M - T - NS, D_C), f32)], axis=0)
    x2 = _mm(att, cb(c_w_out[0]), res=x1, tm=TM, tn=1024, name="odd_out_proj")
    x2 = _peer(x2, norm_ffn[1], cb(p_w_q[1]), cb(p_sub_keys[1].reshape(2 * PEER_HEADS, N_KEYS, N_KEYS)),
               ub[1], vt[1], tm_mm=TM, tmr=256, tm=TM, te=512)

    y_p = _rmsnorm(x2, norm_final, tm=512, row0=0, rows=T)
    y_s = _rmsnorm(x2, norm_final, tm=NS, row0=S_BLK, rows=NS)

    u_p = z0[T - POOL_BUF:T, 3 * D_A:]
    u_s = z0[T:T + NS, 3 * D_A:]
    pool_s = jnp.concatenate([state_pool[0][:, 1:], u_s[:, None, :]], axis=1)
    kv_p, kv_s = [], []
    for g, (win, _) in enumerate(ATTN_GROUPS):
        n = min(win, T)
        kv = z1[T - n:T + NS, g * 3 * D_C + D_C:g * 3 * D_C + 3 * D_C].reshape(n + NS, 2, H_C, HD_C)
        kv_p.append(kv[:n][None, None])
        kv_s.append(kv[n:][None, :, None])
    return (y_p[None], y_s[:, None, :],
            h0[T - 1][None, None], h0[T:T + NS][None],
            wkv_p[None, None], wkv_s[None],
            u_p[None, None], pool_s[None],
            kv_p[0], kv_s[0], kv_p[1], kv_s[1], kv_p[2], kv_s[2])
```

```python
import functools
import math

import jax
import jax.numpy as jnp
from jax import lax
from jax.experimental import pallas as pl
from jax.experimental.pallas import tpu as pltpu

f32 = jnp.float32
bf16 = jnp.bfloat16

D_MODEL = 2048
H_A, HD_A = 16, 64
D_A = H_A * HD_A
D_POOL = D_MODEL - D_A
POOL_WINDOWS = (2, 4, 8, 16)
POOL_GD = D_POOL // len(POOL_WINDOWS)
POOL_BUF = max(POOL_WINDOWS) - 1
GN_EPS = 64e-5
NORM_EPS = 1e-6
ATTN_GROUPS = ((128, 1), (512, 4), (2048, 16))
H_C, HD_C = 8, 128
D_C = H_C * HD_C
N_KEYS = 128
N_EXPERTS = N_KEYS * N_KEYS
PEER_HEADS = 8
PEER_TOPK = 16

VMEM_LIMIT = 56 * 1024 * 1024
WKV_CHUNK = 64
ATTN_BLK = 2048
ATTN_UNROLL = 8
NEG_INF = float("-inf")
LOG2E = math.log2(math.e)


def _params(sem, vmem=VMEM_LIMIT, flags=None):
    return pltpu.CompilerParams(dimension_semantics=sem, vmem_limit_bytes=vmem, flags=flags)


def _dot(a, b):
    return jnp.dot(a, b, preferred_element_type=f32)


def _dot_nt(a, b):
    return lax.dot_general(a, b, (((1,), (1,)), ((), ())), preferred_element_type=f32)


def _dot_tn(a, b):
    return lax.dot_general(a, b, (((0,), (0,)), ((), ())), preferred_element_type=f32)


def _split(x):
    hi = x.astype(bf16)
    lo = (x - hi.astype(f32)).astype(bf16)
    return hi, lo


def _dot3(a, b, dot=_dot):
    ah, al = _split(a)
    bh, bl = _split(b)
    return dot(ah, bh) + (dot(ah, bl) + dot(al, bh))


def _head_sum(x, bd):
    hi, lo = _split(x)
    return _dot(hi, bd) + _dot(lo, bd)


def _gelu(x):
    return 0.5 * x * (1.0 + lax.erf(x * 0.7071067811865476))


def _mm_kernel(*refs, has_norm, has_res, emit_h, transpose_h):
    it = iter(refs)
    x_ref, w_ref = next(it), next(it)
    g_ref = next(it) if has_norm else None
    res_ref = next(it) if has_res else None
    o_ref = next(it)
    h_ref = next(it) if emit_h else None
    xb_ref = next(it)

    @pl.when(pl.program_id(1) == 0)
    def _():
        x = x_ref[...]
        if has_norm:
            ms = jnp.mean(x * x, axis=-1, keepdims=True)
            x = (x * lax.rsqrt(ms + NORM_EPS)) * g_ref[...]
            if emit_h:
                h_ref[...] = (x.T if transpose_h else x).astype(h_ref.dtype)
        xb_ref[...] = x.astype(bf16)

    acc = _dot(xb_ref[...], w_ref[...])
    if has_res:
        acc = acc + res_ref[...]
    o_ref[...] = acc


def _mm(x, w, *, norm_w=None, res=None, emit_h=None, transpose_h=False, tm, tn, row0=0, rows=None, name="proj"):
    K = x.shape[1]
    N = w.shape[1]
    rows = x.shape[0] if rows is None else rows
    assert rows % tm == 0 and N % tn == 0
    in_specs = [pl.BlockSpec((tm, K), lambda i, j: (i + row0, 0)),
                pl.BlockSpec((K, tn), lambda i, j: (0, j))]
    args = [x, w]
    if norm_w is not None:
        in_specs.append(pl.BlockSpec((1, K), lambda i, j: (0, 0)))
        args.append(norm_w.reshape(1, K))
    if res is not None:
        in_specs.append(pl.BlockSpec((tm, tn), lambda i, j: (i + row0, j)))
        args.append(res)
    out_shape = [jax.ShapeDtypeStruct((rows, N), f32)]
    out_specs = [pl.BlockSpec((tm, tn), lambda i, j: (i, j))]
    if emit_h is not None and transpose_h:
        out_shape.append(jax.ShapeDtypeStruct((K, rows), emit_h))
        out_specs.append(pl.BlockSpec((K, tm), lambda i, j: (0, i)))
    elif emit_h is not None:
        out_shape.append(jax.ShapeDtypeStruct((rows, K), emit_h))
        out_specs.append(pl.BlockSpec((tm, K), lambda i, j: (i, 0)))
    outs = pl.pallas_call(
        functools.partial(_mm_kernel, has_norm=norm_w is not None, has_res=res is not None,
                          emit_h=emit_h is not None, transpose_h=transpose_h),
        grid=(rows // tm, N // tn),
        in_specs=in_specs, out_specs=out_specs, out_shape=out_shape,
        scratch_shapes=[pltpu.VMEM((tm, K), bf16)],
        compiler_params=_params(("parallel", "arbitrary")),
        name=name,
    )(*args)
    return outs if emit_h is not None else outs[0]


def _rms_kernel(x_ref, g_ref, o_ref):
    x = x_ref[...]
    ms = jnp.mean(x * x, axis=-1, keepdims=True)
    o_ref[...] = (x * lax.rsqrt(ms + NORM_EPS)) * g_ref[...]


def _rmsnorm(x, g, *, tm, row0, rows):
    K = x.shape[1]
    return pl.pallas_call(
        _rms_kernel, grid=(rows // tm,),
        in_specs=[pl.BlockSpec((tm, K), lambda i: (i + row0, 0)), pl.BlockSpec((1, K), lambda i: (0, 0))],
        out_specs=pl.BlockSpec((tm, K), lambda i: (i, 0)),
        out_shape=jax.ShapeDtypeStruct((rows, K), f32),
        compiler_params=_params(("parallel",)),
        name="rmsnorm",
    )(x, g.reshape(1, K))


def _even_token_math(h, hs, zc, zs, P):
    dh = hs - h
    mu = P["mu_wag"]
    xw = (h + dh * mu[0:1]).astype(bf16)
    xa = (h + dh * mu[1:2]).astype(bf16)
    xg = (h + dh * mu[2:3]).astype(bf16)
    tw = jnp.tanh(_dot(xw, P["w1"])).astype(bf16)
    wl = P["w0"] + _dot(tw, P["w2"])
    w_log = -jax.nn.softplus(-wl) - 0.5
    lw = -jnp.exp(w_log)
    a = jax.nn.sigmoid(P["a0"] + _dot(_dot(xa, P["a1"]).astype(bf16), P["a2"]))
    g = _dot(jax.nn.sigmoid(_dot(xg, P["g1"])).astype(bf16), P["g2"])
    rkv = zc + (zs - zc) * P["mu_rkv"]
    r, k, v = rkv[:, :D_A], rkv[:, D_A:2 * D_A], rkv[:, 2 * D_A:]
    kk = k * P["k_k"]
    nrm = jnp.sqrt(_head_sum(kk * kk, P["bd"]))
    kkn = kk / jnp.maximum(nrm, 1e-12)
    k2 = k * (1.0 + (a - 1.0) * P["k_a"])
    return r, lw, k2, v, kkn, a, g


def _pool_project(pm, P):
    outs = []
    for gi in range(len(POOL_WINDOWS)):
        c = slice(gi * POOL_GD, (gi + 1) * POOL_GD)
        outs.append(_dot(pm[:, c].astype(bf16), P["w_pool"][gi]))
    return jnp.concatenate(outs, axis=-1) * P["pool_scale"]


_EVEN_PARAM_NAMES = ("mu_wag", "mu_rkv", "w0", "w1", "w2", "a0", "a1", "a2", "g1", "g2", "k_k", "k_a",
                     "w_pool", "pool_scale", "bd")


def _load_params(refs):
    return {n: r[...] for n, r in zip(_EVEN_PARAM_NAMES, refs)}


def _even_mid_prompt_kernel(h_ref, hp_ref, z_ref, zp_ref, *rest, tm):
    prefs, outs = rest[:len(_EVEN_PARAM_NAMES)], rest[len(_EVEN_PARAM_NAMES):]
    P = _load_params(prefs)
    first = pl.program_id(0) == 0
    h = h_ref[...]
    z = z_ref[...]
    zc, u = z[:, :3 * D_A], z[:, 3 * D_A:]
    hprev = jnp.where(first, 0.0, hp_ref[15:16, :])
    zprev = jnp.where(first, 0.0, zp_ref[15:16, :3 * D_A])
    row = lax.broadcasted_iota(jnp.int32, (tm, 1), 0)
    hs = jnp.where(row == 0, hprev, pltpu.roll(h, 1, axis=0))
    zs = jnp.where(row == 0, zprev, pltpu.roll(zc, 1, axis=0))
    prow = lax.broadcasted_iota(jnp.int32, (tm, POOL_GD), 0)
    r, lw, k2, v, kkn, a, g = _even_token_math(h, hs, zc, zs, P)

    uprev = jnp.where(first, 0.0, zp_ref[:, 3 * D_A:])
    pos = pl.program_id(0) * tm + prow
    means = []
    for gi, win in enumerate(POOL_WINDOWS):
        c = slice(gi * POOL_GD, (gi + 1) * POOL_GD)
        s = jnp.concatenate([uprev[:, c], u[:, c]], axis=0)
        sh = 1
        while sh < win:
            s = s + pltpu.roll(s, sh, axis=0)
            sh *= 2
        cnt = jnp.minimum(pos + 1, win).astype(f32)
        means.append(s[16:, :] / cnt)
    ob = _pool_project(jnp.concatenate(means, axis=-1) - u, P)
    for o_ref, val in zip(outs, (r, lw, k2, v, kkn, a, g, ob)):
        o_ref[...] = val


def _even_mid_sample_kernel(h_ref, hs_ref, z_ref, zs_ref, buf_ref, *rest):
    prefs, outs = rest[:len(_EVEN_PARAM_NAMES)], rest[len(_EVEN_PARAM_NAMES):]
    P = _load_params(prefs)
    h = h_ref[...]
    z = z_ref[...]
    zc, u = z[:, :3 * D_A], z[:, 3 * D_A:]
    r, lw, k2, v, kkn, a, g = _even_token_math(h, hs_ref[...], zc, zs_ref[...], P)
    means = []
    for gi, win in enumerate(POOL_WINDOWS):
        c = slice(gi * POOL_GD, (gi + 1) * POOL_GD)
        s = u[:, c]
        for j in range(POOL_BUF - (win - 1), POOL_BUF):
            s = s + buf_ref[j, :, c]
        means.append(s / float(win))
    ob = _pool_project(jnp.concatenate(means, axis=-1) - u, P)
    for o_ref, val in zip(outs, (r, lw, k2, v, kkn, a, g, ob)):
        o_ref[...] = val


def _const_spec(shape):
    nd = len(shape)
    return pl.BlockSpec(shape, lambda i, _nd=nd: (0,) * _nd, pipeline_mode=pl.Buffered(1))


def _even_mid_prompt(h_all, z_all, plist, *, T, tm):
    nb = tm // 16
    in_specs = [pl.BlockSpec((tm, D_MODEL), lambda i: (i, 0)),
                pl.BlockSpec((16, D_MODEL), lambda i: (jnp.maximum(i * nb - 1, 0), 0)),
                pl.BlockSpec((tm, 4 * D_A), lambda i: (i, 0)),
                pl.BlockSpec((16, 4 * D_A), lambda i: (jnp.maximum(i * nb - 1, 0), 0))]
    in_specs += [_const_spec(p.shape) for p in plist]
    return pl.pallas_call(
        functools.partial(_even_mid_prompt_kernel, tm=tm),
        grid=(T // tm,), in_specs=in_specs,
        out_specs=[pl.BlockSpec((tm, D_A), lambda i: (i, 0))] * 8,
        out_shape=[jax.ShapeDtypeStruct((T, D_A), f32)] * 8,
        compiler_params=_params(("parallel",)),
        name="even_mid_prompt",
    )(h_all, h_all, z_all, z_all, *plist)


def _even_mid_sample(h_all, hs, z_all, zs, buf_t, plist, *, row0, n):
    in_specs = [pl.BlockSpec((n, D_MODEL), lambda i: (row0, 0)),
                pl.BlockSpec((n, D_MODEL), lambda i: (0, 0)),
                pl.BlockSpec((n, 4 * D_A), lambda i: (row0, 0)),
                pl.BlockSpec((n, 3 * D_A), lambda i: (0, 0)),
                pl.BlockSpec((POOL_BUF, n, D_POOL), lambda i: (0, 0, 0))]
    in_specs += [_const_spec(p.shape) for p in plist]
    return pl.pallas_call(
        _even_mid_sample_kernel, grid=(1,), in_specs=in_specs,
        out_specs=[pl.BlockSpec((n, D_A), lambda i: (0, 0))] * 8,
        out_shape=[jax.ShapeDtypeStruct((n, D_A), f32)] * 8,
        compiler_params=_params(("arbitrary",)),
        name="even_mid_sample",
    )(h_all, hs, z_all, zs, buf_t, *plist)


def _wkv_prompt_kernel(r_ref, lw_ref, k_ref, v_ref, kk_ref, a_ref, o_ref, s_out_ref, S_ref, *, tb, npair):
    C = WKV_CHUNK
    t = pl.program_id(1)

    @pl.when(t == 0)
    def _():
        S_ref[...] = jnp.zeros_like(S_ref)

    ri = lax.broadcasted_iota(jnp.int32, (C, C), 0)
    ci = lax.broadcasted_iota(jnp.int32, (C, C), 1)
    strict = ri > ci
    incl = ri >= ci
    eye = (ri == ci).astype(f32)
    rows = lax.broadcasted_iota(jnp.int32, (C, 2 * HD_A), 0)

    nh = 2 * npair
    heads = range(nh)

    def prep(sl, lanes):
        r, lw, k, v, kk, a = (x[sl, lanes] for x in (r_ref, lw_ref, k_ref, v_ref, kk_ref, a_ref))
        cum = lw
        sh = 1
        while sh < C:
            cum = cum + jnp.where(rows >= sh, pltpu.roll(cum, sh, axis=0), 0.0)
            sh *= 2
        cum_c = cum[C - 1:C, :]
        b = kk * a
        e_neg = jnp.exp(-cum)
        e_rem = jnp.exp(cum_c - cum)
        return dict(at=-kk * jnp.exp(cum - lw), rt=r * jnp.exp(cum), bt=b * e_neg, kt=k * e_neg,
                    bh=b * e_rem, kh=k * e_rem, w_c=jnp.exp(cum_c), v=v)

    def chunk(c, carry):
        off = pl.multiple_of(c * C, C)
        sl = pl.ds(off, C)
        pairs = [prep(sl, slice(p * 2 * HD_A, (p + 1) * 2 * HD_A)) for p in range(npair)]
        S0 = [S_ref[h] for h in heads]

        def head(name, h):
            return pairs[h // 2][name][:, (h % 2) * HD_A:(h % 2 + 1) * HD_A]

        ar = [jnp.concatenate([head("at", h), head("rt", h)], axis=0) for h in heads]
        bk = [jnp.concatenate([head("bt", h), head("kt", h)], axis=0) for h in heads]
        vh = [head("v", h) for h in heads]
        m4 = [_dot3(ar[h], bk[h], _dot_nt) for h in heads]
        sar = [_dot3(ar[h], S0[h], _dot_nt) for h in heads]
        a_ab = [jnp.where(strict, m4[h][:C, :C], 0.0) for h in heads]
        a_ak = [jnp.where(strict, m4[h][:C, C:], 0.0) for h in heads]
        a_rb = [jnp.where(incl, m4[h][C:, :C], 0.0) for h in heads]
        a_rk = [jnp.where(incl, m4[h][C:, C:], 0.0) for h in heads]
        av = [_dot3(jnp.concatenate([a_ak[h], a_rk[h]], axis=0), vh[h]) for h in heads]
        tinv = [eye + a_ab[h] for h in heads]
        pw = [_dot3(a_ab[h], a_ab[h]) for h in heads]
        n_sq = 2
        while n_sq * 2 < C:
            x = [_dot3(pw[h], jnp.concatenate([tinv[h], pw[h]], axis=1)) for h in heads]
            tinv = [tinv[h] + x[h][:, :C] for h in heads]
            pw = [x[h][:, C:] for h in heads]
            n_sq *= 2
        tinv = [tinv[h] + _dot3(pw[h], tinv[h]) for h in heads]
        u = [_dot3(tinv[h], sar[h][:C] + av[h][:C]) for h in heads]
        o = [sar[h][C:] + av[h][C:] + _dot3(a_rb[h], u[h]) for h in heads]
        s_new = []
        for h in heads:
            uv = jnp.concatenate([u[h], vh[h]], axis=0)
            bkh = jnp.concatenate([head("bh", h), head("kh", h)], axis=0)
            s_new.append(S0[h] * head("w_c", h) + _dot3(uv, bkh, _dot_tn))
        for h in heads:
            S_ref[h] = s_new[h]
        for p in range(npair):
            o_ref[sl, p * 2 * HD_A:(p + 1) * 2 * HD_A] = jnp.concatenate([o[2 * p], o[2 * p + 1]], axis=1)
        return carry

    lax.fori_loop(0, tb // C, chunk, 0)

    @pl.when(t == pl.num_programs(1) - 1)
    def _():
        s_out_ref[...] = S_ref[...]


def _wkv_prompt(r, lw, k2, v, kkn, a, *, T, tb, npair):
    spec = pl.BlockSpec((tb, 2 * HD_A * npair), lambda j, t: (t, j))
    nh = 2 * npair
    return pl.pallas_call(
        functools.partial(_wkv_prompt_kernel, tb=tb, npair=npair),
        grid=(H_A // nh, T // tb),
        in_specs=[spec] * 6,
        out_specs=[spec, pl.BlockSpec((nh, HD_A, HD_A), lambda j, t: (j, 0, 0))],
        out_shape=[jax.ShapeDtypeStruct((T, D_A), f32), jax.ShapeDtypeStruct((H_A, HD_A, HD_A), f32)],
        scratch_shapes=[pltpu.VMEM((nh, HD_A, HD_A), f32)],
        compiler_params=_params(("parallel", "arbitrary")),
        name="wkv_prompt",
    )(r, lw, k2, v, kkn, a)


def _wkv_sample_kernel(s_ref, r_ref, lw_ref, k_ref, kk_ref, a_ref, v_ref, o_ref, s_out_ref):
    S = s_ref[0]
    kk = kk_ref[0]
    sa = jnp.sum(S * (-kk), axis=-1, keepdims=True)
    S2 = S * jnp.exp(lw_ref[0]) + sa * (kk * a_ref[0]) + v_ref[0] * k_ref[0]
    s_out_ref[0] = S2
    o_ref[0] = jnp.sum(S2 * r_ref[0], axis=-1, keepdims=True)


def _wkv_sample(S0, r, lw, k2, v, kkn, a):
    n = S0.shape[0]
    row = lambda x: x.reshape(n, H_A, 1, HD_A)
    rspec = pl.BlockSpec((1, H_A, 1, HD_A), lambda b: (b, 0, 0, 0))
    cspec = pl.BlockSpec((1, H_A, HD_A, 1), lambda b: (b, 0, 0, 0))
    sspec = pl.BlockSpec((1, H_A, HD_A, HD_A), lambda b: (b, 0, 0, 0))
    o, S = pl.pallas_call(
        _wkv_sample_kernel, grid=(n,),
        in_specs=[sspec, rspec, rspec, rspec, rspec, rspec, cspec],
        out_specs=[cspec, sspec],
        out_shape=[jax.ShapeDtypeStruct((n, H_A, HD_A, 1), f32), jax.ShapeDtypeStruct(S0.shape, f32)],
        compiler_params=_params(("parallel",)),
        name="wkv_sample",
    )(S0, row(r), row(lw), row(k2), row(kkn), row(a), v.reshape(n, H_A, HD_A, 1))
    return o.reshape(n, D_A), S


def _even_post_kernel(x_ref, o_ref, r_ref, k_ref, v_ref, g_ref, ob_ref, wo_ref, rk_ref, gnw_ref, gnb_ref, bd_ref,
                      out_ref):
    bd = bd_ref[...]
    o = o_ref[...]
    inv = 1.0 / HD_A
    mu = _head_sum(o, bd) * inv
    d = o - mu
    var = _head_sum(d * d, bd) * inv
    on = d * lax.rsqrt(var + GN_EPS) * gnw_ref[...] + gnb_ref[...]
    v = v_ref[...]
    bonus = _head_sum(r_ref[...] * k_ref[...] * rk_ref[...], bd) * v
    oa = ((on + bonus) * g_ref[...]).astype(bf16)
    y = _dot(oa, wo_ref[:D_A, :]) + _dot(ob_ref[...].astype(bf16), wo_ref[D_A:, :])
    out_ref[...] = x_ref[...] + y


def _even_post(x_all, o, r, k2, v, g, ob, wo, rk, gnw, gnb, bd, *, row0, rows, tm):
    a_spec = pl.BlockSpec((tm, D_A), lambda i: (i, 0))
    x_spec = pl.BlockSpec((tm, D_MODEL), lambda i: (i + row0, 0))
    consts = [wo, rk, gnw, gnb, bd]
    return pl.pallas_call(
        _even_post_kernel, grid=(rows // tm,),
        in_specs=[x_spec] + [a_spec] * 6 + [_const_spec(c.shape) for c in consts],
        out_specs=x_spec,
        out_shape=jax.ShapeDtypeStruct(x_all.shape, f32),
        input_output_aliases={0: 0},
        compiler_params=_params(("parallel",)),
        name="even_post",
    )(x_all, o, r, k2, v, g, ob, *consts)


def _attn_prompt_kernel(*refs):
    in_refs, out_ref, og_ref, lg_ref = refs[:15], refs[15], refs[16], refs[17]
    n = pl.program_id(1)
    has_prev = n > 0
    scale = HD_C ** -0.5
    Q = 128
    ri = lax.broadcasted_iota(jnp.int32, (Q, Q), 0)
    ci = lax.broadcasted_iota(jnp.int32, (Q, Q), 1)
    mask_prev0 = ci >= ri
    mask_cur = ci <= ri

    for g, (_, dil) in enumerate(ATTN_GROUPS):
        q_ref, kp_ref, k_ref, vp_ref, v_ref = in_refs[5 * g:5 * g + 5]
        nblk = ATTN_BLK // (Q * dil)

        def units(it, carry, q_ref=q_ref, kp_ref=kp_ref, k_ref=k_ref, vp_ref=vp_ref, v_ref=v_ref,
                  dil=dil, nblk=nblk, g=g):
            us = [it * ATTN_UNROLL + k for k in range(ATTN_UNROLL)]
            rows, q, kc, vc, kp, vp, valid_prev = [], [], [], [], [], [], []
            for u in us:
                c = u // nblk
                m = u % nblk
                r = pl.ds(m * (Q * dil) + c, Q, stride=dil)
                in_blk = m > 0
                r_a = pl.ds(jnp.maximum(m - 1, 0) * (Q * dil) + c, Q, stride=dil)
                r_b = pl.ds((nblk - 1) * (Q * dil) + c, Q, stride=dil)
                rows.append(r)
                q.append(q_ref[r, :].astype(bf16))
                kc.append(k_ref[r, :].astype(bf16))
                vc.append(v_ref[r, :].astype(bf16))
                kp.append(jnp.where(in_blk, k_ref[r_a, :], kp_ref[r_b, :]).astype(bf16))
                vp.append(jnp.where(in_blk, v_ref[r_a, :], vp_ref[r_b, :]).astype(bf16))
                valid_prev.append(jnp.logical_or(in_blk, has_prev))
            n = range(ATTN_UNROLL)
            s_p = [_dot_nt(q[i], kp[i]) * scale for i in n]
            s_c = [_dot_nt(q[i], kc[i]) * scale for i in n]
            s_p = [jnp.where(jnp.logical_and(mask_prev0, valid_prev[i]), s_p[i], NEG_INF) for i in n]
            s_c = [jnp.where(mask_cur, s_c[i], NEG_INF) for i in n]
            mx = [jnp.maximum(jnp.max(s_p[i], axis=1, keepdims=True), jnp.max(s_c[i], axis=1, keepdims=True))
                  for i in n]
            p_p = [jnp.exp(s_p[i] - mx[i]) for i in n]
            p_c = [jnp.exp(s_c[i] - mx[i]) for i in n]
            den = [jnp.sum(p_p[i], axis=1, keepdims=True) + jnp.sum(p_c[i], axis=1, keepdims=True) for i in n]
            o = [(_dot(p_p[i].astype(bf16), vp[i]) + _dot(p_c[i].astype(bf16), vc[i])) / den[i] for i in n]
            lse = [mx[i] + jnp.log(den[i]) for i in n]
            for i in n:
                og_ref[g, rows[i], :] = o[i]
                lg_ref[g, rows[i], :] = jnp.broadcast_to(lse[i], (Q, HD_C))
            return carry

        lax.fori_loop(0, dil * nblk // ATTN_UNROLL, units, 0)

    l0, l1, l2 = lg_ref[0], lg_ref[1], lg_ref[2]
    mx = jnp.maximum(jnp.maximum(l0, l1), l2)
    e0, e1, e2 = jnp.exp(l0 - mx), jnp.exp(l1 - mx), jnp.exp(l2 - mx)
    out_ref[...] = (e0 * og_ref[0] + e1 * og_ref[1] + e2 * og_ref[2]) / (e0 + e1 + e2)


def _attn_prompt(z, *, T):
    nb = T // ATTN_BLK
    in_specs = []
    for g in range(len(ATTN_GROUPS)):
        def col(j, g=g):
            return lambda h, n: (n, g * 3 * H_C + j * H_C + h)

        def col_prev(j, g=g):
            return lambda h, n: (jnp.maximum(n - 1, 0), g * 3 * H_C + j * H_C + h)
        blk = (ATTN_BLK, HD_C)
        in_specs += [pl.BlockSpec(blk, col(0)), pl.BlockSpec(blk, col_prev(1)), pl.BlockSpec(blk, col(1)),
                     pl.BlockSpec(blk, col_prev(2)), pl.BlockSpec(blk, col(2))]
    return pl.pallas_call(
        _attn_prompt_kernel, grid=(H_C, nb),
        in_specs=in_specs,
        out_specs=pl.BlockSpec((ATTN_BLK, HD_C), lambda h, n: (n, h)),
        out_shape=jax.ShapeDtypeStruct((T, D_C), f32),
        scratch_shapes=[pltpu.VMEM((3, ATTN_BLK, HD_C), f32), pltpu.VMEM((3, ATTN_BLK, HD_C), f32)],
        compiler_params=_params(("parallel", "arbitrary")),
        name="attn_prompt",
    )(*([z] * 15))


def _attn_sample_kernel(z_ref, c0_ref, c1_ref, c2_ref, out_ref):
    scale = HD_C ** -0.5
    crefs = (c0_ref, c1_ref, c2_ref)
    for h in range(H_C):
        outs, lses = [], []
        for g in range(len(ATTN_GROUPS)):
            base = g * 3 * D_C + h * HD_C
            q = z_ref[0, :, base:base + HD_C]
            kn = z_ref[0, :, base + D_C:base + D_C + HD_C]
            vn = z_ref[0, :, base + 2 * D_C:base + 2 * D_C + HD_C]
            K = crefs[g][0, :, 0, 0, h, :]
            V = crefs[g][0, :, 0, 1, h, :]
            s_c = jnp.sum(K * q, axis=1, keepdims=True) * scale
            s_n = jnp.sum(kn * q, axis=1, keepdims=True) * scale
            mx = jnp.maximum(jnp.max(s_c, axis=0, keepdims=True), s_n)
            p_c = jnp.exp(s_c - mx)
            p_n = jnp.exp(s_n - mx)
            den = jnp.sum(p_c, axis=0, keepdims=True) + p_n
            outs.append((jnp.sum(p_c * V, axis=0, keepdims=True) + p_n * vn) / den)
            lses.append(mx + jnp.log(den))
        mx = jnp.maximum(jnp.maximum(lses[0], lses[1]), lses[2])
        es = [jnp.exp(l - mx) for l in lses]
        out_ref[0, :, h * HD_C:(h + 1) * HD_C] = (es[0] * outs[0] + es[1] * outs[1] + es[2] * outs[2]) / (
            es[0] + es[1] + es[2])


def _attn_sample(z_s, caches):
    n = z_s.shape[0]
    in_specs = [pl.BlockSpec((1, 1, 9 * D_C), lambda b: (b, 0, 0))]
    args = [z_s.reshape(n, 1, 9 * D_C)]
    for (win, dil), c in zip(ATTN_GROUPS, caches):
        assert c.shape[1] == win
        args.append(c.reshape(n, win // dil, dil, 2, H_C, HD_C))
        in_specs.append(pl.BlockSpec((1, win // dil, 1, 2, H_C, HD_C), lambda b: (b, 0, 0, 0, 0, 0)))
    out = pl.pallas_call(
        _attn_sample_kernel, grid=(n,), in_specs=in_specs,
        out_specs=pl.BlockSpec((1, 1, D_C), lambda b: (b, 0, 0)),
        out_shape=jax.ShapeDtypeStruct((n, 1, D_C), f32),
        compiler_params=_params(("parallel",)),
        name="attn_sample",
    )(*args)
    return out.reshape(n, D_C)


def _top16_rows(s):
    rows = lax.broadcasted_iota(jnp.int32, s.shape, 0)
    work = s
    vals = []
    for _ in range(PEER_TOPK):
        m = jnp.max(work, axis=0, keepdims=True)
        idx = jnp.min(jnp.where(work == m, rows, s.shape[0]), axis=0, keepdims=True)
        work = jnp.where(rows == idx, NEG_INF, work)
        vals.append(m)
    return vals, jnp.logical_and(work == NEG_INF, s != NEG_INF)


def _oddeven_mergesort_pairs(n):
    pairs = []

    def merge(lo, hi, r):
        step = r * 2
        if step < hi - lo:
            merge(lo, hi, step)
            merge(lo + r, hi, step)
            pairs.extend((i, i + r) for i in range(lo + r, hi - r, step))
        else:
            pairs.append((lo, lo + r))

    def sort(lo, hi):
        if hi - lo >= 1:
            mid = lo + (hi - lo) // 2
            sort(lo, mid)
            sort(mid + 1, hi)
            merge(lo, hi, 1)

    sort(0, n - 1)
    return pairs


SUBLANES = 8


def _top16_sorted(s):
    assert s.shape[0] == PEER_TOPK * SUBLANES
    t = [s[SUBLANES * k:SUBLANES * (k + 1), :] for k in range(PEER_TOPK)]
    for i, j in _oddeven_mergesort_pairs(PEER_TOPK):
        t[i], t[j] = jnp.maximum(t[i], t[j]), jnp.minimum(t[i], t[j])
    shift = 1
    while shift < SUBLANES:
        other = [pltpu.roll(x, shift, axis=0) for x in t]
        t = [jnp.maximum(t[i], other[PEER_TOPK - 1 - i]) for i in range(PEER_TOPK)]
        d = PEER_TOPK // 2
        while d >= 1:
            for i in range(PEER_TOPK):
                if i & d == 0:
                    t[i], t[i + d] = jnp.maximum(t[i], t[i + d]), jnp.minimum(t[i], t[i + d])
            d //= 2
        shift *= 2
    thr = t[PEER_TOPK - 1][0:1, :]
    member = s >= thr
    count = jnp.sum(member.astype(f32), axis=0, keepdims=True)
    return [x[0:1, :] for x in t], member, count == float(PEER_TOPK)


def _top16_rows_distinct(ss):
    work = list(ss)
    vals = [[] for _ in ss]
    for _ in range(PEER_TOPK):
        for i in range(len(ss)):
            m = jnp.max(work[i], axis=0, keepdims=True)
            work[i] = jnp.where(work[i] == m, NEG_INF, work[i])
            vals[i].append(m)
    out = []
    for s, w, v in zip(ss, work, vals):
        member = jnp.logical_and(w == NEG_INF, s != NEG_INF)
        count = jnp.sum(member.astype(f32), axis=0, keepdims=True)
        out.append((v, member, count == float(PEER_TOPK)))
    return out


def _peer_route_kernel(q_ref, keys_ref, s1_ref, s2_ref, tau_ref):
    n = q_ref.shape[0]

    def route_head(h, exact):
        if exact:
            top16 = lambda ss: [_top16_rows(s) + (None,) for s in ss]
            top16_scores = top16
        else:
            top16 = _top16_rows_distinct
            top16_scores = lambda ss: [_top16_sorted(s) for s in ss]
        scores = []
        for p in range(2):
            hp = 2 * h + p
            qs = q_ref[:, hp * N_KEYS:(hp + 1) * N_KEYS].astype(bf16)
            scores.append(_dot_nt(keys_ref[hp], qs) * LOG2E)
        sm, vals, oks = [], [], []
        for s, (v, member, ok) in zip(scores, top16_scores(scores)):
            sm.append(jnp.where(member, s, NEG_INF))
            vals.append(v)
            oks.append(ok)

        def pair_sums(z1, z2):
            pieces = []
            for r1 in range(PEER_TOPK):
                cnt = PEER_TOPK // (r1 + 1)
                pieces.append(z1[r1] + jnp.concatenate(z2[:cnt], axis=0))
            npad = (-sum(p.shape[0] for p in pieces)) % 8
            pieces.append(jnp.full((npad, n), NEG_INF, f32))
            return jnp.concatenate(pieces, axis=0)

        z1 = [v - vals[0][0] for v in vals[0]]
        z2 = [v - vals[1][0] for v in vals[1]]
        (top, chosen, ok), = top16([pair_sums(z1, z2)])
        oks.append(ok)
        log_z = jnp.log2(sum(jnp.exp2(t) for t in top))
        z1 = [z - log_z for z in z1]
        sums = pair_sums(z1, z2)
        if exact:
            (top, _, _), = top16([sums])
            tau = top[-1]
        else:
            tau = jnp.min(jnp.where(chosen, sums, jnp.inf), axis=0, keepdims=True)
            count = jnp.sum((sums >= tau).astype(f32), axis=0, keepdims=True)
            oks.append(count == float(PEER_TOPK))
        rows = slice(h * N_KEYS, (h + 1) * N_KEYS)
        s1_ref[rows, :] = (sm[0] - vals[0][0]) - log_z
        s2_ref[rows, :] = sm[1] - vals[1][0]
        tau_ref[h:h + 1, :] = tau
        if exact:
            return None
        all_ok = functools.reduce(jnp.logical_and, oks)
        return jnp.min(all_ok.astype(f32)) > 0.5

    for h in range(PEER_HEADS):
        tie_free = route_head(h, exact=False)

        @pl.when(jnp.logical_not(tie_free))
        def _(h=h):
            route_head(h, exact=True)


def _peer_route(q, keys, *, tmr):
    M = q.shape[0]
    big = jax.ShapeDtypeStruct((PEER_HEADS * N_KEYS, M), f32)
    bspec = pl.BlockSpec((PEER_HEADS * N_KEYS, tmr), lambda i: (0, i))
    return pl.pallas_call(
        _peer_route_kernel, grid=(M // tmr,),
        in_specs=[pl.BlockSpec((tmr, 2 * PEER_HEADS * N_KEYS), lambda i: (i, 0)),
                  _const_spec(keys.shape)],
        out_specs=[bspec] * 2 + [pl.BlockSpec((PEER_HEADS, tmr), lambda i: (0, i))],
        out_shape=[big] * 2 + [jax.ShapeDtypeStruct((PEER_HEADS, M), f32)],
        compiler_params=_params(("parallel",)),
        name="peer_route",
    )(q, keys)


def _peer_prep_kernel(u_ref, v_ref, ub_ref, vt_ref):
    ub_ref[...] = u_ref[...].astype(bf16)
    vt_ref[...] = v_ref[...].T.astype(bf16)


def _peer_prep(u_tab, v_tab, *, te):
    L, E, D = u_tab.shape
    return pl.pallas_call(
        _peer_prep_kernel, grid=(L, E // te),
        in_specs=[pl.BlockSpec((None, te, D), lambda l, j: (l, j, 0))] * 2,
        out_specs=[pl.BlockSpec((None, te, D), lambda l, j: (l, j, 0)),
                   pl.BlockSpec((None, D, te), lambda l, j: (l, 0, j))],
        out_shape=[jax.ShapeDtypeStruct((L, E, D), bf16), jax.ShapeDtypeStruct((L, D, E), bf16)],
        compiler_params=_params(("parallel", "parallel")),
        name="peer_prep",
    )(u_tab, v_tab)


PEER_RB = 32
PEER_LC = 256
PEER_I1 = 4


def _peer_dense_kernel(xt_ref, u_ref, vt_ref, s1_ref, s2_ref, tau_ref, res_ref, out_ref, acc_ref, *, tm, te):
    j = pl.program_id(1)
    n_i1 = te // N_KEYS
    n_rb = N_KEYS // PEER_RB

    @pl.when(j == 0)
    def _():
        acc_ref[...] = jnp.zeros_like(acc_ref)

    chunks = [slice(c * PEER_LC, (c + 1) * PEER_LC) for c in range(tm // PEER_LC)]
    s1_rows = [[pl.ds(h * N_KEYS + j * n_i1 + ii, 1) for ii in range(n_i1)] for h in range(PEER_HEADS)]
    act = _dot(u_ref[...], xt_ref[...])
    cols = []
    for lanes in chunks:
        pieces = [[None] * n_rb for _ in range(n_i1)]
        for rb in range(n_rb):
            for i0 in range(0, n_i1, PEER_I1):
                group = range(i0, i0 + PEER_I1)
                gates = {ii: jnp.zeros((PEER_RB, PEER_LC), f32) for ii in group}
                for h in range(PEER_HEADS):
                    s2 = s2_ref[h * N_KEYS + rb * PEER_RB:h * N_KEYS + (rb + 1) * PEER_RB, lanes]
                    tau = tau_ref[h:h + 1, lanes]
                    for ii in group:
                        c = s2 + s1_ref[s1_rows[h][ii], lanes]
                        gates[ii] = gates[ii] + jnp.where(c >= tau, jnp.exp2(c), 0.0)
                for ii in group:
                    a = act[ii * N_KEYS + rb * PEER_RB:ii * N_KEYS + (rb + 1) * PEER_RB, lanes]
                    pieces[ii][rb] = (_gelu(a) * gates[ii]).astype(bf16)
        cols.append(jnp.concatenate([pc for row in pieces for pc in row], axis=0))
    acc_ref[...] += _dot(vt_ref[...], jnp.concatenate(cols, axis=1))

    @pl.when(j == pl.num_programs(1) - 1)
    def _():
        out_ref[...] = acc_ref[...].T + res_ref[...]


def _peer_dense(xt, ub, vt, layer, s1, s2, tau, res, *, tm, te):
    D, M = xt.shape
    E = ub.shape[1]
    once = dict(pipeline_mode=pl.Buffered(1))
    rspec = pl.BlockSpec((PEER_HEADS * N_KEYS, tm), lambda i, j: (0, i), **once)
    return pl.pallas_call(
        functools.partial(_peer_dense_kernel, tm=tm, te=te),
        grid=(M // tm, E // te),
        in_specs=[pl.BlockSpec((D, tm), lambda i, j: (0, i), **once),
                  pl.BlockSpec((None, te, D), lambda i, j: (layer, j, 0)),
                  pl.BlockSpec((None, D, te), lambda i, j: (layer, 0, j)),
                  rspec, rspec,
                  pl.BlockSpec((PEER_HEADS, tm), lambda i, j: (0, i), **once),
                  pl.BlockSpec((tm, D), lambda i, j: (i, 0), **once)],
        out_specs=pl.BlockSpec((tm, D), lambda i, j: (i, 0), **once),
        out_shape=jax.ShapeDtypeStruct((M, D), f32),
        scratch_shapes=[pltpu.VMEM((D, tm), f32)],
        compiler_params=_params(("parallel", "arbitrary")),
        name="peer_dense",
    )(xt, ub, vt, s1, s2, tau, res)


def _peer(x_all, norm_w, wq, keys, ub, vt, layer, *, tm_mm, tmr, tm, te):
    q, ht = _mm(x_all, wq, norm_w=norm_w, emit_h=bf16, transpose_h=True, tm=tm_mm, tn=1024, name="peer_query")
    s1, s2, tau = _peer_route(q, keys, tmr=tmr)
    return _peer_dense(ht, ub, vt, layer, s1, s2, tau, x_all, tm=tm, te=te)


def _block_diag_ones():
    i = jnp.arange(D_A) // HD_A
    return (i[:, None] == i[None, :]).astype(bf16)


LANE = 128


def _pad_lanes(w):
    return jnp.pad(w, ((0, 0), (0, (-w.shape[1]) % LANE)))


def _pad_rows(w):
    return jnp.pad(w, ((0, (-w.shape[0]) % LANE), (0, 0)))


def kernel(x_prompt, x_sample, state_shift, state_wkv, state_pool, cache_kv_w128, cache_kv_w512, cache_kv_w2048,
           norm_mix, norm_ffn, norm_final, a_w_in, a_w_out, a_mu_rkv, a_mu_wag, a_w0, a_w1, a_w2, a_a0, a_a1,
           a_a2, a_g1, a_g2, a_k_k, a_k_a, a_r_k, a_gn_w, a_gn_b, b_w_pool, b_scale, c_w_in, c_w_out, p_w_q,
           p_sub_keys, p_u, p_v):
    T = x_prompt.shape[1]
    NS = x_sample.shape[0]
    TM = 768
    M = -(-(T + NS) // TM) * TM
    S_BLK = T // NS
    assert T % NS == 0 and T % ATTN_BLK == 0
    cb = lambda w: w.astype(bf16)
    row = lambda w: w.reshape(1, -1)

    x0 = jnp.concatenate([x_prompt.reshape(T, D_MODEL), x_sample.reshape(NS, D_MODEL),
                          jnp.zeros((M - T - NS, D_MODEL), f32)], axis=0)
    ub, vt = _peer_prep(p_u, p_v, te=512)
    bd = _block_diag_ones()

    w_in = cb(a_w_in[0])
    z0, h0 = _mm(x0, w_in, norm_w=norm_mix[0], emit_h=f32, tm=TM, tn=1024, name="even_in_proj")
    zs_prev = _mm(state_shift[0], w_in[:, :3 * D_A], tm=NS, tn=1024, name="even_in_proj_state")
    plist = [a_mu_wag[0], row(a_mu_rkv[0]), row(a_w0[0]), _pad_lanes(cb(a_w1[0])), _pad_rows(cb(a_w2[0])),
             row(a_a0[0]), _pad_lanes(cb(a_a1[0])), _pad_rows(cb(a_a2[0])), cb(a_g1[0]), cb(a_g2[0]),
             row(a_k_k[0]), row(a_k_a[0]), cb(b_w_pool[0]), row(b_scale[0]), bd]
    rp, lwp, kp, vp, kkp, ap, gp, obp = _even_mid_prompt(h0, z0, plist, T=T, tm=256)
    rs, lws, ks, vs, kks, as_, gs, obs = _even_mid_sample(
        h0, state_shift[0], z0, zs_prev, jnp.swapaxes(state_pool[0], 0, 1), plist, row0=S_BLK, n=NS)
    o_p, wkv_p = _wkv_prompt(rp, lwp, kp, vp, kkp, ap, T=T, tb=512, npair=8)
    o_s, wkv_s = _wkv_sample(state_wkv[0], rs, lws, ks, vs, kks, as_)
    post_c = (cb(a_w_out[0]), row(a_r_k[0]), row(a_gn_w[0]), row(a_gn_b[0]), bd)
    x1 = _even_post(x0, o_p, rp, kp, vp, gp, obp, *post_c, row0=0, rows=T, tm=256)
    x1 = _even_post(x1, o_s, rs, ks, vs, gs, obs, *post_c, row0=S_BLK, rows=NS, tm=NS)
    x1 = _peer(x1, norm_ffn[0], cb(p_w_q[0]), cb(p_sub_keys[0].reshape(2 * PEER_HEADS, N_KEYS, N_KEYS)),
               ub, vt, 0, tm_mm=TM, tmr=256, tm=TM, te=1024)

    z1 = _mm(x1, cb(c_w_in[0]), norm_w=norm_mix[1], tm=TM, tn=1024, name="odd_in_proj")
    att_p = _attn_prompt(z1, T=T)
    z1s = z1[T:T + NS]
    att_s = _attn_sample(z1s, (cache_kv_w128[0], cache_kv_w512[0], cache_kv_w2048[0]))
    att = jnp.concatenate([att_p, att_s, jnp.zeros((M - T - NS, D_C), f32)], axis=0)
    x2 = _mm(att, cb(c_w_out[0]), res=x1, tm=TM, tn=1024, name="odd_out_proj")
    x2 = _peer(x2, norm_ffn[1], cb(p_w_q[1]), cb(p_sub_keys[1].reshape(2 * PEER_HEADS, N_KEYS, N_KEYS)),
               ub, vt, 1, tm_mm=TM, tmr=256, tm=TM, te=1024)

    y_p = _rmsnorm(x2, norm_final, tm=512, row0=0, rows=T)
    y_s = _rmsnorm(x2, norm_final, tm=NS, row0=S_BLK, rows=NS)

    u_p = z0[T - POOL_BUF:T, 3 * D_A:]
    u_s = z0[T:T + NS, 3 * D_A:]
    pool_s = jnp.concatenate([state_pool[0][:, 1:], u_s[:, None, :]], axis=1)
    kv_p, kv_s = [], []
    for g, (win, _) in enumerate(ATTN_GROUPS):
        n = min(win, T)
        kv = z1[T - n:T + NS, g * 3 * D_C + D_C:g * 3 * D_C + 3 * D_C].reshape(n + NS, 2, H_C, HD_C)
        kv_p.append(kv[:n][None, None])
        kv_s.append(kv[n:][None, :, None])
    return (y_p[None], y_s[:, None, :],
            h0[T - 1][None, None], h0[T:T + NS][None],
            wkv_p[None, None], wkv_s[None],
            u_p[None, None], pool_s[None],
            kv_p[0], kv_s[0], kv_p[1], kv_s[1], kv_p[2], kv_s[2])
```

```python
import functools
import math

import jax
import jax.numpy as jnp
from jax import lax
from jax.experimental import pallas as pl
from jax.experimental.pallas import tpu as pltpu

f32 = jnp.float32
bf16 = jnp.bfloat16

D_MODEL = 2048
H_A, HD_A = 16, 64
D_A = H_A * HD_A
D_POOL = D_MODEL - D_A
POOL_WINDOWS = (2, 4, 8, 16)
POOL_GD = D_POOL // len(POOL_WINDOWS)
POOL_BUF = max(POOL_WINDOWS) - 1
GN_EPS = 64e-5
NORM_EPS = 1e-6
ATTN_GROUPS = ((128, 1), (512, 4), (2048, 16))
H_C, HD_C = 8, 128
D_C = H_C * HD_C
N_KEYS = 128
N_EXPERTS = N_KEYS * N_KEYS
PEER_HEADS = 8
PEER_TOPK = 16

VMEM_LIMIT = 56 * 1024 * 1024
WKV_CHUNK = 64
ATTN_BLK = 2048
ATTN_UNROLL = 8
NEG_INF = float("-inf")
LOG2E = math.log2(math.e)


def _params(sem, vmem=VMEM_LIMIT, flags=None):
    return pltpu.CompilerParams(dimension_semantics=sem, vmem_limit_bytes=vmem, flags=flags)


def _dot(a, b):
    return jnp.dot(a, b, preferred_element_type=f32)


def _dot_nt(a, b):
    return lax.dot_general(a, b, (((1,), (1,)), ((), ())), preferred_element_type=f32)


def _dot_tn(a, b):
    return lax.dot_general(a, b, (((0,), (0,)), ((), ())), preferred_element_type=f32)


def _split(x):
    hi = x.astype(bf16)
    lo = (x - hi.astype(f32)).astype(bf16)
    return hi, lo


def _dot3(a, b, dot=_dot):
    ah, al = _split(a)
    bh, bl = _split(b)
    return dot(ah, bh) + (dot(ah, bl) + dot(al, bh))


def _head_sum(x, bd):
    hi, lo = _split(x)
    return _dot(hi, bd) + _dot(lo, bd)


def _gelu(x):
    return 0.5 * x * (1.0 + lax.erf(x * 0.7071067811865476))


def _mm_kernel(*refs, has_norm, has_res, emit_h, transpose_h):
    it = iter(refs)
    x_ref, w_ref = next(it), next(it)
    g_ref = next(it) if has_norm else None
    res_ref = next(it) if has_res else None
    o_ref = next(it)
    h_ref = next(it) if emit_h else None
    xb_ref = next(it)

    @pl.when(pl.program_id(1) == 0)
    def _():
        x = x_ref[...]
        if has_norm:
            ms = jnp.mean(x * x, axis=-1, keepdims=True)
            x = (x * lax.rsqrt(ms + NORM_EPS)) * g_ref[...]
            if emit_h:
                h_ref[...] = (x.T if transpose_h else x).astype(h_ref.dtype)
        xb_ref[...] = x.astype(bf16)

    acc = _dot(xb_ref[...], w_ref[...])
    if has_res:
        acc = acc + res_ref[...]
    o_ref[...] = acc


def _mm(x, w, *, norm_w=None, res=None, emit_h=None, transpose_h=False, tm, tn, row0=0, rows=None, name="proj"):
    K = x.shape[1]
    N = w.shape[1]
    rows = x.shape[0] if rows is None else rows
    assert rows % tm == 0 and N % tn == 0
    w_mode = dict(pipeline_mode=pl.Buffered(1)) if tn == N else {}
    in_specs = [pl.BlockSpec((tm, K), lambda i, j: (i + row0, 0)),
                pl.BlockSpec((K, tn), lambda i, j: (0, j), **w_mode)]
    args = [x, w]
    if norm_w is not None:
        in_specs.append(pl.BlockSpec((1, K), lambda i, j: (0, 0)))
        args.append(norm_w.reshape(1, K))
    if res is not None:
        in_specs.append(pl.BlockSpec((tm, tn), lambda i, j: (i + row0, j)))
        args.append(res)
    out_shape = [jax.ShapeDtypeStruct((rows, N), f32)]
    out_specs = [pl.BlockSpec((tm, tn), lambda i, j: (i, j))]
    if emit_h is not None and transpose_h:
        out_shape.append(jax.ShapeDtypeStruct((K, rows), emit_h))
        out_specs.append(pl.BlockSpec((K, tm), lambda i, j: (0, i)))
    elif emit_h is not None:
        out_shape.append(jax.ShapeDtypeStruct((rows, K), emit_h))
        out_specs.append(pl.BlockSpec((tm, K), lambda i, j: (i, 0)))
    outs = pl.pallas_call(
        functools.partial(_mm_kernel, has_norm=norm_w is not None, has_res=res is not None,
                          emit_h=emit_h is not None, transpose_h=transpose_h),
        grid=(rows // tm, N // tn),
        in_specs=in_specs, out_specs=out_specs, out_shape=out_shape,
        scratch_shapes=[pltpu.VMEM((tm, K), bf16)],
        compiler_params=_params(("parallel", "arbitrary")),
        name=name,
    )(*args)
    return outs if emit_h is not None else outs[0]


def _rms_kernel(x_ref, g_ref, o_ref):
    x = x_ref[...]
    ms = jnp.mean(x * x, axis=-1, keepdims=True)
    o_ref[...] = (x * lax.rsqrt(ms + NORM_EPS)) * g_ref[...]


def _rmsnorm(x, g, *, tm, row0, rows):
    K = x.shape[1]
    return pl.pallas_call(
        _rms_kernel, grid=(rows // tm,),
        in_specs=[pl.BlockSpec((tm, K), lambda i: (i + row0, 0)), pl.BlockSpec((1, K), lambda i: (0, 0))],
        out_specs=pl.BlockSpec((tm, K), lambda i: (i, 0)),
        out_shape=jax.ShapeDtypeStruct((rows, K), f32),
        compiler_params=_params(("parallel",)),
        name="rmsnorm",
    )(x, g.reshape(1, K))


def _even_token_math(h, hs, zc, zs, P):
    dh = hs - h
    mu = P["mu_wag"]
    xw = (h + dh * mu[0:1]).astype(bf16)
    xa = (h + dh * mu[1:2]).astype(bf16)
    xg = (h + dh * mu[2:3]).astype(bf16)
    tw = jnp.tanh(_dot(xw, P["w1"])).astype(bf16)
    wl = P["w0"] + _dot(tw, P["w2"])
    w_log = -jax.nn.softplus(-wl) - 0.5
    lw = -jnp.exp(w_log)
    a = jax.nn.sigmoid(P["a0"] + _dot(_dot(xa, P["a1"]).astype(bf16), P["a2"]))
    g = _dot(jax.nn.sigmoid(_dot(xg, P["g1"])).astype(bf16), P["g2"])
    rkv = zc + (zs - zc) * P["mu_rkv"]
    r, k, v = rkv[:, :D_A], rkv[:, D_A:2 * D_A], rkv[:, 2 * D_A:]
    kk = k * P["k_k"]
    nrm = jnp.sqrt(_head_sum(kk * kk, P["bd"]))
    kkn = kk / jnp.maximum(nrm, 1e-12)
    k2 = k * (1.0 + (a - 1.0) * P["k_a"])
    return r, lw, k2, v, kkn, a, g


def _pool_project(pm, P):
    outs = []
    for gi in range(len(POOL_WINDOWS)):
        c = slice(gi * POOL_GD, (gi + 1) * POOL_GD)
        outs.append(_dot(pm[:, c].astype(bf16), P["w_pool"][gi]))
    return jnp.concatenate(outs, axis=-1) * P["pool_scale"]


_EVEN_PARAM_NAMES = ("mu_wag", "mu_rkv", "w0", "w1", "w2", "a0", "a1", "a2", "g1", "g2", "k_k", "k_a",
                     "w_pool", "pool_scale", "bd")


def _load_params(refs):
    return {n: r[...] for n, r in zip(_EVEN_PARAM_NAMES, refs)}


def _even_mid_prompt_kernel(h_ref, hp_ref, z_ref, zp_ref, *rest, tm):
    prefs, outs = rest[:len(_EVEN_PARAM_NAMES)], rest[len(_EVEN_PARAM_NAMES):]
    P = _load_params(prefs)
    first = pl.program_id(0) == 0
    h = h_ref[...]
    z = z_ref[...]
    zc, u = z[:, :3 * D_A], z[:, 3 * D_A:]
    hprev = jnp.where(first, 0.0, hp_ref[15:16, :])
    zprev = jnp.where(first, 0.0, zp_ref[15:16, :3 * D_A])
    row = lax.broadcasted_iota(jnp.int32, (tm, 1), 0)
    hs = jnp.where(row == 0, hprev, pltpu.roll(h, 1, axis=0))
    zs = jnp.where(row == 0, zprev, pltpu.roll(zc, 1, axis=0))
    prow = lax.broadcasted_iota(jnp.int32, (tm, POOL_GD), 0)
    r, lw, k2, v, kkn, a, g = _even_token_math(h, hs, zc, zs, P)

    uprev = jnp.where(first, 0.0, zp_ref[:, 3 * D_A:])
    pos = pl.program_id(0) * tm + prow
    means = []
    for gi, win in enumerate(POOL_WINDOWS):
        c = slice(gi * POOL_GD, (gi + 1) * POOL_GD)
        s = jnp.concatenate([uprev[:, c], u[:, c]], axis=0)
        sh = 1
        while sh < win:
            s = s + pltpu.roll(s, sh, axis=0)
            sh *= 2
        cnt = jnp.minimum(pos + 1, win).astype(f32)
        means.append(s[16:, :] / cnt)
    ob = _pool_project(jnp.concatenate(means, axis=-1) - u, P)
    for o_ref, val in zip(outs, (r, lw, k2, v, kkn, a, g, ob)):
        o_ref[...] = val


def _even_mid_sample_kernel(h_ref, hs_ref, z_ref, zs_ref, buf_ref, *rest):
    prefs, outs = rest[:len(_EVEN_PARAM_NAMES)], rest[len(_EVEN_PARAM_NAMES):]
    P = _load_params(prefs)
    h = h_ref[...]
    z = z_ref[...]
    zc, u = z[:, :3 * D_A], z[:, 3 * D_A:]
    r, lw, k2, v, kkn, a, g = _even_token_math(h, hs_ref[...], zc, zs_ref[...], P)
    means = []
    for gi, win in enumerate(POOL_WINDOWS):
        c = slice(gi * POOL_GD, (gi + 1) * POOL_GD)
        s = u[:, c]
        for j in range(POOL_BUF - (win - 1), POOL_BUF):
            s = s + buf_ref[j, :, c]
        means.append(s / float(win))
    ob = _pool_project(jnp.concatenate(means, axis=-1) - u, P)
    for o_ref, val in zip(outs, (r, lw, k2, v, kkn, a, g, ob)):
        o_ref[...] = val


def _const_spec(shape):
    nd = len(shape)
    return pl.BlockSpec(shape, lambda i, _nd=nd: (0,) * _nd, pipeline_mode=pl.Buffered(1))


def _even_mid_prompt(h_all, z_all, plist, *, T, tm):
    nb = tm // 16
    in_specs = [pl.BlockSpec((tm, D_MODEL), lambda i: (i, 0)),
                pl.BlockSpec((16, D_MODEL), lambda i: (jnp.maximum(i * nb - 1, 0), 0)),
                pl.BlockSpec((tm, 4 * D_A), lambda i: (i, 0)),
                pl.BlockSpec((16, 4 * D_A), lambda i: (jnp.maximum(i * nb - 1, 0), 0))]
    in_specs += [_const_spec(p.shape) for p in plist]
    return pl.pallas_call(
        functools.partial(_even_mid_prompt_kernel, tm=tm),
        grid=(T // tm,), in_specs=in_specs,
        out_specs=[pl.BlockSpec((tm, D_A), lambda i: (i, 0))] * 8,
        out_shape=[jax.ShapeDtypeStruct((T, D_A), f32)] * 8,
        compiler_params=_params(("parallel",)),
        name="even_mid_prompt",
    )(h_all, h_all, z_all, z_all, *plist)


def _even_mid_sample(h_all, hs, z_all, zs, buf_t, plist, *, row0, n):
    in_specs = [pl.BlockSpec((n, D_MODEL), lambda i: (row0, 0)),
                pl.BlockSpec((n, D_MODEL), lambda i: (0, 0)),
                pl.BlockSpec((n, 4 * D_A), lambda i: (row0, 0)),
                pl.BlockSpec((n, 3 * D_A), lambda i: (0, 0)),
                pl.BlockSpec((POOL_BUF, n, D_POOL), lambda i: (0, 0, 0))]
    in_specs += [_const_spec(p.shape) for p in plist]
    return pl.pallas_call(
        _even_mid_sample_kernel, grid=(1,), in_specs=in_specs,
        out_specs=[pl.BlockSpec((n, D_A), lambda i: (0, 0))] * 8,
        out_shape=[jax.ShapeDtypeStruct((n, D_A), f32)] * 8,
        compiler_params=_params(("arbitrary",)),
        name="even_mid_sample",
    )(h_all, hs, z_all, zs, buf_t, *plist)


def _wkv_prompt_kernel(r_ref, lw_ref, k_ref, v_ref, kk_ref, a_ref, o_ref, s_out_ref, S_ref, *, tb, npair):
    C = WKV_CHUNK
    t = pl.program_id(1)

    @pl.when(t == 0)
    def _():
        S_ref[...] = jnp.zeros_like(S_ref)

    ri = lax.broadcasted_iota(jnp.int32, (C, C), 0)
    ci = lax.broadcasted_iota(jnp.int32, (C, C), 1)
    strict = ri > ci
    incl = ri >= ci
    eye = (ri == ci).astype(f32)
    rows = lax.broadcasted_iota(jnp.int32, (C, 2 * HD_A), 0)

    nh = 2 * npair
    heads = range(nh)

    def prep(sl, lanes):
        r, lw, k, v, kk, a = (x[sl, lanes] for x in (r_ref, lw_ref, k_ref, v_ref, kk_ref, a_ref))
        cum = lw
        sh = 1
        while sh < C:
            cum = cum + jnp.where(rows >= sh, pltpu.roll(cum, sh, axis=0), 0.0)
            sh *= 2
        cum_c = cum[C - 1:C, :]
        b = kk * a
        e_neg = jnp.exp(-cum)
        e_rem = jnp.exp(cum_c - cum)
        return dict(at=-kk * jnp.exp(cum - lw), rt=r * jnp.exp(cum), bt=b * e_neg, kt=k * e_neg,
                    bh=b * e_rem, kh=k * e_rem, w_c=jnp.exp(cum_c), v=v)

    def chunk(c, carry):
        off = pl.multiple_of(c * C, C)
        sl = pl.ds(off, C)
        pairs = [prep(sl, slice(p * 2 * HD_A, (p + 1) * 2 * HD_A)) for p in range(npair)]
        S0 = [S_ref[h] for h in heads]

        def head(name, h):
            return pairs[h // 2][name][:, (h % 2) * HD_A:(h % 2 + 1) * HD_A]

        ar = [jnp.concatenate([head("at", h), head("rt", h)], axis=0) for h in heads]
        bk = [jnp.concatenate([head("bt", h), head("kt", h)], axis=0) for h in heads]
        vh = [head("v", h) for h in heads]
        m4 = [_dot3(ar[h], bk[h], _dot_nt) for h in heads]
        sar = [_dot3(ar[h], S0[h], _dot_nt) for h in heads]
        a_ab = [jnp.where(strict, m4[h][:C, :C], 0.0) for h in heads]
        a_ak = [jnp.where(strict, m4[h][:C, C:], 0.0) for h in heads]
        a_rb = [jnp.where(incl, m4[h][C:, :C], 0.0) for h in heads]
        a_rk = [jnp.where(incl, m4[h][C:, C:], 0.0) for h in heads]
        av = [_dot3(jnp.concatenate([a_ak[h], a_rk[h]], axis=0), vh[h]) for h in heads]
        tinv = [eye + a_ab[h] for h in heads]
        pw = [_dot3(a_ab[h], a_ab[h]) for h in heads]
        n_sq = 2
        while n_sq * 2 < C:
            x = [_dot3(pw[h], jnp.concatenate([tinv[h], pw[h]], axis=1)) for h in heads]
            tinv = [tinv[h] + x[h][:, :C] for h in heads]
            pw = [x[h][:, C:] for h in heads]
            n_sq *= 2
        tinv = [tinv[h] + _dot3(pw[h], tinv[h]) for h in heads]
        u = [_dot3(tinv[h], sar[h][:C] + av[h][:C]) for h in heads]
        o = [sar[h][C:] + av[h][C:] + _dot3(a_rb[h], u[h]) for h in heads]
        s_new = []
        for h in heads:
            uv = jnp.concatenate([u[h], vh[h]], axis=0)
            bkh = jnp.concatenate([head("bh", h), head("kh", h)], axis=0)
            s_new.append(S0[h] * head("w_c", h) + _dot3(uv, bkh, _dot_tn))
        for h in heads:
            S_ref[h] = s_new[h]
        for p in range(npair):
            o_ref[sl, p * 2 * HD_A:(p + 1) * 2 * HD_A] = jnp.concatenate([o[2 * p], o[2 * p + 1]], axis=1)
        return carry

    lax.fori_loop(0, tb // C, chunk, 0)

    @pl.when(t == pl.num_programs(1) - 1)
    def _():
        s_out_ref[...] = S_ref[...]


def _wkv_prompt(r, lw, k2, v, kkn, a, *, T, tb, npair):
    spec = pl.BlockSpec((tb, 2 * HD_A * npair), lambda j, t: (t, j))
    nh = 2 * npair
    return pl.pallas_call(
        functools.partial(_wkv_prompt_kernel, tb=tb, npair=npair),
        grid=(H_A // nh, T // tb),
        in_specs=[spec] * 6,
        out_specs=[spec, pl.BlockSpec((nh, HD_A, HD_A), lambda j, t: (j, 0, 0))],
        out_shape=[jax.ShapeDtypeStruct((T, D_A), f32), jax.ShapeDtypeStruct((H_A, HD_A, HD_A), f32)],
        scratch_shapes=[pltpu.VMEM((nh, HD_A, HD_A), f32)],
        compiler_params=_params(("parallel", "arbitrary")),
        name="wkv_prompt",
    )(r, lw, k2, v, kkn, a)


def _wkv_sample_kernel(s_ref, r_ref, lw_ref, k_ref, kk_ref, a_ref, v_ref, o_ref, s_out_ref):
    S = s_ref[0]
    kk = kk_ref[0]
    sa = jnp.sum(S * (-kk), axis=-1, keepdims=True)
    S2 = S * jnp.exp(lw_ref[0]) + sa * (kk * a_ref[0]) + v_ref[0] * k_ref[0]
    s_out_ref[0] = S2
    o_ref[0] = jnp.sum(S2 * r_ref[0], axis=-1, keepdims=True)


def _wkv_sample(S0, r, lw, k2, v, kkn, a):
    n = S0.shape[0]
    row = lambda x: x.reshape(n, H_A, 1, HD_A)
    rspec = pl.BlockSpec((1, H_A, 1, HD_A), lambda b: (b, 0, 0, 0))
    cspec = pl.BlockSpec((1, H_A, HD_A, 1), lambda b: (b, 0, 0, 0))
    sspec = pl.BlockSpec((1, H_A, HD_A, HD_A), lambda b: (b, 0, 0, 0))
    o, S = pl.pallas_call(
        _wkv_sample_kernel, grid=(n,),
        in_specs=[sspec, rspec, rspec, rspec, rspec, rspec, cspec],
        out_specs=[cspec, sspec],
        out_shape=[jax.ShapeDtypeStruct((n, H_A, HD_A, 1), f32), jax.ShapeDtypeStruct(S0.shape, f32)],
        compiler_params=_params(("parallel",)),
        name="wkv_sample",
    )(S0, row(r), row(lw), row(k2), row(kkn), row(a), v.reshape(n, H_A, HD_A, 1))
    return o.reshape(n, D_A), S


def _even_post_kernel(x_ref, o_ref, r_ref, k_ref, v_ref, g_ref, ob_ref, wo_ref, rk_ref, gnw_ref, gnb_ref, bd_ref,
                      out_ref):
    bd = bd_ref[...]
    o = o_ref[...]
    inv = 1.0 / HD_A
    mu = _head_sum(o, bd) * inv
    d = o - mu
    var = _head_sum(d * d, bd) * inv
    on = d * lax.rsqrt(var + GN_EPS) * gnw_ref[...] + gnb_ref[...]
    v = v_ref[...]
    bonus = _head_sum(r_ref[...] * k_ref[...] * rk_ref[...], bd) * v
    oa = ((on + bonus) * g_ref[...]).astype(bf16)
    y = _dot(oa, wo_ref[:D_A, :]) + _dot(ob_ref[...].astype(bf16), wo_ref[D_A:, :])
    out_ref[...] = x_ref[...] + y


def _even_post(x_all, o, r, k2, v, g, ob, wo, rk, gnw, gnb, bd, *, row0, rows, tm):
    a_spec = pl.BlockSpec((tm, D_A), lambda i: (i, 0))
    x_spec = pl.BlockSpec((tm, D_MODEL), lambda i: (i + row0, 0))
    consts = [wo, rk, gnw, gnb, bd]
    return pl.pallas_call(
        _even_post_kernel, grid=(rows // tm,),
        in_specs=[x_spec] + [a_spec] * 6 + [_const_spec(c.shape) for c in consts],
        out_specs=x_spec,
        out_shape=jax.ShapeDtypeStruct(x_all.shape, f32),
        input_output_aliases={0: 0},
        compiler_params=_params(("parallel",)),
        name="even_post",
    )(x_all, o, r, k2, v, g, ob, *consts)


def _attn_prompt_kernel(*refs):
    in_refs, out_ref, og_ref, lg_ref = refs[:15], refs[15], refs[16], refs[17]
    n = pl.program_id(1)
    has_prev = n > 0
    scale = HD_C ** -0.5
    Q = 128
    ri = lax.broadcasted_iota(jnp.int32, (Q, Q), 0)
    ci = lax.broadcasted_iota(jnp.int32, (Q, Q), 1)
    mask_prev0 = ci >= ri
    mask_cur = ci <= ri

    for g, (_, dil) in enumerate(ATTN_GROUPS):
        q_ref, kp_ref, k_ref, vp_ref, v_ref = in_refs[5 * g:5 * g + 5]
        nblk = ATTN_BLK // (Q * dil)

        def units(it, carry, q_ref=q_ref, kp_ref=kp_ref, k_ref=k_ref, vp_ref=vp_ref, v_ref=v_ref,
                  dil=dil, nblk=nblk, g=g):
            us = [it * ATTN_UNROLL + k for k in range(ATTN_UNROLL)]
            rows, q, kc, vc, kp, vp, valid_prev = [], [], [], [], [], [], []
            for u in us:
                c = u // nblk
                m = u % nblk
                r = pl.ds(m * (Q * dil) + c, Q, stride=dil)
                in_blk = m > 0
                r_a = pl.ds(jnp.maximum(m - 1, 0) * (Q * dil) + c, Q, stride=dil)
                r_b = pl.ds((nblk - 1) * (Q * dil) + c, Q, stride=dil)
                rows.append(r)
                q.append(q_ref[r, :].astype(bf16))
                kc.append(k_ref[r, :].astype(bf16))
                vc.append(v_ref[r, :].astype(bf16))
                kp.append(jnp.where(in_blk, k_ref[r_a, :], kp_ref[r_b, :]).astype(bf16))
                vp.append(jnp.where(in_blk, v_ref[r_a, :], vp_ref[r_b, :]).astype(bf16))
                valid_prev.append(jnp.logical_or(in_blk, has_prev))
            n = range(ATTN_UNROLL)
            s_p = [_dot_nt(q[i], kp[i]) * scale for i in n]
            s_c = [_dot_nt(q[i], kc[i]) * scale for i in n]
            s_p = [jnp.where(jnp.logical_and(mask_prev0, valid_prev[i]), s_p[i], NEG_INF) for i in n]
            s_c = [jnp.where(mask_cur, s_c[i], NEG_INF) for i in n]
            mx = [jnp.maximum(jnp.max(s_p[i], axis=1, keepdims=True), jnp.max(s_c[i], axis=1, keepdims=True))
                  for i in n]
            p_p = [jnp.exp(s_p[i] - mx[i]) for i in n]
            p_c = [jnp.exp(s_c[i] - mx[i]) for i in n]
            den = [jnp.sum(p_p[i], axis=1, keepdims=True) + jnp.sum(p_c[i], axis=1, keepdims=True) for i in n]
            o = [(_dot(p_p[i].astype(bf16), vp[i]) + _dot(p_c[i].astype(bf16), vc[i])) / den[i] for i in n]
            lse = [mx[i] + jnp.log(den[i]) for i in n]
            for i in n:
                og_ref[g, rows[i], :] = o[i]
                lg_ref[g, rows[i], :] = jnp.broadcast_to(lse[i], (Q, HD_C))
            return carry

        lax.fori_loop(0, dil * nblk // ATTN_UNROLL, units, 0)

    l0, l1, l2 = lg_ref[0], lg_ref[1], lg_ref[2]
    mx = jnp.maximum(jnp.maximum(l0, l1), l2)
    e0, e1, e2 = jnp.exp(l0 - mx), jnp.exp(l1 - mx), jnp.exp(l2 - mx)
    out_ref[...] = (e0 * og_ref[0] + e1 * og_ref[1] + e2 * og_ref[2]) / (e0 + e1 + e2)


def _attn_prompt(z, *, T):
    nb = T // ATTN_BLK
    in_specs = []
    for g in range(len(ATTN_GROUPS)):
        def col(j, g=g):
            return lambda h, n: (n, g * 3 * H_C + j * H_C + h)

        def col_prev(j, g=g):
            return lambda h, n: (jnp.maximum(n - 1, 0), g * 3 * H_C + j * H_C + h)
        blk = (ATTN_BLK, HD_C)
        in_specs += [pl.BlockSpec(blk, col(0)), pl.BlockSpec(blk, col_prev(1)), pl.BlockSpec(blk, col(1)),
                     pl.BlockSpec(blk, col_prev(2)), pl.BlockSpec(blk, col(2))]
    return pl.pallas_call(
        _attn_prompt_kernel, grid=(H_C, nb),
        in_specs=in_specs,
        out_specs=pl.BlockSpec((ATTN_BLK, HD_C), lambda h, n: (n, h)),
        out_shape=jax.ShapeDtypeStruct((T, D_C), f32),
        scratch_shapes=[pltpu.VMEM((3, ATTN_BLK, HD_C), f32), pltpu.VMEM((3, ATTN_BLK, HD_C), f32)],
        compiler_params=_params(("parallel", "arbitrary")),
        name="attn_prompt",
    )(*([z] * 15))


def _attn_sample_kernel(z_ref, c0_ref, c1_ref, c2_ref, out_ref):
    scale = HD_C ** -0.5
    crefs = (c0_ref, c1_ref, c2_ref)
    for h in range(H_C):
        outs, lses = [], []
        for g in range(len(ATTN_GROUPS)):
            base = g * 3 * D_C + h * HD_C
            q = z_ref[0, :, base:base + HD_C]
            kn = z_ref[0, :, base + D_C:base + D_C + HD_C]
            vn = z_ref[0, :, base + 2 * D_C:base + 2 * D_C + HD_C]
            K = crefs[g][0, :, 0, 0, h, :]
            V = crefs[g][0, :, 0, 1, h, :]
            s_c = jnp.sum(K * q, axis=1, keepdims=True) * scale
            s_n = jnp.sum(kn * q, axis=1, keepdims=True) * scale
            mx = jnp.maximum(jnp.max(s_c, axis=0, keepdims=True), s_n)
            p_c = jnp.exp(s_c - mx)
            p_n = jnp.exp(s_n - mx)
            den = jnp.sum(p_c, axis=0, keepdims=True) + p_n
            outs.append((jnp.sum(p_c * V, axis=0, keepdims=True) + p_n * vn) / den)
            lses.append(mx + jnp.log(den))
        mx = jnp.maximum(jnp.maximum(lses[0], lses[1]), lses[2])
        es = [jnp.exp(l - mx) for l in lses]
        out_ref[0, :, h * HD_C:(h + 1) * HD_C] = (es[0] * outs[0] + es[1] * outs[1] + es[2] * outs[2]) / (
            es[0] + es[1] + es[2])


def _attn_sample(z_s, caches):
    n = z_s.shape[0]
    in_specs = [pl.BlockSpec((1, 1, 9 * D_C), lambda b: (b, 0, 0))]
    args = [z_s.reshape(n, 1, 9 * D_C)]
    for (win, dil), c in zip(ATTN_GROUPS, caches):
        assert c.shape[1] == win
        args.append(c.reshape(n, win // dil, dil, 2, H_C, HD_C))
        in_specs.append(pl.BlockSpec((1, win // dil, 1, 2, H_C, HD_C), lambda b: (b, 0, 0, 0, 0, 0)))
    out = pl.pallas_call(
        _attn_sample_kernel, grid=(n,), in_specs=in_specs,
        out_specs=pl.BlockSpec((1, 1, D_C), lambda b: (b, 0, 0)),
        out_shape=jax.ShapeDtypeStruct((n, 1, D_C), f32),
        compiler_params=_params(("parallel",)),
        name="attn_sample",
    )(*args)
    return out.reshape(n, D_C)


def _top16_rows(s):
    rows = lax.broadcasted_iota(jnp.int32, s.shape, 0)
    work = s
    vals = []
    for _ in range(PEER_TOPK):
        m = jnp.max(work, axis=0, keepdims=True)
        idx = jnp.min(jnp.where(work == m, rows, s.shape[0]), axis=0, keepdims=True)
        work = jnp.where(rows == idx, NEG_INF, work)
        vals.append(m)
    return vals, jnp.logical_and(work == NEG_INF, s != NEG_INF)


def _oddeven_mergesort_pairs(n):
    pairs = []

    def merge(lo, hi, r):
        step = r * 2
        if step < hi - lo:
            merge(lo, hi, step)
            merge(lo + r, hi, step)
            pairs.extend((i, i + r) for i in range(lo + r, hi - r, step))
        else:
            pairs.append((lo, lo + r))

    def sort(lo, hi):
        if hi - lo >= 1:
            mid = lo + (hi - lo) // 2
            sort(lo, mid)
            sort(mid + 1, hi)
            merge(lo, hi, 1)

    sort(0, n - 1)
    return pairs


SUBLANES = 8


def _top16_sorted(s):
    assert s.shape[0] == PEER_TOPK * SUBLANES
    t = [s[SUBLANES * k:SUBLANES * (k + 1), :] for k in range(PEER_TOPK)]
    for i, j in _oddeven_mergesort_pairs(PEER_TOPK):
        t[i], t[j] = jnp.maximum(t[i], t[j]), jnp.minimum(t[i], t[j])
    shift = 1
    while shift < SUBLANES:
        other = [pltpu.roll(x, shift, axis=0) for x in t]
        t = [jnp.maximum(t[i], other[PEER_TOPK - 1 - i]) for i in range(PEER_TOPK)]
        d = PEER_TOPK // 2
        while d >= 1:
            for i in range(PEER_TOPK):
                if i & d == 0:
                    t[i], t[i + d] = jnp.maximum(t[i], t[i + d]), jnp.minimum(t[i], t[i + d])
            d //= 2
        shift *= 2
    thr = t[PEER_TOPK - 1][0:1, :]
    member = s >= thr
    count = jnp.sum(member.astype(f32), axis=0, keepdims=True)
    return [x[0:1, :] for x in t], member, count == float(PEER_TOPK)


def _top16_rows_distinct(ss):
    work = list(ss)
    vals = [[] for _ in ss]
    for _ in range(PEER_TOPK):
        for i in range(len(ss)):
            m = jnp.max(work[i], axis=0, keepdims=True)
            work[i] = jnp.where(work[i] == m, NEG_INF, work[i])
            vals[i].append(m)
    out = []
    for s, w, v in zip(ss, work, vals):
        member = jnp.logical_and(w == NEG_INF, s != NEG_INF)
        count = jnp.sum(member.astype(f32), axis=0, keepdims=True)
        out.append((v, member, count == float(PEER_TOPK)))
    return out


def _peer_route_kernel(q_ref, keys_ref, s1_ref, s2_ref, tau_ref):
    n = q_ref.shape[0]

    def route_head(h, exact):
        if exact:
            top16 = lambda ss: [_top16_rows(s) + (None,) for s in ss]
            top16_scores = top16
        else:
            top16 = _top16_rows_distinct
            top16_scores = lambda ss: [_top16_sorted(s) for s in ss]
        scores = []
        for p in range(2):
            hp = 2 * h + p
            qs = q_ref[:, hp * N_KEYS:(hp + 1) * N_KEYS].astype(bf16)
            scores.append(_dot_nt(keys_ref[hp], qs) * LOG2E)
        sm, vals, oks = [], [], []
        for s, (v, member, ok) in zip(scores, top16_scores(scores)):
            sm.append(jnp.where(member, s, NEG_INF))
            vals.append(v)
            oks.append(ok)

        def pair_sums(z1, z2):
            pieces = []
            for r1 in range(PEER_TOPK):
                cnt = PEER_TOPK // (r1 + 1)
                pieces.append(z1[r1] + jnp.concatenate(z2[:cnt], axis=0))
            npad = (-sum(p.shape[0] for p in pieces)) % 8
            pieces.append(jnp.full((npad, n), NEG_INF, f32))
            return jnp.concatenate(pieces, axis=0)

        z1 = [v - vals[0][0] for v in vals[0]]
        z2 = [v - vals[1][0] for v in vals[1]]
        (top, chosen, ok), = top16([pair_sums(z1, z2)])
        oks.append(ok)
        log_z = jnp.log2(sum(jnp.exp2(t) for t in top))
        z1 = [z - log_z for z in z1]
        sums = pair_sums(z1, z2)
        if exact:
            (top, _, _), = top16([sums])
            tau = top[-1]
        else:
            tau = jnp.min(jnp.where(chosen, sums, jnp.inf), axis=0, keepdims=True)
            count = jnp.sum((sums >= tau).astype(f32), axis=0, keepdims=True)
            oks.append(count == float(PEER_TOPK))
        rows = slice(h * N_KEYS, (h + 1) * N_KEYS)
        s1_ref[rows, :] = (sm[0] - vals[0][0]) - log_z
        s2_ref[rows, :] = sm[1] - vals[1][0]
        tau_ref[h:h + 1, :] = tau
        if exact:
            return None
        all_ok = functools.reduce(jnp.logical_and, oks)
        return jnp.min(all_ok.astype(f32)) > 0.5

    tie_free = [route_head(h, exact=False) for h in range(PEER_HEADS)]
    for h in range(PEER_HEADS):
        pl.when(jnp.logical_not(tie_free[h]))(functools.partial(route_head, h, True))


def _peer_route(q, keys, *, tmr):
    M = q.shape[0]
    big = jax.ShapeDtypeStruct((PEER_HEADS * N_KEYS, M), f32)
    bspec = pl.BlockSpec((PEER_HEADS * N_KEYS, tmr), lambda i: (0, i))
    return pl.pallas_call(
        _peer_route_kernel, grid=(M // tmr,),
        in_specs=[pl.BlockSpec((tmr, 2 * PEER_HEADS * N_KEYS), lambda i: (i, 0)),
                  _const_spec(keys.shape)],
        out_specs=[bspec] * 2 + [pl.BlockSpec((PEER_HEADS, tmr), lambda i: (0, i))],
        out_shape=[big] * 2 + [jax.ShapeDtypeStruct((PEER_HEADS, M), f32)],
        compiler_params=_params(("parallel",)),
        name="peer_route",
    )(q, keys)


def _peer_prep_kernel(u_ref, v_ref, ub_ref, vt_ref):
    ub_ref[...] = u_ref[...].astype(bf16)
    vt_ref[...] = v_ref[...].T.astype(bf16)


def _peer_prep(u_tab, v_tab, *, te):
    L, E, D = u_tab.shape
    return pl.pallas_call(
        _peer_prep_kernel, grid=(L, E // te),
        in_specs=[pl.BlockSpec((None, te, D), lambda l, j: (l, j, 0))] * 2,
        out_specs=[pl.BlockSpec((None, te, D), lambda l, j: (l, j, 0)),
                   pl.BlockSpec((None, D, te), lambda l, j: (l, 0, j))],
        out_shape=[jax.ShapeDtypeStruct((L, E, D), bf16), jax.ShapeDtypeStruct((L, D, E), bf16)],
        compiler_params=_params(("parallel", "parallel")),
        name="peer_prep",
    )(u_tab, v_tab)


PEER_RB = 32
PEER_LC = 256
PEER_I1 = 4


def _peer_dense_kernel(xt_ref, u_ref, vt_ref, s1_ref, s2_ref, tau_ref, res_ref, out_ref, acc_ref, *, tm, te):
    j = pl.program_id(1)
    n_i1 = te // N_KEYS
    n_rb = N_KEYS // PEER_RB

    @pl.when(j == 0)
    def _():
        acc_ref[...] = jnp.zeros_like(acc_ref)

    chunks = [slice(c * PEER_LC, (c + 1) * PEER_LC) for c in range(tm // PEER_LC)]
    s1_rows = [[pl.ds(h * N_KEYS + j * n_i1 + ii, 1) for ii in range(n_i1)] for h in range(PEER_HEADS)]
    act = _dot(u_ref[...], xt_ref[...])
    cols = []
    for lanes in chunks:
        pieces = [[None] * n_rb for _ in range(n_i1)]
        for rb in range(n_rb):
            for i0 in range(0, n_i1, PEER_I1):
                group = range(i0, i0 + PEER_I1)
                gates = {ii: jnp.zeros((PEER_RB, PEER_LC), f32) for ii in group}
                for h in range(PEER_HEADS):
                    s2 = s2_ref[h * N_KEYS + rb * PEER_RB:h * N_KEYS + (rb + 1) * PEER_RB, lanes]
                    tau = tau_ref[h:h + 1, lanes]
                    for ii in group:
                        c = s2 + s1_ref[s1_rows[h][ii], lanes]
                        gates[ii] = gates[ii] + jnp.where(c >= tau, jnp.exp2(c), 0.0)
                for ii in group:
                    a = act[ii * N_KEYS + rb * PEER_RB:ii * N_KEYS + (rb + 1) * PEER_RB, lanes]
                    pieces[ii][rb] = (_gelu(a) * gates[ii]).astype(bf16)
        cols.append(jnp.concatenate([pc for row in pieces for pc in row], axis=0))
    acc_ref[...] += _dot(vt_ref[...], jnp.concatenate(cols, axis=1))

    @pl.when(j == pl.num_programs(1) - 1)
    def _():
        out_ref[...] = acc_ref[...].T + res_ref[...]


def _peer_dense(xt, ub, vt, layer, s1, s2, tau, res, *, tm, te):
    D, M = xt.shape
    E = ub.shape[1]
    once = dict(pipeline_mode=pl.Buffered(1))
    rspec = pl.BlockSpec((PEER_HEADS * N_KEYS, tm), lambda i, j: (0, i), **once)
    return pl.pallas_call(
        functools.partial(_peer_dense_kernel, tm=tm, te=te),
        grid=(M // tm, E // te),
        in_specs=[pl.BlockSpec((D, tm), lambda i, j: (0, i), **once),
                  pl.BlockSpec((None, te, D), lambda i, j: (layer, j, 0)),
                  pl.BlockSpec((None, D, te), lambda i, j: (layer, 0, j)),
                  rspec, rspec,
                  pl.BlockSpec((PEER_HEADS, tm), lambda i, j: (0, i), **once),
                  pl.BlockSpec((tm, D), lambda i, j: (i, 0), **once)],
        out_specs=pl.BlockSpec((tm, D), lambda i, j: (i, 0), **once),
        out_shape=jax.ShapeDtypeStruct((M, D), f32),
        scratch_shapes=[pltpu.VMEM((D, tm), f32)],
        compiler_params=_params(("parallel", "arbitrary")),
        name="peer_dense",
    )(xt, ub, vt, s1, s2, tau, res)


def _peer(x_all, norm_w, wq, keys, ub, vt, layer, *, tm_mm, tmr, tm, te):
    q, ht = _mm(x_all, wq, norm_w=norm_w, emit_h=bf16, transpose_h=True, tm=tm_mm, tn=wq.shape[1],
                name="peer_query")
    s1, s2, tau = _peer_route(q, keys, tmr=tmr)
    return _peer_dense(ht, ub, vt, layer, s1, s2, tau, x_all, tm=tm, te=te)


def _block_diag_ones():
    i = jnp.arange(D_A) // HD_A
    return (i[:, None] == i[None, :]).astype(bf16)


LANE = 128


def _pad_lanes(w):
    return jnp.pad(w, ((0, 0), (0, (-w.shape[1]) % LANE)))


def _pad_rows(w):
    return jnp.pad(w, ((0, (-w.shape[0]) % LANE), (0, 0)))


def kernel(x_prompt, x_sample, state_shift, state_wkv, state_pool, cache_kv_w128, cache_kv_w512, cache_kv_w2048,
           norm_mix, norm_ffn, norm_final, a_w_in, a_w_out, a_mu_rkv, a_mu_wag, a_w0, a_w1, a_w2, a_a0, a_a1,
           a_a2, a_g1, a_g2, a_k_k, a_k_a, a_r_k, a_gn_w, a_gn_b, b_w_pool, b_scale, c_w_in, c_w_out, p_w_q,
           p_sub_keys, p_u, p_v):
    T = x_prompt.shape[1]
    NS = x_sample.shape[0]
    TM = 768
    M = -(-(T + NS) // TM) * TM
    S_BLK = T // NS
    assert T % NS == 0 and T % ATTN_BLK == 0
    cb = lambda w: w.astype(bf16)
    row = lambda w: w.reshape(1, -1)

    x0 = jnp.concatenate([x_prompt.reshape(T, D_MODEL), x_sample.reshape(NS, D_MODEL),
                          jnp.zeros((M - T - NS, D_MODEL), f32)], axis=0)
    ub, vt = _peer_prep(p_u, p_v, te=512)
    bd = _block_diag_ones()

    w_in = cb(a_w_in[0])
    z0, h0 = _mm(x0, w_in, norm_w=norm_mix[0], emit_h=f32, tm=TM, tn=1024, name="even_in_proj")
    zs_prev = _mm(state_shift[0], w_in[:, :3 * D_A], tm=NS, tn=1024, name="even_in_proj_state")
    plist = [a_mu_wag[0], row(a_mu_rkv[0]), row(a_w0[0]), _pad_lanes(cb(a_w1[0])), _pad_rows(cb(a_w2[0])),
             row(a_a0[0]), _pad_lanes(cb(a_a1[0])), _pad_rows(cb(a_a2[0])), cb(a_g1[0]), cb(a_g2[0]),
             row(a_k_k[0]), row(a_k_a[0]), cb(b_w_pool[0]), row(b_scale[0]), bd]
    rp, lwp, kp, vp, kkp, ap, gp, obp = _even_mid_prompt(h0, z0, plist, T=T, tm=256)
    rs, lws, ks, vs, kks, as_, gs, obs = _even_mid_sample(
        h0, state_shift[0], z0, zs_prev, jnp.swapaxes(state_pool[0], 0, 1), plist, row0=S_BLK, n=NS)
    o_p, wkv_p = _wkv_prompt(rp, lwp, kp, vp, kkp, ap, T=T, tb=512, npair=8)
    o_s, wkv_s = _wkv_sample(state_wkv[0], rs, lws, ks, vs, kks, as_)
    post_c = (cb(a_w_out[0]), row(a_r_k[0]), row(a_gn_w[0]), row(a_gn_b[0]), bd)
    x1 = _even_post(x0, o_p, rp, kp, vp, gp, obp, *post_c, row0=0, rows=T, tm=256)
    x1 = _even_post(x1, o_s, rs, ks, vs, gs, obs, *post_c, row0=S_BLK, rows=NS, tm=NS)
    x1 = _peer(x1, norm_ffn[0], cb(p_w_q[0]), cb(p_sub_keys[0].reshape(2 * PEER_HEADS, N_KEYS, N_KEYS)),
               ub, vt, 0, tm_mm=TM, tmr=256, tm=TM, te=1024)

    z1 = _mm(x1, cb(c_w_in[0]), norm_w=norm_mix[1], tm=TM, tn=1024, name="odd_in_proj")
    att_p = _attn_prompt(z1, T=T)
    z1s = z1[T:T + NS]
    att_s = _attn_sample(z1s, (cache_kv_w128[0], cache_kv_w512[0], cache_kv_w2048[0]))
    att = jnp.concatenate([att_p, att_s, jnp.zeros((M - T - NS, D_C), f32)], axis=0)
    x2 = _mm(att, cb(c_w_out[0]), res=x1, tm=TM, tn=D_MODEL, name="odd_out_proj")
    x2 = _peer(x2, norm_ffn[1], cb(p_w_q[1]), cb(p_sub_keys[1].reshape(2 * PEER_HEADS, N_KEYS, N_KEYS)),
               ub, vt, 1, tm_mm=TM, tmr=256, tm=TM, te=1024)

    y_p = _rmsnorm(x2, norm_final, tm=512, row0=0, rows=T)
    y_s = _rmsnorm(x2, norm_final, tm=NS, row0=S_BLK, rows=NS)

    u_p = z0[T - POOL_BUF:T, 3 * D_A:]
    u_s = z0[T:T + NS, 3 * D_A:]
    pool_s = jnp.concatenate([state_pool[0][:, 1:], u_s[:, None, :]], axis=1)
    kv_p, kv_s = [], []
    for g, (win, _) in enumerate(ATTN_GROUPS):
        n = min(win, T)
        kv = z1[T - n:T + NS, g * 3 * D_C + D_C:g * 3 * D_C + 3 * D_C].reshape(n + NS, 2, H_C, HD_C)
        kv_p.append(kv[:n][None, None])
        kv_s.append(kv[n:][None, :, None])
    return (y_p[None], y_s[:, None, :],
            h0[T - 1][None, None], h0[T:T + NS][None],
            wkv_p[None, None], wkv_s[None],
            u_p[None, None], pool_s[None],
            kv_p[0], kv_s[0], kv_p[1], kv_s[1], kv_p[2], kv_s[2])
```

```python
import functools
import math

import jax
import jax.numpy as jnp
from jax import lax
from jax.experimental import pallas as pl
from jax.experimental.pallas import tpu as pltpu

f32 = jnp.float32
bf16 = jnp.bfloat16

D_MODEL = 2048
H_A, HD_A = 16, 64
D_A = H_A * HD_A
D_POOL = D_MODEL - D_A
POOL_WINDOWS = (2, 4, 8, 16)
POOL_GD = D_POOL // len(POOL_WINDOWS)
POOL_BUF = max(POOL_WINDOWS) - 1
GN_EPS = 64e-5
NORM_EPS = 1e-6
ATTN_GROUPS = ((128, 1), (512, 4), (2048, 16))
H_C, HD_C = 8, 128
D_C = H_C * HD_C
N_KEYS = 128
N_EXPERTS = N_KEYS * N_KEYS
PEER_HEADS = 8
PEER_TOPK = 16

VMEM_LIMIT = 56 * 1024 * 1024
WKV_CHUNK = 64
ATTN_BLK = 2048
ATTN_UNROLL = 8
NEG_INF = float("-inf")
LOG2E = math.log2(math.e)


def _params(sem, vmem=VMEM_LIMIT, flags=None):
    return pltpu.CompilerParams(dimension_semantics=sem, vmem_limit_bytes=vmem, flags=flags)


def _dot(a, b):
    return jnp.dot(a, b, preferred_element_type=f32)


def _dot_nt(a, b):
    return lax.dot_general(a, b, (((1,), (1,)), ((), ())), preferred_element_type=f32)


def _dot_tn(a, b):
    return lax.dot_general(a, b, (((0,), (0,)), ((), ())), preferred_element_type=f32)


def _split(x):
    hi = x.astype(bf16)
    lo = (x - hi.astype(f32)).astype(bf16)
    return hi, lo


def _dot3(a, b, dot=_dot):
    ah, al = _split(a)
    bh, bl = _split(b)
    return dot(ah, bh) + (dot(ah, bl) + dot(al, bh))


def _head_sum(x, bd):
    hi, lo = _split(x)
    sh, sl = _split(_dot_nt(hi, bd) + _dot_nt(lo, bd))
    return _dot(sh, bd) + _dot(sl, bd)


def _gelu(x):
    return 0.5 * x * (1.0 + lax.erf(x * 0.7071067811865476))


def _mm_kernel(*refs, has_norm, has_res, emit_h, transpose_h):
    it = iter(refs)
    x_ref, w_ref = next(it), next(it)
    g_ref = next(it) if has_norm else None
    res_ref = next(it) if has_res else None
    o_ref = next(it)
    h_ref = next(it) if emit_h else None
    xb_ref = next(it)

    @pl.when(pl.program_id(1) == 0)
    def _():
        x = x_ref[...]
        if has_norm:
            ms = jnp.mean(x * x, axis=-1, keepdims=True)
            x = (x * lax.rsqrt(ms + NORM_EPS)) * g_ref[...]
            if emit_h:
                h_ref[...] = (x.T if transpose_h else x).astype(h_ref.dtype)
        xb_ref[...] = x.astype(bf16)

    acc = _dot(xb_ref[...], w_ref[...])
    if has_res:
        acc = acc + res_ref[...]
    o_ref[...] = acc


def _mm(x, w, *, norm_w=None, res=None, emit_h=None, transpose_h=False, tm, tn, row0=0, rows=None, name="proj"):
    K = x.shape[1]
    N = w.shape[1]
    rows = x.shape[0] if rows is None else rows
    assert rows % tm == 0 and N % tn == 0
    w_mode = dict(pipeline_mode=pl.Buffered(1)) if tn == N else {}
    in_specs = [pl.BlockSpec((tm, K), lambda i, j: (i + row0, 0)),
                pl.BlockSpec((K, tn), lambda i, j: (0, j), **w_mode)]
    args = [x, w]
    if norm_w is not None:
        in_specs.append(pl.BlockSpec((1, K), lambda i, j: (0, 0)))
        args.append(norm_w.reshape(1, K))
    if res is not None:
        in_specs.append(pl.BlockSpec((tm, tn), lambda i, j: (i + row0, j)))
        args.append(res)
    out_shape = [jax.ShapeDtypeStruct((rows, N), f32)]
    out_specs = [pl.BlockSpec((tm, tn), lambda i, j: (i, j))]
    if emit_h is not None and transpose_h:
        out_shape.append(jax.ShapeDtypeStruct((K, rows), emit_h))
        out_specs.append(pl.BlockSpec((K, tm), lambda i, j: (0, i)))
    elif emit_h is not None:
        out_shape.append(jax.ShapeDtypeStruct((rows, K), emit_h))
        out_specs.append(pl.BlockSpec((tm, K), lambda i, j: (i, 0)))
    outs = pl.pallas_call(
        functools.partial(_mm_kernel, has_norm=norm_w is not None, has_res=res is not None,
                          emit_h=emit_h is not None, transpose_h=transpose_h),
        grid=(rows // tm, N // tn),
        in_specs=in_specs, out_specs=out_specs, out_shape=out_shape,
        scratch_shapes=[pltpu.VMEM((tm, K), bf16)],
        compiler_params=_params(("parallel", "arbitrary")),
        name=name,
    )(*args)
    return outs if emit_h is not None else outs[0]


def _rms_kernel(x_ref, g_ref, o_ref):
    x = x_ref[...]
    ms = jnp.mean(x * x, axis=-1, keepdims=True)
    o_ref[...] = (x * lax.rsqrt(ms + NORM_EPS)) * g_ref[...]


def _rmsnorm(x, g, *, tm, row0, rows):
    K = x.shape[1]
    return pl.pallas_call(
        _rms_kernel, grid=(rows // tm,),
        in_specs=[pl.BlockSpec((tm, K), lambda i: (i + row0, 0)), pl.BlockSpec((1, K), lambda i: (0, 0))],
        out_specs=pl.BlockSpec((tm, K), lambda i: (i, 0)),
        out_shape=jax.ShapeDtypeStruct((rows, K), f32),
        compiler_params=_params(("parallel",)),
        name="rmsnorm",
    )(x, g.reshape(1, K))


def _even_token_math(h, hs, zc, zs, P):
    dh = hs - h
    mu = P["mu_wag"]
    xw = (h + dh * mu[0:1]).astype(bf16)
    xa = (h + dh * mu[1:2]).astype(bf16)
    xg = (h + dh * mu[2:3]).astype(bf16)
    tw = jnp.tanh(_dot(xw, P["w1"])).astype(bf16)
    wl = P["w0"] + _dot(tw, P["w2"])
    w_log = -jax.nn.softplus(-wl) - 0.5
    lw = -jnp.exp(w_log)
    a = jax.nn.sigmoid(P["a0"] + _dot(_dot(xa, P["a1"]).astype(bf16), P["a2"]))
    g = _dot(jax.nn.sigmoid(_dot(xg, P["g1"])).astype(bf16), P["g2"])
    rkv = zc + (zs - zc) * P["mu_rkv"]
    r, k, v = rkv[:, :D_A], rkv[:, D_A:2 * D_A], rkv[:, 2 * D_A:]
    kk = k * P["k_k"]
    nrm = jnp.sqrt(_head_sum(kk * kk, P["bd"]))
    kkn = kk / jnp.maximum(nrm, 1e-12)
    k2 = k * (1.0 + (a - 1.0) * P["k_a"])
    return r, lw, k2, v, kkn, a, g


def _pool_project(pm, P):
    outs = []
    for gi in range(len(POOL_WINDOWS)):
        c = slice(gi * POOL_GD, (gi + 1) * POOL_GD)
        outs.append(_dot(pm[:, c].astype(bf16), P["w_pool"][gi]))
    return jnp.concatenate(outs, axis=-1) * P["pool_scale"]


_EVEN_PARAM_NAMES = ("mu_wag", "mu_rkv", "w0", "w1", "w2", "a0", "a1", "a2", "g1", "g2", "k_k", "k_a",
                     "w_pool", "pool_scale", "bd")


def _load_params(refs):
    return {n: r[...] for n, r in zip(_EVEN_PARAM_NAMES, refs)}


def _even_mid_prompt_kernel(h_ref, hp_ref, z_ref, zp_ref, *rest, tm):
    prefs, outs = rest[:len(_EVEN_PARAM_NAMES)], rest[len(_EVEN_PARAM_NAMES):]
    P = _load_params(prefs)
    first = pl.program_id(0) == 0
    h = h_ref[...]
    z = z_ref[...]
    zc, u = z[:, :3 * D_A], z[:, 3 * D_A:]
    hprev = jnp.where(first, 0.0, hp_ref[15:16, :])
    zprev = jnp.where(first, 0.0, zp_ref[15:16, :3 * D_A])
    row = lax.broadcasted_iota(jnp.int32, (tm, 1), 0)
    hs = jnp.where(row == 0, hprev, pltpu.roll(h, 1, axis=0))
    zs = jnp.where(row == 0, zprev, pltpu.roll(zc, 1, axis=0))
    prow = lax.broadcasted_iota(jnp.int32, (tm, POOL_GD), 0)
    r, lw, k2, v, kkn, a, g = _even_token_math(h, hs, zc, zs, P)

    uprev = jnp.where(first, 0.0, zp_ref[:, 3 * D_A:])
    pos = pl.program_id(0) * tm + prow
    means = []
    for gi, win in enumerate(POOL_WINDOWS):
        c = slice(gi * POOL_GD, (gi + 1) * POOL_GD)
        s = jnp.concatenate([uprev[:, c], u[:, c]], axis=0)
        sh = 1
        while sh < win:
            s = s + pltpu.roll(s, sh, axis=0)
            sh *= 2
        cnt = jnp.minimum(pos + 1, win).astype(f32)
        means.append(s[16:, :] / cnt)
    ob = _pool_project(jnp.concatenate(means, axis=-1) - u, P)
    for o_ref, val in zip(outs, (r, lw, k2, v, kkn, a, g, ob)):
        o_ref[...] = val


def _even_mid_sample_kernel(h_ref, hs_ref, z_ref, zs_ref, buf_ref, *rest):
    prefs, outs = rest[:len(_EVEN_PARAM_NAMES)], rest[len(_EVEN_PARAM_NAMES):]
    P = _load_params(prefs)
    h = h_ref[...]
    z = z_ref[...]
    zc, u = z[:, :3 * D_A], z[:, 3 * D_A:]
    r, lw, k2, v, kkn, a, g = _even_token_math(h, hs_ref[...], zc, zs_ref[...], P)
    means = []
    for gi, win in enumerate(POOL_WINDOWS):
        c = slice(gi * POOL_GD, (gi + 1) * POOL_GD)
        s = u[:, c]
        for j in range(POOL_BUF - (win - 1), POOL_BUF):
            s = s + buf_ref[j, :, c]
        means.append(s / float(win))
    ob = _pool_project(jnp.concatenate(means, axis=-1) - u, P)
    for o_ref, val in zip(outs, (r, lw, k2, v, kkn, a, g, ob)):
        o_ref[...] = val


def _const_spec(shape):
    nd = len(shape)
    return pl.BlockSpec(shape, lambda i, _nd=nd: (0,) * _nd, pipeline_mode=pl.Buffered(1))


def _even_mid_prompt(h_all, z_all, plist, *, T, tm):
    nb = tm // 16
    in_specs = [pl.BlockSpec((tm, D_MODEL), lambda i: (i, 0)),
                pl.BlockSpec((16, D_MODEL), lambda i: (jnp.maximum(i * nb - 1, 0), 0)),
                pl.BlockSpec((tm, 4 * D_A), lambda i: (i, 0)),
                pl.BlockSpec((16, 4 * D_A), lambda i: (jnp.maximum(i * nb - 1, 0), 0))]
    in_specs += [_const_spec(p.shape) for p in plist]
    return pl.pallas_call(
        functools.partial(_even_mid_prompt_kernel, tm=tm),
        grid=(T // tm,), in_specs=in_specs,
        out_specs=[pl.BlockSpec((tm, D_A), lambda i: (i, 0))] * 8,
        out_shape=[jax.ShapeDtypeStruct((T, D_A), f32)] * 8,
        compiler_params=_params(("parallel",)),
        name="even_mid_prompt",
    )(h_all, h_all, z_all, z_all, *plist)


def _even_mid_sample(h_all, hs, z_all, zs, buf_t, plist, *, row0, n):
    in_specs = [pl.BlockSpec((n, D_MODEL), lambda i: (row0, 0)),
                pl.BlockSpec((n, D_MODEL), lambda i: (0, 0)),
                pl.BlockSpec((n, 4 * D_A), lambda i: (row0, 0)),
                pl.BlockSpec((n, 3 * D_A), lambda i: (0, 0)),
                pl.BlockSpec((POOL_BUF, n, D_POOL), lambda i: (0, 0, 0))]
    in_specs += [_const_spec(p.shape) for p in plist]
    return pl.pallas_call(
        _even_mid_sample_kernel, grid=(1,), in_specs=in_specs,
        out_specs=[pl.BlockSpec((n, D_A), lambda i: (0, 0))] * 8,
        out_shape=[jax.ShapeDtypeStruct((n, D_A), f32)] * 8,
        compiler_params=_params(("arbitrary",)),
        name="even_mid_sample",
    )(h_all, hs, z_all, zs, buf_t, *plist)


def _wkv_prompt_kernel(r_ref, lw_ref, k_ref, v_ref, kk_ref, a_ref, o_ref, s_out_ref, S_ref, *, tb, npair):
    C = WKV_CHUNK
    t = pl.program_id(1)

    @pl.when(t == 0)
    def _():
        S_ref[...] = jnp.zeros_like(S_ref)

    ri = lax.broadcasted_iota(jnp.int32, (C, C), 0)
    ci = lax.broadcasted_iota(jnp.int32, (C, C), 1)
    strict = ri > ci
    incl = ri >= ci
    eye = (ri == ci).astype(f32)
    rows = lax.broadcasted_iota(jnp.int32, (C, 2 * HD_A), 0)

    nh = 2 * npair
    heads = range(nh)

    def prep(sl, lanes):
        r, lw, k, v, kk, a = (x[sl, lanes] for x in (r_ref, lw_ref, k_ref, v_ref, kk_ref, a_ref))
        cum = lw
        sh = 1
        while sh < C:
            cum = cum + jnp.where(rows >= sh, pltpu.roll(cum, sh, axis=0), 0.0)
            sh *= 2
        cum_c = cum[C - 1:C, :]
        b = kk * a
        e_neg = jnp.exp(-cum)
        e_rem = jnp.exp(cum_c - cum)
        return dict(at=-kk * jnp.exp(cum - lw), rt=r * jnp.exp(cum), bt=b * e_neg, kt=k * e_neg,
                    bh=b * e_rem, kh=k * e_rem, w_c=jnp.exp(cum_c), v=v)

    def chunk(c, carry):
        off = pl.multiple_of(c * C, C)
        sl = pl.ds(off, C)
        pairs = [prep(sl, slice(p * 2 * HD_A, (p + 1) * 2 * HD_A)) for p in range(npair)]
        S0 = [S_ref[h] for h in heads]

        def head(name, h):
            return pairs[h // 2][name][:, (h % 2) * HD_A:(h % 2 + 1) * HD_A]

        ar = [jnp.concatenate([head("at", h), head("rt", h)], axis=0) for h in heads]
        bk = [jnp.concatenate([head("bt", h), head("kt", h)], axis=0) for h in heads]
        vh = [head("v", h) for h in heads]
        m4s = [_dot3(ar[h], jnp.concatenate([bk[h], S0[h]], axis=0), _dot_nt) for h in heads]
        m4 = [x[:, :2 * C] for x in m4s]
        sar = [x[:, 2 * C:] for x in m4s]
        a_ab = [jnp.where(strict, m4[h][:C, :C], 0.0) for h in heads]
        a_ak = [jnp.where(strict, m4[h][:C, C:], 0.0) for h in heads]
        a_rb = [jnp.where(incl, m4[h][C:, :C], 0.0) for h in heads]
        a_rk = [jnp.where(incl, m4[h][C:, C:], 0.0) for h in heads]
        av = [_dot3(jnp.concatenate([a_ak[h], a_rk[h]], axis=0), vh[h]) for h in heads]
        tinv = [eye + a_ab[h] for h in heads]
        pw = [_dot3(a_ab[h], a_ab[h]) for h in heads]
        n_sq = 2
        while n_sq * 2 < C:
            x = [_dot3(pw[h], jnp.concatenate([tinv[h], pw[h]], axis=1)) for h in heads]
            tinv = [tinv[h] + x[h][:, :C] for h in heads]
            pw = [x[h][:, C:] for h in heads]
            n_sq *= 2
        tinv = [tinv[h] + _dot3(pw[h], tinv[h]) for h in heads]
        u = [_dot3(tinv[h], sar[h][:C] + av[h][:C]) for h in heads]
        o = [sar[h][C:] + av[h][C:] + _dot3(a_rb[h], u[h]) for h in heads]
        s_new = []
        for h in heads:
            uv = jnp.concatenate([u[h], vh[h]], axis=0)
            bkh = jnp.concatenate([head("bh", h), head("kh", h)], axis=0)
            s_new.append(S0[h] * head("w_c", h) + _dot3(uv, bkh, _dot_tn))
        for h in heads:
            S_ref[h] = s_new[h]
        for p in range(npair):
            o_ref[sl, p * 2 * HD_A:(p + 1) * 2 * HD_A] = jnp.concatenate([o[2 * p], o[2 * p + 1]], axis=1)
        return carry

    lax.fori_loop(0, tb // C, chunk, 0)

    @pl.when(t == pl.num_programs(1) - 1)
    def _():
        s_out_ref[...] = S_ref[...]


def _wkv_prompt(r, lw, k2, v, kkn, a, *, T, tb, npair):
    spec = pl.BlockSpec((tb, 2 * HD_A * npair), lambda j, t: (t, j))
    nh = 2 * npair
    return pl.pallas_call(
        functools.partial(_wkv_prompt_kernel, tb=tb, npair=npair),
        grid=(H_A // nh, T // tb),
        in_specs=[spec] * 6,
        out_specs=[spec, pl.BlockSpec((nh, HD_A, HD_A), lambda j, t: (j, 0, 0))],
        out_shape=[jax.ShapeDtypeStruct((T, D_A), f32), jax.ShapeDtypeStruct((H_A, HD_A, HD_A), f32)],
        scratch_shapes=[pltpu.VMEM((nh, HD_A, HD_A), f32)],
        compiler_params=_params(("parallel", "arbitrary")),
        name="wkv_prompt",
    )(r, lw, k2, v, kkn, a)


def _wkv_sample_kernel(s_ref, r_ref, lw_ref, k_ref, kk_ref, a_ref, v_ref, o_ref, s_out_ref):
    S = s_ref[0]
    kk = kk_ref[0]
    sa = jnp.sum(S * (-kk), axis=-1, keepdims=True)
    S2 = S * jnp.exp(lw_ref[0]) + sa * (kk * a_ref[0]) + v_ref[0] * k_ref[0]
    s_out_ref[0] = S2
    o_ref[0] = jnp.sum(S2 * r_ref[0], axis=-1, keepdims=True)


def _wkv_sample(S0, r, lw, k2, v, kkn, a):
    n = S0.shape[0]
    row = lambda x: x.reshape(n, H_A, 1, HD_A)
    rspec = pl.BlockSpec((1, H_A, 1, HD_A), lambda b: (b, 0, 0, 0))
    cspec = pl.BlockSpec((1, H_A, HD_A, 1), lambda b: (b, 0, 0, 0))
    sspec = pl.BlockSpec((1, H_A, HD_A, HD_A), lambda b: (b, 0, 0, 0))
    o, S = pl.pallas_call(
        _wkv_sample_kernel, grid=(n,),
        in_specs=[sspec, rspec, rspec, rspec, rspec, rspec, cspec],
        out_specs=[cspec, sspec],
        out_shape=[jax.ShapeDtypeStruct((n, H_A, HD_A, 1), f32), jax.ShapeDtypeStruct(S0.shape, f32)],
        compiler_params=_params(("parallel",)),
        name="wkv_sample",
    )(S0, row(r), row(lw), row(k2), row(kkn), row(a), v.reshape(n, H_A, HD_A, 1))
    return o.reshape(n, D_A), S


def _even_post_kernel(x_ref, o_ref, r_ref, k_ref, v_ref, g_ref, ob_ref, wo_ref, rk_ref, gnw_ref, gnb_ref, bd_ref,
                      out_ref):
    bd = bd_ref[...]
    o = o_ref[...]
    inv = 1.0 / HD_A
    mu = _head_sum(o, bd) * inv
    d = o - mu
    var = _head_sum(d * d, bd) * inv
    on = d * lax.rsqrt(var + GN_EPS) * gnw_ref[...] + gnb_ref[...]
    v = v_ref[...]
    bonus = _head_sum(r_ref[...] * k_ref[...] * rk_ref[...], bd) * v
    oa = ((on + bonus) * g_ref[...]).astype(bf16)
    y = _dot(oa, wo_ref[:D_A, :]) + _dot(ob_ref[...].astype(bf16), wo_ref[D_A:, :])
    out_ref[...] = x_ref[...] + y


def _even_post(x_all, o, r, k2, v, g, ob, wo, rk, gnw, gnb, bd, *, row0, rows, tm):
    a_spec = pl.BlockSpec((tm, D_A), lambda i: (i, 0))
    x_spec = pl.BlockSpec((tm, D_MODEL), lambda i: (i + row0, 0))
    consts = [wo, rk, gnw, gnb, bd]
    return pl.pallas_call(
        _even_post_kernel, grid=(rows // tm,),
        in_specs=[x_spec] + [a_spec] * 6 + [_const_spec(c.shape) for c in consts],
        out_specs=x_spec,
        out_shape=jax.ShapeDtypeStruct(x_all.shape, f32),
        input_output_aliases={0: 0},
        compiler_params=_params(("parallel",)),
        name="even_post",
    )(x_all, o, r, k2, v, g, ob, *consts)


def _attn_prompt_kernel(*refs):
    in_refs, out_ref, og_ref, lg_ref = refs[:15], refs[15], refs[16], refs[17]
    n = pl.program_id(1)
    has_prev = n > 0
    scale = HD_C ** -0.5
    Q = 128
    ri = lax.broadcasted_iota(jnp.int32, (Q, Q), 0)
    ci = lax.broadcasted_iota(jnp.int32, (Q, Q), 1)
    mask_prev0 = ci >= ri
    mask_cur = ci <= ri

    for g, (_, dil) in enumerate(ATTN_GROUPS):
        q_ref, kp_ref, k_ref, vp_ref, v_ref = in_refs[5 * g:5 * g + 5]
        nblk = ATTN_BLK // (Q * dil)

        def units(it, carry, q_ref=q_ref, kp_ref=kp_ref, k_ref=k_ref, vp_ref=vp_ref, v_ref=v_ref,
                  dil=dil, nblk=nblk, g=g):
            us = [it * ATTN_UNROLL + k for k in range(ATTN_UNROLL)]
            rows, q, kc, vc, kp, vp, valid_prev = [], [], [], [], [], [], []
            for u in us:
                c = u // nblk
                m = u % nblk
                r = pl.ds(m * (Q * dil) + c, Q, stride=dil)
                in_blk = m > 0
                r_a = pl.ds(jnp.maximum(m - 1, 0) * (Q * dil) + c, Q, stride=dil)
                r_b = pl.ds((nblk - 1) * (Q * dil) + c, Q, stride=dil)
                rows.append(r)
                q.append(q_ref[r, :].astype(bf16))
                kc.append(k_ref[r, :].astype(bf16))
                vc.append(v_ref[r, :].astype(bf16))
                kp.append(jnp.where(in_blk, k_ref[r_a, :], kp_ref[r_b, :]).astype(bf16))
                vp.append(jnp.where(in_blk, v_ref[r_a, :], vp_ref[r_b, :]).astype(bf16))
                valid_prev.append(jnp.logical_or(in_blk, has_prev))
            n = range(ATTN_UNROLL)
            s_p = [_dot_nt(q[i], kp[i]) * scale for i in n]
            s_c = [_dot_nt(q[i], kc[i]) * scale for i in n]
            s_p = [jnp.where(jnp.logical_and(mask_prev0, valid_prev[i]), s_p[i], NEG_INF) for i in n]
            s_c = [jnp.where(mask_cur, s_c[i], NEG_INF) for i in n]
            mx = [jnp.maximum(jnp.max(s_p[i], axis=1, keepdims=True), jnp.max(s_c[i], axis=1, keepdims=True))
                  for i in n]
            p_p = [jnp.exp(s_p[i] - mx[i]) for i in n]
            p_c = [jnp.exp(s_c[i] - mx[i]) for i in n]
            den = [jnp.sum(p_p[i], axis=1, keepdims=True) + jnp.sum(p_c[i], axis=1, keepdims=True) for i in n]
            o = [(_dot(p_p[i].astype(bf16), vp[i]) + _dot(p_c[i].astype(bf16), vc[i])) / den[i] for i in n]
            lse = [mx[i] + jnp.log(den[i]) for i in n]
            for i in n:
                og_ref[g, rows[i], :] = o[i]
                lg_ref[g, rows[i], :] = jnp.broadcast_to(lse[i], (Q, HD_C))
            return carry

        lax.fori_loop(0, dil * nblk // ATTN_UNROLL, units, 0)

    l0, l1, l2 = lg_ref[0], lg_ref[1], lg_ref[2]
    mx = jnp.maximum(jnp.maximum(l0, l1), l2)
    e0, e1, e2 = jnp.exp(l0 - mx), jnp.exp(l1 - mx), jnp.exp(l2 - mx)
    out_ref[...] = (e0 * og_ref[0] + e1 * og_ref[1] + e2 * og_ref[2]) / (e0 + e1 + e2)


def _attn_prompt(z, *, T):
    nb = T // ATTN_BLK
    in_specs = []
    for g in range(len(ATTN_GROUPS)):
        def col(j, g=g):
            return lambda h, n: (n, g * 3 * H_C + j * H_C + h)

        def col_prev(j, g=g):
            return lambda h, n: (jnp.maximum(n - 1, 0), g * 3 * H_C + j * H_C + h)
        blk = (ATTN_BLK, HD_C)
        in_specs += [pl.BlockSpec(blk, col(0)), pl.BlockSpec(blk, col_prev(1)), pl.BlockSpec(blk, col(1)),
                     pl.BlockSpec(blk, col_prev(2)), pl.BlockSpec(blk, col(2))]
    return pl.pallas_call(
        _attn_prompt_kernel, grid=(H_C, nb),
        in_specs=in_specs,
        out_specs=pl.BlockSpec((ATTN_BLK, HD_C), lambda h, n: (n, h)),
        out_shape=jax.ShapeDtypeStruct((T, D_C), f32),
        scratch_shapes=[pltpu.VMEM((3, ATTN_BLK, HD_C), f32), pltpu.VMEM((3, ATTN_BLK, HD_C), f32)],
        compiler_params=_params(("parallel", "arbitrary")),
        name="attn_prompt",
    )(*([z] * 15))


ATTN_SAMPLE_NB = 4


def _attn_sample_kernel(z_ref, c0_ref, c1_ref, c2_ref, out_ref):
    scale = HD_C ** -0.5
    crefs = (c0_ref, c1_ref, c2_ref)
    for b in range(ATTN_SAMPLE_NB):
        for h in range(H_C):
            outs, lses = [], []
            for g in range(len(ATTN_GROUPS)):
                base = g * 3 * D_C + h * HD_C
                q = z_ref[b, :, base:base + HD_C]
                kn = z_ref[b, :, base + D_C:base + D_C + HD_C]
                vn = z_ref[b, :, base + 2 * D_C:base + 2 * D_C + HD_C]
                K = crefs[g][b, :, 0, 0, h, :]
                V = crefs[g][b, :, 0, 1, h, :]
                s_c = jnp.sum(K * q, axis=1, keepdims=True) * scale
                s_n = jnp.sum(kn * q, axis=1, keepdims=True) * scale
                mx = jnp.maximum(jnp.max(s_c, axis=0, keepdims=True), s_n)
                p_c = jnp.exp(s_c - mx)
                p_n = jnp.exp(s_n - mx)
                den = jnp.sum(p_c, axis=0, keepdims=True) + p_n
                outs.append((jnp.sum(p_c * V, axis=0, keepdims=True) + p_n * vn) / den)
                lses.append(mx + jnp.log(den))
            mx = jnp.maximum(jnp.maximum(lses[0], lses[1]), lses[2])
            es = [jnp.exp(l - mx) for l in lses]
            out_ref[b, :, h * HD_C:(h + 1) * HD_C] = (es[0] * outs[0] + es[1] * outs[1] + es[2] * outs[2]) / (
                es[0] + es[1] + es[2])


def _attn_sample(z_s, caches):
    n = z_s.shape[0]
    nb = ATTN_SAMPLE_NB
    assert n % nb == 0
    in_specs = [pl.BlockSpec((nb, 1, 9 * D_C), lambda b: (b, 0, 0))]
    args = [z_s.reshape(n, 1, 9 * D_C)]
    for (win, dil), c in zip(ATTN_GROUPS, caches):
        assert c.shape[1] == win
        args.append(c.reshape(n, win // dil, dil, 2, H_C, HD_C))
        in_specs.append(pl.BlockSpec((nb, win // dil, 1, 2, H_C, HD_C), lambda b: (b, 0, 0, 0, 0, 0)))
    out = pl.pallas_call(
        _attn_sample_kernel, grid=(n // nb,), in_specs=in_specs,
        out_specs=pl.BlockSpec((nb, 1, D_C), lambda b: (b, 0, 0)),
        out_shape=jax.ShapeDtypeStruct((n, 1, D_C), f32),
        compiler_params=_params(("parallel",)),
        name="attn_sample",
    )(*args)
    return out.reshape(n, D_C)


def _top16_rows(s):
    rows = lax.broadcasted_iota(jnp.int32, s.shape, 0)
    work = s
    vals = []
    for _ in range(PEER_TOPK):
        m = jnp.max(work, axis=0, keepdims=True)
        idx = jnp.min(jnp.where(work == m, rows, s.shape[0]), axis=0, keepdims=True)
        work = jnp.where(rows == idx, NEG_INF, work)
        vals.append(m)
    return vals, jnp.logical_and(work == NEG_INF, s != NEG_INF)


def _oddeven_mergesort_pairs(n):
    pairs = []

    def merge(lo, hi, r):
        step = r * 2
        if step < hi - lo:
            merge(lo, hi, step)
            merge(lo + r, hi, step)
            pairs.extend((i, i + r) for i in range(lo + r, hi - r, step))
        else:
            pairs.append((lo, lo + r))

    def sort(lo, hi):
        if hi - lo >= 1:
            mid = lo + (hi - lo) // 2
            sort(lo, mid)
            sort(mid + 1, hi)
            merge(lo, hi, 1)

    sort(0, n - 1)
    return pairs


SUBLANES = 8


def _top16_sorted(s):
    assert s.shape[0] == PEER_TOPK * SUBLANES
    t = [s[SUBLANES * k:SUBLANES * (k + 1), :] for k in range(PEER_TOPK)]
    for i, j in _oddeven_mergesort_pairs(PEER_TOPK):
        t[i], t[j] = jnp.maximum(t[i], t[j]), jnp.minimum(t[i], t[j])
    shift = 1
    while shift < SUBLANES:
        other = [pltpu.roll(x, shift, axis=0) for x in t]
        t = [jnp.maximum(t[i], other[PEER_TOPK - 1 - i]) for i in range(PEER_TOPK)]
        d = PEER_TOPK // 2
        while d >= 1:
            for i in range(PEER_TOPK):
                if i & d == 0:
                    t[i], t[i + d] = jnp.maximum(t[i], t[i + d]), jnp.minimum(t[i], t[i + d])
            d //= 2
        shift *= 2
    thr = t[PEER_TOPK - 1][0:1, :]
    member = s >= thr
    count = jnp.sum(member.astype(f32), axis=0, keepdims=True)
    return [x[0:1, :] for x in t], member, count == float(PEER_TOPK)


def _top16_rows_distinct(ss):
    work = list(ss)
    vals = [[] for _ in ss]
    for _ in range(PEER_TOPK):
        for i in range(len(ss)):
            m = jnp.max(work[i], axis=0, keepdims=True)
            work[i] = jnp.where(work[i] == m, NEG_INF, work[i])
            vals[i].append(m)
    out = []
    for s, w, v in zip(ss, work, vals):
        member = jnp.logical_and(w == NEG_INF, s != NEG_INF)
        count = jnp.sum(member.astype(f32), axis=0, keepdims=True)
        out.append((v, member, count == float(PEER_TOPK)))
    return out


def _peer_route_kernel(q_ref, keys_ref, s1_ref, s2_ref, tau_ref):
    n = q_ref.shape[0]

    def route_head(h, exact):
        if exact:
            top16 = lambda ss: [_top16_rows(s) + (None,) for s in ss]
            top16_scores = top16
        else:
            top16 = _top16_rows_distinct
            top16_scores = lambda ss: [_top16_sorted(s) for s in ss]
        scores = []
        for p in range(2):
            hp = 2 * h + p
            qs = q_ref[:, hp * N_KEYS:(hp + 1) * N_KEYS].astype(bf16)
            scores.append(_dot_nt(keys_ref[hp], qs) * LOG2E)
        sm, vals, oks = [], [], []
        for s, (v, member, ok) in zip(scores, top16_scores(scores)):
            sm.append(jnp.where(member, s, NEG_INF))
            vals.append(v)
            oks.append(ok)

        def pair_sums(z1, z2):
            pieces = []
            for r1 in range(PEER_TOPK):
                cnt = PEER_TOPK // (r1 + 1)
                pieces.append(z1[r1] + jnp.concatenate(z2[:cnt], axis=0))
            npad = (-sum(p.shape[0] for p in pieces)) % 8
            pieces.append(jnp.full((npad, n), NEG_INF, f32))
            return jnp.concatenate(pieces, axis=0)

        z1 = [v - vals[0][0] for v in vals[0]]
        z2 = [v - vals[1][0] for v in vals[1]]
        (top, chosen, ok), = top16([pair_sums(z1, z2)])
        oks.append(ok)
        log_z = jnp.log2(sum(jnp.exp2(t) for t in top))
        z1 = [z - log_z for z in z1]
        sums = pair_sums(z1, z2)
        if exact:
            (top, _, _), = top16([sums])
            tau = top[-1]
        else:
            tau = jnp.min(jnp.where(chosen, sums, jnp.inf), axis=0, keepdims=True)
            count = jnp.sum((sums >= tau).astype(f32), axis=0, keepdims=True)
            oks.append(count == float(PEER_TOPK))
        rows = slice(h * N_KEYS, (h + 1) * N_KEYS)
        s1_ref[rows, :] = (sm[0] - vals[0][0]) - log_z
        s2_ref[rows, :] = sm[1] - vals[1][0]
        tau_ref[h:h + 1, :] = tau
        if exact:
            return None
        all_ok = functools.reduce(jnp.logical_and, oks)
        return jnp.min(all_ok.astype(f32)) > 0.5

    tie_free = [route_head(h, exact=False) for h in range(PEER_HEADS)]
    for h in range(PEER_HEADS):
        pl.when(jnp.logical_not(tie_free[h]))(functools.partial(route_head, h, True))


def _peer_route(q, keys, *, tmr):
    M = q.shape[0]
    big = jax.ShapeDtypeStruct((PEER_HEADS * N_KEYS, M), f32)
    bspec = pl.BlockSpec((PEER_HEADS * N_KEYS, tmr), lambda i: (0, i))
    return pl.pallas_call(
        _peer_route_kernel, grid=(M // tmr,),
        in_specs=[pl.BlockSpec((tmr, 2 * PEER_HEADS * N_KEYS), lambda i: (i, 0)),
                  _const_spec(keys.shape)],
        out_specs=[bspec] * 2 + [pl.BlockSpec((PEER_HEADS, tmr), lambda i: (0, i))],
        out_shape=[big] * 2 + [jax.ShapeDtypeStruct((PEER_HEADS, M), f32)],
        compiler_params=_params(("parallel",)),
        name="peer_route",
    )(q, keys)


def _peer_prep_kernel(u_ref, v_ref, ub_ref, vt_ref):
    ub_ref[...] = u_ref[...].astype(bf16)
    vt_ref[...] = v_ref[...].T.astype(bf16)


def _peer_prep(u_tab, v_tab, *, te):
    L, E, D = u_tab.shape
    return pl.pallas_call(
        _peer_prep_kernel, grid=(L, E // te),
        in_specs=[pl.BlockSpec((None, te, D), lambda l, j: (l, j, 0))] * 2,
        out_specs=[pl.BlockSpec((None, te, D), lambda l, j: (l, j, 0)),
                   pl.BlockSpec((None, D, te), lambda l, j: (l, 0, j))],
        out_shape=[jax.ShapeDtypeStruct((L, E, D), bf16), jax.ShapeDtypeStruct((L, D, E), bf16)],
        compiler_params=_params(("parallel", "parallel")),
        name="peer_prep",
    )(u_tab, v_tab)


PEER_RB = 32
PEER_LC = 256
PEER_I1 = 4


def _peer_dense_kernel(xt_ref, u_ref, vt_ref, s1_ref, s2_ref, tau_ref, res_ref, out_ref, acc_ref, *, tm, te):
    j = pl.program_id(1)
    n_i1 = te // N_KEYS
    n_rb = N_KEYS // PEER_RB

    @pl.when(j == 0)
    def _():
        acc_ref[...] = jnp.zeros_like(acc_ref)

    chunks = [slice(c * PEER_LC, (c + 1) * PEER_LC) for c in range(tm // PEER_LC)]
    s1_rows = [[pl.ds(h * N_KEYS + j * n_i1 + ii, 1) for ii in range(n_i1)] for h in range(PEER_HEADS)]
    act = _dot(u_ref[...], xt_ref[...])
    cols = []
    for lanes in chunks:
        pieces = [[None] * n_rb for _ in range(n_i1)]
        for rb in range(n_rb):
            for i0 in range(0, n_i1, PEER_I1):
                group = range(i0, i0 + PEER_I1)
                gates = {ii: jnp.zeros((PEER_RB, PEER_LC), f32) for ii in group}
                for h in range(PEER_HEADS):
                    s2 = s2_ref[h * N_KEYS + rb * PEER_RB:h * N_KEYS + (rb + 1) * PEER_RB, lanes]
                    tau = tau_ref[h:h + 1, lanes]
                    for ii in group:
                        c = s2 + s1_ref[s1_rows[h][ii], lanes]
                        gates[ii] = gates[ii] + jnp.where(c >= tau, jnp.exp2(c), 0.0)
                for ii in group:
                    a = act[ii * N_KEYS + rb * PEER_RB:ii * N_KEYS + (rb + 1) * PEER_RB, lanes]
                    pieces[ii][rb] = (_gelu(a) * gates[ii]).astype(bf16)
        cols.append(jnp.concatenate([pc for row in pieces for pc in row], axis=0))
    acc_ref[...] += _dot(vt_ref[...], jnp.concatenate(cols, axis=1))

    @pl.when(j == pl.num_programs(1) - 1)
    def _():
        out_ref[...] = acc_ref[...].T + res_ref[...]


def _peer_dense(xt, ub, vt, layer, s1, s2, tau, res, *, tm, te):
    D, M = xt.shape
    E = ub.shape[1]
    once = dict(pipeline_mode=pl.Buffered(1))
    rspec = pl.BlockSpec((PEER_HEADS * N_KEYS, tm), lambda i, j: (0, i), **once)
    return pl.pallas_call(
        functools.partial(_peer_dense_kernel, tm=tm, te=te),
        grid=(M // tm, E // te),
        in_specs=[pl.BlockSpec((D, tm), lambda i, j: (0, i)),
                  pl.BlockSpec((None, te, D), lambda i, j: (layer, j, 0)),
                  pl.BlockSpec((None, D, te), lambda i, j: (layer, 0, j)),
                  rspec, rspec,
                  pl.BlockSpec((PEER_HEADS, tm), lambda i, j: (0, i)),
                  pl.BlockSpec((tm, D), lambda i, j: (i, 0), **once)],
        out_specs=pl.BlockSpec((tm, D), lambda i, j: (i, 0), **once),
        out_shape=jax.ShapeDtypeStruct((M, D), f32),
        scratch_shapes=[pltpu.VMEM((D, tm), f32)],
        compiler_params=_params(("parallel", "arbitrary")),
        name="peer_dense",
    )(xt, ub, vt, s1, s2, tau, res)


def _peer(x_all, norm_w, wq, keys, ub, vt, layer, *, tm_mm, tmr, tm, te):
    q, ht = _mm(x_all, wq, norm_w=norm_w, emit_h=bf16, transpose_h=True, tm=tm_mm, tn=wq.shape[1],
                name="peer_query")
    s1, s2, tau = _peer_route(q, keys, tmr=tmr)
    return _peer_dense(ht, ub, vt, layer, s1, s2, tau, x_all, tm=tm, te=te)


LANE = 128


def _block_diag_ones():
    head_of_lane = jnp.arange(D_A) // HD_A
    return (jnp.arange(LANE)[:, None] == head_of_lane[None, :]).astype(bf16)


def _pad_lanes(w):
    return jnp.pad(w, ((0, 0), (0, (-w.shape[1]) % LANE)))


def _pad_rows(w):
    return jnp.pad(w, ((0, (-w.shape[0]) % LANE), (0, 0)))


def kernel(x_prompt, x_sample, state_shift, state_wkv, state_pool, cache_kv_w128, cache_kv_w512, cache_kv_w2048,
           norm_mix, norm_ffn, norm_final, a_w_in, a_w_out, a_mu_rkv, a_mu_wag, a_w0, a_w1, a_w2, a_a0, a_a1,
           a_a2, a_g1, a_g2, a_k_k, a_k_a, a_r_k, a_gn_w, a_gn_b, b_w_pool, b_scale, c_w_in, c_w_out, p_w_q,
           p_sub_keys, p_u, p_v):
    T = x_prompt.shape[1]
    NS = x_sample.shape[0]
    TM = 768
    M = -(-(T + NS) // TM) * TM
    S_BLK = T // NS
    assert T % NS == 0 and T % ATTN_BLK == 0
    cb = lambda w: w.astype(bf16)
    row = lambda w: w.reshape(1, -1)

    x0 = jnp.concatenate([x_prompt.reshape(T, D_MODEL), x_sample.reshape(NS, D_MODEL),
                          jnp.zeros((M - T - NS, D_MODEL), f32)], axis=0)
    ub, vt = _peer_prep(p_u, p_v, te=512)
    bd = _block_diag_ones()

    w_in = cb(a_w_in[0])
    z0, h0 = _mm(x0, w_in, norm_w=norm_mix[0], emit_h=f32, tm=TM, tn=1024, name="even_in_proj")
    zs_prev = _mm(state_shift[0], w_in[:, :3 * D_A], tm=NS, tn=1024, name="even_in_proj_state")
    plist = [a_mu_wag[0], row(a_mu_rkv[0]), row(a_w0[0]), _pad_lanes(cb(a_w1[0])), _pad_rows(cb(a_w2[0])),
             row(a_a0[0]), _pad_lanes(cb(a_a1[0])), _pad_rows(cb(a_a2[0])), cb(a_g1[0]), cb(a_g2[0]),
             row(a_k_k[0]), row(a_k_a[0]), cb(b_w_pool[0]), row(b_scale[0]), bd]
    rp, lwp, kp, vp, kkp, ap, gp, obp = _even_mid_prompt(h0, z0, plist, T=T, tm=256)
    rs, lws, ks, vs, kks, as_, gs, obs = _even_mid_sample(
        h0, state_shift[0], z0, zs_prev, jnp.swapaxes(state_pool[0], 0, 1), plist, row0=S_BLK, n=NS)
    o_p, wkv_p = _wkv_prompt(rp, lwp, kp, vp, kkp, ap, T=T, tb=512, npair=8)
    o_s, wkv_s = _wkv_sample(state_wkv[0], rs, lws, ks, vs, kks, as_)
    post_c = (cb(a_w_out[0]), row(a_r_k[0]), row(a_gn_w[0]), row(a_gn_b[0]), bd)
    x1 = _even_post(x0, o_p, rp, kp, vp, gp, obp, *post_c, row0=0, rows=T, tm=256)
    x1 = _even_post(x1, o_s, rs, ks, vs, gs, obs, *post_c, row0=S_BLK, rows=NS, tm=NS)
    x1 = _peer(x1, norm_ffn[0], cb(p_w_q[0]), cb(p_sub_keys[0].reshape(2 * PEER_HEADS, N_KEYS, N_KEYS)),
               ub, vt, 0, tm_mm=TM, tmr=256, tm=TM, te=1024)

    z1 = _mm(x1, cb(c_w_in[0]), norm_w=norm_mix[1], tm=TM, tn=1024, name="odd_in_proj")
    att_p = _attn_prompt(z1, T=T)
    z1s = z1[T:T + NS]
    att_s = _attn_sample(z1s, (cache_kv_w128[0], cache_kv_w512[0], cache_kv_w2048[0]))
    att = jnp.concatenate([att_p, att_s, jnp.zeros((M - T - NS, D_C), f32)], axis=0)
    x2 = _mm(att, cb(c_w_out[0]), res=x1, tm=TM, tn=D_MODEL, name="odd_out_proj")
    x2 = _peer(x2, norm_ffn[1], cb(p_w_q[1]), cb(p_sub_keys[1].reshape(2 * PEER_HEADS, N_KEYS, N_KEYS)),
               ub, vt, 1, tm_mm=TM, tmr=256, tm=TM, te=1024)

    y_p = _rmsnorm(x2, norm_final, tm=512, row0=0, rows=T)
    y_s = _rmsnorm(x2, norm_final, tm=NS, row0=S_BLK, rows=NS)

    u_p = z0[T - POOL_BUF:T, 3 * D_A:]
    u_s = z0[T:T + NS, 3 * D_A:]
    pool_s = jnp.concatenate([state_pool[0][:, 1:], u_s[:, None, :]], axis=1)
    kv_p, kv_s = [], []
    for g, (win, _) in enumerate(ATTN_GROUPS):
        n = min(win, T)
        kv = z1[T - n:T + NS, g * 3 * D_C + D_C:g * 3 * D_C + 3 * D_C].reshape(n + NS, 2, H_C, HD_C)
        kv_p.append(kv[:n][None, None])
        kv_s.append(kv[n:][None, :, None])
    return (y_p[None], y_s[:, None, :],
            h0[T - 1][None, None], h0[T:T + NS][None],
            wkv_p[None, None], wkv_s[None],
            u_p[None, None], pool_s[None],
            kv_p[0], kv_s[0], kv_p[1], kv_s[1], kv_p[2], kv_s[2])
```

```python
import functools
import math

import jax
import jax.numpy as jnp
from jax import lax
from jax.experimental import pallas as pl
from jax.experimental.pallas import tpu as pltpu

f32 = jnp.float32
bf16 = jnp.bfloat16

D_MODEL = 2048
H_A, HD_A = 16, 64
D_A = H_A * HD_A
D_POOL = D_MODEL - D_A
POOL_WINDOWS = (2, 4, 8, 16)
POOL_GD = D_POOL // len(POOL_WINDOWS)
POOL_BUF = max(POOL_WINDOWS) - 1
GN_EPS = 64e-5
NORM_EPS = 1e-6
ATTN_GROUPS = ((128, 1), (512, 4), (2048, 16))
H_C, HD_C = 8, 128
D_C = H_C * HD_C
N_KEYS = 128
N_EXPERTS = N_KEYS * N_KEYS
PEER_HEADS = 8
PEER_TOPK = 16

VMEM_LIMIT = 56 * 1024 * 1024
WKV_CHUNK = 64
ATTN_BLK = 2048
ATTN_UNROLL = 8
NEG_INF = float("-inf")
LOG2E = math.log2(math.e)


def _params(sem, vmem=VMEM_LIMIT, flags=None):
    return pltpu.CompilerParams(dimension_semantics=sem, vmem_limit_bytes=vmem, flags=flags)


def _dot(a, b):
    return jnp.dot(a, b, preferred_element_type=f32)


def _dot_nt(a, b):
    return lax.dot_general(a, b, (((1,), (1,)), ((), ())), preferred_element_type=f32)


def _dot_tn(a, b):
    return lax.dot_general(a, b, (((0,), (0,)), ((), ())), preferred_element_type=f32)


def _split(x):
    hi = x.astype(bf16)
    lo = (x - hi.astype(f32)).astype(bf16)
    return hi, lo


def _dot3(a, b, dot=_dot):
    ah, al = _split(a)
    bh, bl = _split(b)
    return dot(ah, bh) + (dot(ah, bl) + dot(al, bh))


def _head_sum(x, bd):
    hi, lo = _split(x)
    sh, sl = _split(_dot_nt(hi, bd) + _dot_nt(lo, bd))
    return _dot(sh, bd) + _dot(sl, bd)


def _gelu(x):
    return 0.5 * x * (1.0 + lax.erf(x * 0.7071067811865476))


def _mm_kernel(*refs, has_norm, has_res, emit_h, transpose_h):
    it = iter(refs)
    x_ref, w_ref = next(it), next(it)
    g_ref = next(it) if has_norm else None
    res_ref = next(it) if has_res else None
    o_ref = next(it)
    h_ref = next(it) if emit_h else None
    xb_ref = next(it)

    @pl.when(pl.program_id(1) == 0)
    def _():
        x = x_ref[...]
        if has_norm:
            ms = jnp.mean(x * x, axis=-1, keepdims=True)
            x = (x * lax.rsqrt(ms + NORM_EPS)) * g_ref[...]
            if emit_h:
                h_ref[...] = (x.T if transpose_h else x).astype(h_ref.dtype)
        xb_ref[...] = x.astype(bf16)

    acc = _dot(xb_ref[...], w_ref[...])
    if has_res:
        acc = acc + res_ref[...]
    o_ref[...] = acc


def _mm(x, w, *, norm_w=None, res=None, emit_h=None, transpose_h=False, tm, tn, row0=0, rows=None, name="proj"):
    K = x.shape[1]
    N = w.shape[1]
    rows = x.shape[0] if rows is None else rows
    assert rows % tm == 0 and N % tn == 0
    w_mode = dict(pipeline_mode=pl.Buffered(1)) if tn == N else {}
    in_specs = [pl.BlockSpec((tm, K), lambda i, j: (i + row0, 0)),
                pl.BlockSpec((K, tn), lambda i, j: (0, j), **w_mode)]
    args = [x, w]
    if norm_w is not None:
        in_specs.append(pl.BlockSpec((1, K), lambda i, j: (0, 0)))
        args.append(norm_w.reshape(1, K))
    if res is not None:
        in_specs.append(pl.BlockSpec((tm, tn), lambda i, j: (i + row0, j)))
        args.append(res)
    out_shape = [jax.ShapeDtypeStruct((rows, N), f32)]
    out_specs = [pl.BlockSpec((tm, tn), lambda i, j: (i, j))]
    if emit_h is not None and transpose_h:
        out_shape.append(jax.ShapeDtypeStruct((K, rows), emit_h))
        out_specs.append(pl.BlockSpec((K, tm), lambda i, j: (0, i)))
    elif emit_h is not None:
        out_shape.append(jax.ShapeDtypeStruct((rows, K), emit_h))
        out_specs.append(pl.BlockSpec((tm, K), lambda i, j: (i, 0)))
    outs = pl.pallas_call(
        functools.partial(_mm_kernel, has_norm=norm_w is not None, has_res=res is not None,
                          emit_h=emit_h is not None, transpose_h=transpose_h),
        grid=(rows // tm, N // tn),
        in_specs=in_specs, out_specs=out_specs, out_shape=out_shape,
        scratch_shapes=[pltpu.VMEM((tm, K), bf16)],
        compiler_params=_params(("parallel", "arbitrary")),
        name=name,
    )(*args)
    return outs if emit_h is not None else outs[0]


def _rms_kernel(x_ref, g_ref, o_ref):
    x = x_ref[...]
    ms = jnp.mean(x * x, axis=-1, keepdims=True)
    o_ref[...] = (x * lax.rsqrt(ms + NORM_EPS)) * g_ref[...]


def _rmsnorm(x, g, *, tm, row0, rows):
    K = x.shape[1]
    return pl.pallas_call(
        _rms_kernel, grid=(rows // tm,),
        in_specs=[pl.BlockSpec((tm, K), lambda i: (i + row0, 0)), pl.BlockSpec((1, K), lambda i: (0, 0))],
        out_specs=pl.BlockSpec((tm, K), lambda i: (i, 0)),
        out_shape=jax.ShapeDtypeStruct((rows, K), f32),
        compiler_params=_params(("parallel",)),
        name="rmsnorm",
    )(x, g.reshape(1, K))


def _even_token_math(h, hs, zc, zs, P):
    dh = hs - h
    mu = P["mu_wag"]
    xw = (h + dh * mu[0:1]).astype(bf16)
    xa = (h + dh * mu[1:2]).astype(bf16)
    xg = (h + dh * mu[2:3]).astype(bf16)
    tw = jnp.tanh(_dot(xw, P["w1"])).astype(bf16)
    wl = P["w0"] + _dot(tw, P["w2"])
    w_log = -jax.nn.softplus(-wl) - 0.5
    lw = -jnp.exp(w_log)
    a = jax.nn.sigmoid(P["a0"] + _dot(_dot(xa, P["a1"]).astype(bf16), P["a2"]))
    g = _dot(jax.nn.sigmoid(_dot(xg, P["g1"])).astype(bf16), P["g2"])
    rkv = zc + (zs - zc) * P["mu_rkv"]
    r, k, v = rkv[:, :D_A], rkv[:, D_A:2 * D_A], rkv[:, 2 * D_A:]
    kk = k * P["k_k"]
    nrm = jnp.sqrt(_head_sum(kk * kk, P["bd"]))
    kkn = kk / jnp.maximum(nrm, 1e-12)
    k2 = k * (1.0 + (a - 1.0) * P["k_a"])
    return r, lw, k2, v, kkn, a, g


def _pool_project(pm, P):
    outs = []
    for gi in range(len(POOL_WINDOWS)):
        c = slice(gi * POOL_GD, (gi + 1) * POOL_GD)
        outs.append(_dot(pm[:, c].astype(bf16), P["w_pool"][gi]))
    return jnp.concatenate(outs, axis=-1) * P["pool_scale"]


_EVEN_PARAM_NAMES = ("mu_wag", "mu_rkv", "w0", "w1", "w2", "a0", "a1", "a2", "g1", "g2", "k_k", "k_a",
                     "w_pool", "pool_scale", "bd")


def _load_params(refs):
    return {n: r[...] for n, r in zip(_EVEN_PARAM_NAMES, refs)}


def _even_mid_prompt_kernel(h_ref, hp_ref, z_ref, zp_ref, *rest, tm):
    prefs, outs = rest[:len(_EVEN_PARAM_NAMES)], rest[len(_EVEN_PARAM_NAMES):]
    P = _load_params(prefs)
    first = pl.program_id(0) == 0
    h = h_ref[...]
    z = z_ref[...]
    zc, u = z[:, :3 * D_A], z[:, 3 * D_A:]
    hprev = jnp.where(first, 0.0, hp_ref[15:16, :])
    zprev = jnp.where(first, 0.0, zp_ref[15:16, :3 * D_A])
    row = lax.broadcasted_iota(jnp.int32, (tm, 1), 0)
    hs = jnp.where(row == 0, hprev, pltpu.roll(h, 1, axis=0))
    zs = jnp.where(row == 0, zprev, pltpu.roll(zc, 1, axis=0))
    prow = lax.broadcasted_iota(jnp.int32, (tm, POOL_GD), 0)
    r, lw, k2, v, kkn, a, g = _even_token_math(h, hs, zc, zs, P)

    uprev = jnp.where(first, 0.0, zp_ref[:, 3 * D_A:])
    pos = pl.program_id(0) * tm + prow
    means = []
    for gi, win in enumerate(POOL_WINDOWS):
        c = slice(gi * POOL_GD, (gi + 1) * POOL_GD)
        s = jnp.concatenate([uprev[:, c], u[:, c]], axis=0)
        sh = 1
        while sh < win:
            s = s + pltpu.roll(s, sh, axis=0)
            sh *= 2
        cnt = jnp.minimum(pos + 1, win).astype(f32)
        means.append(s[16:, :] / cnt)
    ob = _pool_project(jnp.concatenate(means, axis=-1) - u, P)
    for o_ref, val in zip(outs, (r, lw, k2, v, kkn, a, g, ob)):
        o_ref[...] = val


def _even_mid_sample_kernel(h_ref, hs_ref, z_ref, zs_ref, buf_ref, *rest):
    prefs, outs = rest[:len(_EVEN_PARAM_NAMES)], rest[len(_EVEN_PARAM_NAMES):]
    P = _load_params(prefs)
    h = h_ref[...]
    z = z_ref[...]
    zc, u = z[:, :3 * D_A], z[:, 3 * D_A:]
    r, lw, k2, v, kkn, a, g = _even_token_math(h, hs_ref[...], zc, zs_ref[...], P)
    means = []
    for gi, win in enumerate(POOL_WINDOWS):
        c = slice(gi * POOL_GD, (gi + 1) * POOL_GD)
        s = u[:, c]
        for j in range(POOL_BUF - (win - 1), POOL_BUF):
            s = s + buf_ref[j, :, c]
        means.append(s / float(win))
    ob = _pool_project(jnp.concatenate(means, axis=-1) - u, P)
    for o_ref, val in zip(outs, (r, lw, k2, v, kkn, a, g, ob)):
        o_ref[...] = val


def _const_spec(shape):
    nd = len(shape)
    return pl.BlockSpec(shape, lambda i, _nd=nd: (0,) * _nd, pipeline_mode=pl.Buffered(1))


def _even_mid_prompt(h_all, z_all, plist, *, T, tm):
    nb = tm // 16
    in_specs = [pl.BlockSpec((tm, D_MODEL), lambda i: (i, 0)),
                pl.BlockSpec((16, D_MODEL), lambda i: (jnp.maximum(i * nb - 1, 0), 0)),
                pl.BlockSpec((tm, 4 * D_A), lambda i: (i, 0)),
                pl.BlockSpec((16, 4 * D_A), lambda i: (jnp.maximum(i * nb - 1, 0), 0))]
    in_specs += [_const_spec(p.shape) for p in plist]
    return pl.pallas_call(
        functools.partial(_even_mid_prompt_kernel, tm=tm),
        grid=(T // tm,), in_specs=in_specs,
        out_specs=[pl.BlockSpec((tm, D_A), lambda i: (i, 0))] * 8,
        out_shape=[jax.ShapeDtypeStruct((T, D_A), f32)] * 8,
        compiler_params=_params(("parallel",)),
        name="even_mid_prompt",
    )(h_all, h_all, z_all, z_all, *plist)


def _even_mid_sample(h_all, hs, z_all, zs, buf_t, plist, *, row0, n):
    in_specs = [pl.BlockSpec((n, D_MODEL), lambda i: (row0, 0)),
                pl.BlockSpec((n, D_MODEL), lambda i: (0, 0)),
                pl.BlockSpec((n, 4 * D_A), lambda i: (row0, 0)),
                pl.BlockSpec((n, 3 * D_A), lambda i: (0, 0)),
                pl.BlockSpec((POOL_BUF, n, D_POOL), lambda i: (0, 0, 0))]
    in_specs += [_const_spec(p.shape) for p in plist]
    return pl.pallas_call(
        _even_mid_sample_kernel, grid=(1,), in_specs=in_specs,
        out_specs=[pl.BlockSpec((n, D_A), lambda i: (0, 0))] * 8,
        out_shape=[jax.ShapeDtypeStruct((n, D_A), f32)] * 8,
        compiler_params=_params(("arbitrary",)),
        name="even_mid_sample",
    )(h_all, hs, z_all, zs, buf_t, *plist)


def _wkv_prompt_kernel(r_ref, lw_ref, k_ref, v_ref, kk_ref, a_ref, o_ref, s_out_ref, S_ref, *, tb, npair):
    C = WKV_CHUNK
    t = pl.program_id(1)

    @pl.when(t == 0)
    def _():
        S_ref[...] = jnp.zeros_like(S_ref)

    ri = lax.broadcasted_iota(jnp.int32, (C, C), 0)
    ci = lax.broadcasted_iota(jnp.int32, (C, C), 1)
    strict = ri > ci
    incl = ri >= ci
    eye = (ri == ci).astype(f32)
    rows = lax.broadcasted_iota(jnp.int32, (C, 2 * HD_A), 0)

    nh = 2 * npair
    heads = range(nh)

    def prep(sl, lanes):
        r, lw, k, v, kk, a = (x[sl, lanes] for x in (r_ref, lw_ref, k_ref, v_ref, kk_ref, a_ref))
        cum = lw
        sh = 1
        while sh < C:
            cum = cum + jnp.where(rows >= sh, pltpu.roll(cum, sh, axis=0), 0.0)
            sh *= 2
        cum_c = cum[C - 1:C, :]
        b = kk * a
        e_neg = jnp.exp(-cum)
        e_rem = jnp.exp(cum_c - cum)
        return dict(at=-kk * jnp.exp(cum - lw), rt=r * jnp.exp(cum), bt=b * e_neg, kt=k * e_neg,
                    bh=b * e_rem, kh=k * e_rem, w_c=jnp.exp(cum_c), v=v)

    def chunk(c, carry):
        off = pl.multiple_of(c * C, C)
        sl = pl.ds(off, C)
        pairs = [prep(sl, slice(p * 2 * HD_A, (p + 1) * 2 * HD_A)) for p in range(npair)]
        S0 = [S_ref[h] for h in heads]

        def head(name, h):
            return pairs[h // 2][name][:, (h % 2) * HD_A:(h % 2 + 1) * HD_A]

        ar = [jnp.concatenate([head("at", h), head("rt", h)], axis=0) for h in heads]
        bk = [jnp.concatenate([head("bt", h), head("kt", h)], axis=0) for h in heads]
        vh = [head("v", h) for h in heads]
        m4 = [_dot3(ar[h], bk[h], _dot_nt) for h in heads]
        sar = [_dot3(ar[h], S0[h], _dot_nt) for h in heads]
        a_ab = [jnp.where(strict, m4[h][:C, :C], 0.0) for h in heads]
        a_ak = [jnp.where(strict, m4[h][:C, C:], 0.0) for h in heads]
        a_rb = [jnp.where(incl, m4[h][C:, :C], 0.0) for h in heads]
        a_rk = [jnp.where(incl, m4[h][C:, C:], 0.0) for h in heads]
        av = [_dot3(jnp.concatenate([a_ak[h], a_rk[h]], axis=0), vh[h]) for h in heads]
        tinv = [eye + a_ab[h] for h in heads]
        pw = [_dot3(a_ab[h], a_ab[h]) for h in heads]
        n_sq = 2
        while n_sq * 2 < C:
            x = [_dot3(pw[h], jnp.concatenate([tinv[h], pw[h]], axis=1)) for h in heads]
            tinv = [tinv[h] + x[h][:, :C] for h in heads]
            pw = [x[h][:, C:] for h in heads]
            n_sq *= 2
        tinv = [tinv[h] + _dot3(pw[h], tinv[h]) for h in heads]
        u = [_dot3(tinv[h], sar[h][:C] + av[h][:C]) for h in heads]
        o = [sar[h][C:] + av[h][C:] + _dot3(a_rb[h], u[h]) for h in heads]
        s_new = []
        for h in heads:
            uv = jnp.concatenate([u[h], vh[h]], axis=0)
            bkh = jnp.concatenate([head("bh", h), head("kh", h)], axis=0)
            s_new.append(S0[h] * head("w_c", h) + _dot3(uv, bkh, _dot_tn))
        for h in heads:
            S_ref[h] = s_new[h]
        for p in range(npair):
            o_ref[sl, p * 2 * HD_A:(p + 1) * 2 * HD_A] = jnp.concatenate([o[2 * p], o[2 * p + 1]], axis=1)
        return carry

    lax.fori_loop(0, tb // C, chunk, 0)

    @pl.when(t == pl.num_programs(1) - 1)
    def _():
        s_out_ref[...] = S_ref[...]


def _wkv_prompt(r, lw, k2, v, kkn, a, *, T, tb, npair):
    spec = pl.BlockSpec((tb, 2 * HD_A * npair), lambda j, t: (t, j))
    nh = 2 * npair
    return pl.pallas_call(
        functools.partial(_wkv_prompt_kernel, tb=tb, npair=npair),
        grid=(H_A // nh, T // tb),
        in_specs=[spec] * 6,
        out_specs=[spec, pl.BlockSpec((nh, HD_A, HD_A), lambda j, t: (j, 0, 0))],
        out_shape=[jax.ShapeDtypeStruct((T, D_A), f32), jax.ShapeDtypeStruct((H_A, HD_A, HD_A), f32)],
        scratch_shapes=[pltpu.VMEM((nh, HD_A, HD_A), f32)],
        compiler_params=_params(("parallel", "arbitrary")),
        name="wkv_prompt",
    )(r, lw, k2, v, kkn, a)


def _wkv_sample_kernel(s_ref, r_ref, lw_ref, k_ref, kk_ref, a_ref, v_ref, o_ref, s_out_ref):
    S = s_ref[0]
    kk = kk_ref[0]
    sa = jnp.sum(S * (-kk), axis=-1, keepdims=True)
    S2 = S * jnp.exp(lw_ref[0]) + sa * (kk * a_ref[0]) + v_ref[0] * k_ref[0]
    s_out_ref[0] = S2
    o_ref[0] = jnp.sum(S2 * r_ref[0], axis=-1, keepdims=True)


def _wkv_sample(S0, r, lw, k2, v, kkn, a):
    n = S0.shape[0]
    row = lambda x: x.reshape(n, H_A, 1, HD_A)
    rspec = pl.BlockSpec((1, H_A, 1, HD_A), lambda b: (b, 0, 0, 0))
    cspec = pl.BlockSpec((1, H_A, HD_A, 1), lambda b: (b, 0, 0, 0))
    sspec = pl.BlockSpec((1, H_A, HD_A, HD_A), lambda b: (b, 0, 0, 0))
    o, S = pl.pallas_call(
        _wkv_sample_kernel, grid=(n,),
        in_specs=[sspec, rspec, rspec, rspec, rspec, rspec, cspec],
        out_specs=[cspec, sspec],
        out_shape=[jax.ShapeDtypeStruct((n, H_A, HD_A, 1), f32), jax.ShapeDtypeStruct(S0.shape, f32)],
        compiler_params=_params(("parallel",)),
        name="wkv_sample",
    )(S0, row(r), row(lw), row(k2), row(kkn), row(a), v.reshape(n, H_A, HD_A, 1))
    return o.reshape(n, D_A), S


def _even_post_kernel(x_ref, o_ref, r_ref, k_ref, v_ref, g_ref, ob_ref, wo_ref, rk_ref, gnw_ref, gnb_ref, bd_ref,
                      out_ref):
    bd = bd_ref[...]
    o = o_ref[...]
    inv = 1.0 / HD_A
    mu = _head_sum(o, bd) * inv
    d = o - mu
    var = _head_sum(d * d, bd) * inv
    on = d * lax.rsqrt(var + GN_EPS) * gnw_ref[...] + gnb_ref[...]
    v = v_ref[...]
    bonus = _head_sum(r_ref[...] * k_ref[...] * rk_ref[...], bd) * v
    oa = ((on + bonus) * g_ref[...]).astype(bf16)
    y = _dot(oa, wo_ref[:D_A, :]) + _dot(ob_ref[...].astype(bf16), wo_ref[D_A:, :])
    out_ref[...] = x_ref[...] + y


def _even_post(x_all, o, r, k2, v, g, ob, wo, rk, gnw, gnb, bd, *, row0, rows, tm):
    a_spec = pl.BlockSpec((tm, D_A), lambda i: (i, 0))
    x_spec = pl.BlockSpec((tm, D_MODEL), lambda i: (i + row0, 0))
    consts = [wo, rk, gnw, gnb, bd]
    return pl.pallas_call(
        _even_post_kernel, grid=(rows // tm,),
        in_specs=[x_spec] + [a_spec] * 6 + [_const_spec(c.shape) for c in consts],
        out_specs=x_spec,
        out_shape=jax.ShapeDtypeStruct(x_all.shape, f32),
        input_output_aliases={0: 0},
        compiler_params=_params(("parallel",)),
        name="even_post",
    )(x_all, o, r, k2, v, g, ob, *consts)


def _attn_prompt_kernel(*refs):
    in_refs, out_ref, og_ref, lg_ref = refs[:15], refs[15], refs[16], refs[17]
    n = pl.program_id(1)
    has_prev = n > 0
    scale = HD_C ** -0.5
    Q = 128
    ri = lax.broadcasted_iota(jnp.int32, (Q, Q), 0)
    ci = lax.broadcasted_iota(jnp.int32, (Q, Q), 1)
    mask_prev0 = ci >= ri
    mask_cur = ci <= ri

    for g, (_, dil) in enumerate(ATTN_GROUPS):
        q_ref, kp_ref, k_ref, vp_ref, v_ref = in_refs[5 * g:5 * g + 5]
        nblk = ATTN_BLK // (Q * dil)

        def units(it, carry, q_ref=q_ref, kp_ref=kp_ref, k_ref=k_ref, vp_ref=vp_ref, v_ref=v_ref,
                  dil=dil, nblk=nblk, g=g):
            us = [it * ATTN_UNROLL + k for k in range(ATTN_UNROLL)]
            rows, q, kc, vc, kp, vp, valid_prev = [], [], [], [], [], [], []
            for u in us:
                c = u // nblk
                m = u % nblk
                r = pl.ds(m * (Q * dil) + c, Q, stride=dil)
                in_blk = m > 0
                r_a = pl.ds(jnp.maximum(m - 1, 0) * (Q * dil) + c, Q, stride=dil)
                r_b = pl.ds((nblk - 1) * (Q * dil) + c, Q, stride=dil)
                rows.append(r)
                q.append(q_ref[r, :].astype(bf16))
                kc.append(k_ref[r, :].astype(bf16))
                vc.append(v_ref[r, :].astype(bf16))
                kp.append(jnp.where(in_blk, k_ref[r_a, :], kp_ref[r_b, :]).astype(bf16))
                vp.append(jnp.where(in_blk, v_ref[r_a, :], vp_ref[r_b, :]).astype(bf16))
                valid_prev.append(jnp.logical_or(in_blk, has_prev))
            n = range(ATTN_UNROLL)
            s_p = [_dot_nt(q[i], kp[i]) * scale for i in n]
            s_c = [_dot_nt(q[i], kc[i]) * scale for i in n]
            s_p = [jnp.where(jnp.logical_and(mask_prev0, valid_prev[i]), s_p[i], NEG_INF) for i in n]
            s_c = [jnp.where(mask_cur, s_c[i], NEG_INF) for i in n]
            mx = [jnp.maximum(jnp.max(s_p[i], axis=1, keepdims=True), jnp.max(s_c[i], axis=1, keepdims=True))
                  for i in n]
            p_p = [jnp.exp(s_p[i] - mx[i]) for i in n]
            p_c = [jnp.exp(s_c[i] - mx[i]) for i in n]
            den = [jnp.sum(p_p[i], axis=1, keepdims=True) + jnp.sum(p_c[i], axis=1, keepdims=True) for i in n]
            o = [(_dot(p_p[i].astype(bf16), vp[i]) + _dot(p_c[i].astype(bf16), vc[i])) / den[i] for i in n]
            lse = [mx[i] + jnp.log(den[i]) for i in n]
            for i in n:
                og_ref[g, rows[i], :] = o[i]
                lg_ref[g, rows[i], :] = jnp.broadcast_to(lse[i], (Q, HD_C))
            return carry

        lax.fori_loop(0, dil * nblk // ATTN_UNROLL, units, 0)

    l0, l1, l2 = lg_ref[0], lg_ref[1], lg_ref[2]
    mx = jnp.maximum(jnp.maximum(l0, l1), l2)
    e0, e1, e2 = jnp.exp(l0 - mx), jnp.exp(l1 - mx), jnp.exp(l2 - mx)
    out_ref[...] = (e0 * og_ref[0] + e1 * og_ref[1] + e2 * og_ref[2]) / (e0 + e1 + e2)


def _attn_prompt(z, *, T):
    nb = T // ATTN_BLK
    in_specs = []
    for g in range(len(ATTN_GROUPS)):
        def col(j, g=g):
            return lambda h, n: (n, g * 3 * H_C + j * H_C + h)

        def col_prev(j, g=g):
            return lambda h, n: (jnp.maximum(n - 1, 0), g * 3 * H_C + j * H_C + h)
        blk = (ATTN_BLK, HD_C)
        in_specs += [pl.BlockSpec(blk, col(0)), pl.BlockSpec(blk, col_prev(1)), pl.BlockSpec(blk, col(1)),
                     pl.BlockSpec(blk, col_prev(2)), pl.BlockSpec(blk, col(2))]
    return pl.pallas_call(
        _attn_prompt_kernel, grid=(H_C, nb),
        in_specs=in_specs,
        out_specs=pl.BlockSpec((ATTN_BLK, HD_C), lambda h, n: (n, h)),
        out_shape=jax.ShapeDtypeStruct((T, D_C), f32),
        scratch_shapes=[pltpu.VMEM((3, ATTN_BLK, HD_C), f32), pltpu.VMEM((3, ATTN_BLK, HD_C), f32)],
        compiler_params=_params(("parallel", "arbitrary")),
        name="attn_prompt",
    )(*([z] * 15))


ATTN_SAMPLE_NB = 4


def _attn_sample_kernel(z_ref, c0_ref, c1_ref, c2_ref, out_ref):
    scale = HD_C ** -0.5
    crefs = (c0_ref, c1_ref, c2_ref)
    for b in range(ATTN_SAMPLE_NB):
        for h in range(H_C):
            outs, lses = [], []
            for g in range(len(ATTN_GROUPS)):
                base = g * 3 * D_C + h * HD_C
                q = z_ref[b, :, base:base + HD_C]
                kn = z_ref[b, :, base + D_C:base + D_C + HD_C]
                vn = z_ref[b, :, base + 2 * D_C:base + 2 * D_C + HD_C]
                K = crefs[g][b, :, 0, 0, h, :]
                V = crefs[g][b, :, 0, 1, h, :]
                s_c = jnp.sum(K * q, axis=1, keepdims=True) * scale
                s_n = jnp.sum(kn * q, axis=1, keepdims=True) * scale
                mx = jnp.maximum(jnp.max(s_c, axis=0, keepdims=True), s_n)
                p_c = jnp.exp(s_c - mx)
                p_n = jnp.exp(s_n - mx)
                den = jnp.sum(p_c, axis=0, keepdims=True) + p_n
                outs.append((jnp.sum(p_c * V, axis=0, keepdims=True) + p_n * vn) / den)
                lses.append(mx + jnp.log(den))
            mx = jnp.maximum(jnp.maximum(lses[0], lses[1]), lses[2])
            es = [jnp.exp(l - mx) for l in lses]
            out_ref[b, :, h * HD_C:(h + 1) * HD_C] = (es[0] * outs[0] + es[1] * outs[1] + es[2] * outs[2]) / (
                es[0] + es[1] + es[2])


def _attn_sample(z_s, caches):
    n = z_s.shape[0]
    nb = ATTN_SAMPLE_NB
    assert n % nb == 0
    in_specs = [pl.BlockSpec((nb, 1, 9 * D_C), lambda b: (b, 0, 0))]
    args = [z_s.reshape(n, 1, 9 * D_C)]
    for (win, dil), c in zip(ATTN_GROUPS, caches):
        assert c.shape[1] == win
        args.append(c.reshape(n, win // dil, dil, 2, H_C, HD_C))
        in_specs.append(pl.BlockSpec((nb, win // dil, 1, 2, H_C, HD_C), lambda b: (b, 0, 0, 0, 0, 0)))
    out = pl.pallas_call(
        _attn_sample_kernel, grid=(n // nb,), in_specs=in_specs,
        out_specs=pl.BlockSpec((nb, 1, D_C), lambda b: (b, 0, 0)),
        out_shape=jax.ShapeDtypeStruct((n, 1, D_C), f32),
        compiler_params=_params(("parallel",)),
        name="attn_sample",
    )(*args)
    return out.reshape(n, D_C)


def _top16_rows(s):
    rows = lax.broadcasted_iota(jnp.int32, s.shape, 0)
    work = s
    vals = []
    for _ in range(PEER_TOPK):
        m = jnp.max(work, axis=0, keepdims=True)
        idx = jnp.min(jnp.where(work == m, rows, s.shape[0]), axis=0, keepdims=True)
        work = jnp.where(rows == idx, NEG_INF, work)
        vals.append(m)
    return vals, jnp.logical_and(work == NEG_INF, s != NEG_INF)


def _oddeven_mergesort_pairs(n):
    pairs = []

    def merge(lo, hi, r):
        step = r * 2
        if step < hi - lo:
            merge(lo, hi, step)
            merge(lo + r, hi, step)
            pairs.extend((i, i + r) for i in range(lo + r, hi - r, step))
        else:
            pairs.append((lo, lo + r))

    def sort(lo, hi):
        if hi - lo >= 1:
            mid = lo + (hi - lo) // 2
            sort(lo, mid)
            sort(mid + 1, hi)
            merge(lo, hi, 1)

    sort(0, n - 1)
    return pairs


SUBLANES = 8


def _top16_sorted(s):
    assert s.shape[0] == PEER_TOPK * SUBLANES
    t = [s[SUBLANES * k:SUBLANES * (k + 1), :] for k in range(PEER_TOPK)]
    for i, j in _oddeven_mergesort_pairs(PEER_TOPK):
        t[i], t[j] = jnp.maximum(t[i], t[j]), jnp.minimum(t[i], t[j])
    shift = 1
    while shift < SUBLANES:
        other = [pltpu.roll(x, shift, axis=0) for x in t]
        t = [jnp.maximum(t[i], other[PEER_TOPK - 1 - i]) for i in range(PEER_TOPK)]
        d = PEER_TOPK // 2
        while d >= 1:
            for i in range(PEER_TOPK):
                if i & d == 0:
                    t[i], t[i + d] = jnp.maximum(t[i], t[i + d]), jnp.minimum(t[i], t[i + d])
            d //= 2
        shift *= 2
    thr = t[PEER_TOPK - 1][0:1, :]
    member = s >= thr
    count = jnp.sum(member.astype(f32), axis=0, keepdims=True)
    return [x[0:1, :] for x in t], member, count == float(PEER_TOPK)


def _top16_rows_distinct(ss):
    work = list(ss)
    vals = [[] for _ in ss]
    for _ in range(PEER_TOPK):
        for i in range(len(ss)):
            m = jnp.max(work[i], axis=0, keepdims=True)
            work[i] = jnp.where(work[i] == m, NEG_INF, work[i])
            vals[i].append(m)
    out = []
    for s, w, v in zip(ss, work, vals):
        member = jnp.logical_and(w == NEG_INF, s != NEG_INF)
        count = jnp.sum(member.astype(f32), axis=0, keepdims=True)
        out.append((v, member, count == float(PEER_TOPK)))
    return out


def _peer_route_kernel(q_ref, keys_ref, s1_ref, s2_ref, tau_ref):
    n = q_ref.shape[0]

    def route_head(h, exact):
        if exact:
            top16 = lambda ss: [_top16_rows(s) + (None,) for s in ss]
            top16_scores = top16
        else:
            top16 = _top16_rows_distinct
            top16_scores = lambda ss: [_top16_sorted(s) for s in ss]
        scores = []
        for p in range(2):
            hp = 2 * h + p
            qs = q_ref[:, hp * N_KEYS:(hp + 1) * N_KEYS].astype(bf16)
            scores.append(_dot_nt(keys_ref[hp], qs) * LOG2E)
        sm, vals, oks = [], [], []
        for s, (v, member, ok) in zip(scores, top16_scores(scores)):
            sm.append(jnp.where(member, s, NEG_INF))
            vals.append(v)
            oks.append(ok)

        def pair_sums(z1, z2):
            pieces = []
            for r1 in range(PEER_TOPK):
                cnt = PEER_TOPK // (r1 + 1)
                pieces.append(z1[r1] + jnp.concatenate(z2[:cnt], axis=0))
            npad = (-sum(p.shape[0] for p in pieces)) % 8
            pieces.append(jnp.full((npad, n), NEG_INF, f32))
            return jnp.concatenate(pieces, axis=0)

        z1 = [v - vals[0][0] for v in vals[0]]
        z2 = [v - vals[1][0] for v in vals[1]]
        (top, chosen, ok), = top16([pair_sums(z1, z2)])
        oks.append(ok)
        log_z = jnp.log2(sum(jnp.exp2(t) for t in top))
        z1 = [z - log_z for z in z1]
        sums = pair_sums(z1, z2)
        if exact:
            (top, _, _), = top16([sums])
            tau = top[-1]
        else:
            tau = jnp.min(jnp.where(chosen, sums, jnp.inf), axis=0, keepdims=True)
            count = jnp.sum((sums >= tau).astype(f32), axis=0, keepdims=True)
            oks.append(count == float(PEER_TOPK))
        rows = slice(h * N_KEYS, (h + 1) * N_KEYS)
        s1_ref[rows, :] = (sm[0] - vals[0][0]) - log_z
        s2_ref[rows, :] = sm[1] - vals[1][0]
        tau_ref[h:h + 1, :] = tau
        if exact:
            return None
        all_ok = functools.reduce(jnp.logical_and, oks)
        return jnp.min(all_ok.astype(f32)) > 0.5

    tie_free = [route_head(h, exact=False) for h in range(PEER_HEADS)]
    for h in range(PEER_HEADS):
        pl.when(jnp.logical_not(tie_free[h]))(functools.partial(route_head, h, True))


def _peer_route(q, keys, *, tmr):
    M = q.shape[0]
    big = jax.ShapeDtypeStruct((PEER_HEADS * N_KEYS, M), f32)
    bspec = pl.BlockSpec((PEER_HEADS * N_KEYS, tmr), lambda i: (0, i))
    return pl.pallas_call(
        _peer_route_kernel, grid=(M // tmr,),
        in_specs=[pl.BlockSpec((tmr, 2 * PEER_HEADS * N_KEYS), lambda i: (i, 0)),
                  _const_spec(keys.shape)],
        out_specs=[bspec] * 2 + [pl.BlockSpec((PEER_HEADS, tmr), lambda i: (0, i))],
        out_shape=[big] * 2 + [jax.ShapeDtypeStruct((PEER_HEADS, M), f32)],
        compiler_params=_params(("parallel",)),
        name="peer_route",
    )(q, keys)


def _peer_prep_kernel(u_ref, v_ref, ub_ref, vt_ref):
    ub_ref[...] = u_ref[...].astype(bf16)
    vt_ref[...] = v_ref[...].T.astype(bf16)


def _peer_prep(u_tab, v_tab, *, te):
    L, E, D = u_tab.shape
    return pl.pallas_call(
        _peer_prep_kernel, grid=(L, E // te),
        in_specs=[pl.BlockSpec((None, te, D), lambda l, j: (l, j, 0))] * 2,
        out_specs=[pl.BlockSpec((None, te, D), lambda l, j: (l, j, 0)),
                   pl.BlockSpec((None, D, te), lambda l, j: (l, 0, j))],
        out_shape=[jax.ShapeDtypeStruct((L, E, D), bf16), jax.ShapeDtypeStruct((L, D, E), bf16)],
        compiler_params=_params(("parallel", "parallel")),
        name="peer_prep",
    )(u_tab, v_tab)


PEER_RB = 32
PEER_LC = 256
PEER_I1 = 4


def _peer_dense_kernel(xt_ref, u_ref, vt_ref, s1_ref, s2_ref, tau_ref, res_ref, out_ref, acc_ref, *, tm, te):
    j = pl.program_id(1)
    n_i1 = te // N_KEYS
    n_rb = N_KEYS // PEER_RB

    @pl.when(j == 0)
    def _():
        acc_ref[...] = jnp.zeros_like(acc_ref)

    chunks = [slice(c * PEER_LC, (c + 1) * PEER_LC) for c in range(tm // PEER_LC)]
    assert n_i1 == SUBLANES
    s1_rows = [pl.ds(pl.multiple_of(h * N_KEYS + j * n_i1, n_i1), n_i1) for h in range(PEER_HEADS)]
    act = _dot(u_ref[...], xt_ref[...])
    cols = []
    for lanes in chunks:
        s1 = [s1_ref[s1_rows[h], lanes] for h in range(PEER_HEADS)]
        pieces = [[None] * n_rb for _ in range(n_i1)]
        for rb in range(n_rb):
            for i0 in range(0, n_i1, PEER_I1):
                group = range(i0, i0 + PEER_I1)
                gates = {ii: jnp.zeros((PEER_RB, PEER_LC), f32) for ii in group}
                for h in range(PEER_HEADS):
                    s2 = s2_ref[h * N_KEYS + rb * PEER_RB:h * N_KEYS + (rb + 1) * PEER_RB, lanes]
                    tau = tau_ref[h:h + 1, lanes]
                    for ii in group:
                        c = s2 + s1[h][ii:ii + 1, :]
                        gates[ii] = gates[ii] + jnp.where(c >= tau, jnp.exp2(c), 0.0)
                for ii in group:
                    a = act[ii * N_KEYS + rb * PEER_RB:ii * N_KEYS + (rb + 1) * PEER_RB, lanes]
                    pieces[ii][rb] = (_gelu(a) * gates[ii]).astype(bf16)
        cols.append(jnp.concatenate([pc for row in pieces for pc in row], axis=0))
    acc_ref[...] += _dot(vt_ref[...], jnp.concatenate(cols, axis=1))

    @pl.when(j == pl.num_programs(1) - 1)
    def _():
        out_ref[...] = acc_ref[...].T + res_ref[...]


def _peer_dense(xt, ub, vt, layer, s1, s2, tau, res, *, tm, te):
    D, M = xt.shape
    E = ub.shape[1]
    once = dict(pipeline_mode=pl.Buffered(1))
    rspec = pl.BlockSpec((PEER_HEADS * N_KEYS, tm), lambda i, j: (0, i), **once)
    return pl.pallas_call(
        functools.partial(_peer_dense_kernel, tm=tm, te=te),
        grid=(M // tm, E // te),
        in_specs=[pl.BlockSpec((D, tm), lambda i, j: (0, i)),
                  pl.BlockSpec((None, te, D), lambda i, j: (layer, j, 0)),
                  pl.BlockSpec((None, D, te), lambda i, j: (layer, 0, j)),
                  rspec, rspec,
                  pl.BlockSpec((PEER_HEADS, tm), lambda i, j: (0, i)),
                  pl.BlockSpec((tm, D), lambda i, j: (i, 0), **once)],
        out_specs=pl.BlockSpec((tm, D), lambda i, j: (i, 0), **once),
        out_shape=jax.ShapeDtypeStruct((M, D), f32),
        scratch_shapes=[pltpu.VMEM((D, tm), f32)],
        compiler_params=_params(("parallel", "arbitrary")),
        name="peer_dense",
    )(xt, ub, vt, s1, s2, tau, res)


def _peer(x_all, norm_w, wq, keys, ub, vt, layer, *, tm_mm, tmr, tm, te):
    q, ht = _mm(x_all, wq, norm_w=norm_w, emit_h=bf16, transpose_h=True, tm=tm_mm, tn=wq.shape[1],
                name="peer_query")
    s1, s2, tau = _peer_route(q, keys, tmr=tmr)
    return _peer_dense(ht, ub, vt, layer, s1, s2, tau, x_all, tm=tm, te=te)


LANE = 128


def _block_diag_ones():
    head_of_lane = jnp.arange(D_A) // HD_A
    return (jnp.arange(LANE)[:, None] == head_of_lane[None, :]).astype(bf16)


def _pad_lanes(w):
    return jnp.pad(w, ((0, 0), (0, (-w.shape[1]) % LANE)))


def _pad_rows(w):
    return jnp.pad(w, ((0, (-w.shape[0]) % LANE), (0, 0)))


def kernel(x_prompt, x_sample, state_shift, state_wkv, state_pool, cache_kv_w128, cache_kv_w512, cache_kv_w2048,
           norm_mix, norm_ffn, norm_final, a_w_in, a_w_out, a_mu_rkv, a_mu_wag, a_w0, a_w1, a_w2, a_a0, a_a1,
           a_a2, a_g1, a_g2, a_k_k, a_k_a, a_r_k, a_gn_w, a_gn_b, b_w_pool, b_scale, c_w_in, c_w_out, p_w_q,
           p_sub_keys, p_u, p_v):
    T = x_prompt.shape[1]
    NS = x_sample.shape[0]
    TM = 768
    M = -(-(T + NS) // TM) * TM
    S_BLK = T // NS
    assert T % NS == 0 and T % ATTN_BLK == 0
    cb = lambda w: w.astype(bf16)
    row = lambda w: w.reshape(1, -1)

    x0 = jnp.concatenate([x_prompt.reshape(T, D_MODEL), x_sample.reshape(NS, D_MODEL),
                          jnp.zeros((M - T - NS, D_MODEL), f32)], axis=0)
    ub, vt = _peer_prep(p_u, p_v, te=512)
    bd = _block_diag_ones()

    w_in = cb(a_w_in[0])
    z0, h0 = _mm(x0, w_in, norm_w=norm_mix[0], emit_h=f32, tm=TM, tn=1024, name="even_in_proj")
    zs_prev = _mm(state_shift[0], w_in[:, :3 * D_A], tm=NS, tn=1024, name="even_in_proj_state")
    plist = [a_mu_wag[0], row(a_mu_rkv[0]), row(a_w0[0]), _pad_lanes(cb(a_w1[0])), _pad_rows(cb(a_w2[0])),
             row(a_a0[0]), _pad_lanes(cb(a_a1[0])), _pad_rows(cb(a_a2[0])), cb(a_g1[0]), cb(a_g2[0]),
             row(a_k_k[0]), row(a_k_a[0]), cb(b_w_pool[0]), row(b_scale[0]), bd]
    rp, lwp, kp, vp, kkp, ap, gp, obp = _even_mid_prompt(h0, z0, plist, T=T, tm=256)
    rs, lws, ks, vs, kks, as_, gs, obs = _even_mid_sample(
        h0, state_shift[0], z0, zs_prev, jnp.swapaxes(state_pool[0], 0, 1), plist, row0=S_BLK, n=NS)
    o_p, wkv_p = _wkv_prompt(rp, lwp, kp, vp, kkp, ap, T=T, tb=512, npair=8)
    o_s, wkv_s = _wkv_sample(state_wkv[0], rs, lws, ks, vs, kks, as_)
    post_c = (cb(a_w_out[0]), row(a_r_k[0]), row(a_gn_w[0]), row(a_gn_b[0]), bd)
    x1 = _even_post(x0, o_p, rp, kp, vp, gp, obp, *post_c, row0=0, rows=T, tm=256)
    x1 = _even_post(x1, o_s, rs, ks, vs, gs, obs, *post_c, row0=S_BLK, rows=NS, tm=NS)
    x1 = _peer(x1, norm_ffn[0], cb(p_w_q[0]), cb(p_sub_keys[0].reshape(2 * PEER_HEADS, N_KEYS, N_KEYS)),
               ub, vt, 0, tm_mm=TM, tmr=256, tm=TM, te=1024)

    z1 = _mm(x1, cb(c_w_in[0]), norm_w=norm_mix[1], tm=TM, tn=1024, name="odd_in_proj")
    att_p = _attn_prompt(z1, T=T)
    z1s = z1[T:T + NS]
    att_s = _attn_sample(z1s, (cache_kv_w128[0], cache_kv_w512[0], cache_kv_w2048[0]))
    att = jnp.concatenate([att_p, att_s, jnp.zeros((M - T - NS, D_C), f32)], axis=0)
    x2 = _mm(att, cb(c_w_out[0]), res=x1, tm=TM, tn=D_MODEL, name="odd_out_proj")
    x2 = _peer(x2, norm_ffn[1], cb(p_w_q[1]), cb(p_sub_keys[1].reshape(2 * PEER_HEADS, N_KEYS, N_KEYS)),
               ub, vt, 1, tm_mm=TM, tmr=256, tm=TM, te=1024)

    y_p = _rmsnorm(x2, norm_final, tm=512, row0=0, rows=T)
    y_s = _rmsnorm(x2, norm_final, tm=NS, row0=S_BLK, rows=NS)

    u_p = z0[T - POOL_BUF:T, 3 * D_A:]
    u_s = z0[T:T + NS, 3 * D_A:]
    pool_s = jnp.concatenate([state_pool[0][:, 1:], u_s[:, None, :]], axis=1)
    kv_p, kv_s = [], []
    for g, (win, _) in enumerate(ATTN_GROUPS):
        n = min(win, T)
        kv = z1[T - n:T + NS, g * 3 * D_C + D_C:g * 3 * D_C + 3 * D_C].reshape(n + NS, 2, H_C, HD_C)
        kv_p.append(kv[:n][None, None])
        kv_s.append(kv[n:][None, :, None])
    return (y_p[None], y_s[:, None, :],
            h0[T - 1][None, None], h0[T:T + NS][None],
            wkv_p[None, None], wkv_s[None],
            u_p[None, None], pool_s[None],
            kv_p[0], kv_s[0], kv_p[1], kv_s[1], kv_p[2], kv_s[2])
```

```python
import functools
import math

import jax
import jax.numpy as jnp
from jax import lax
from jax.experimental import pallas as pl
from jax.experimental.pallas import tpu as pltpu

f32 = jnp.float32
bf16 = jnp.bfloat16

D_MODEL = 2048
H_A, HD_A = 16, 64
D_A = H_A * HD_A
D_POOL = D_MODEL - D_A
POOL_WINDOWS = (2, 4, 8, 16)
POOL_GD = D_POOL // len(POOL_WINDOWS)
POOL_BUF = max(POOL_WINDOWS) - 1
GN_EPS = 64e-5
NORM_EPS = 1e-6
ATTN_GROUPS = ((128, 1), (512, 4), (2048, 16))
H_C, HD_C = 8, 128
D_C = H_C * HD_C
N_KEYS = 128
N_EXPERTS = N_KEYS * N_KEYS
PEER_HEADS = 8
PEER_TOPK = 16

VMEM_LIMIT = 56 * 1024 * 1024
WKV_CHUNK = 64
ATTN_BLK = 2048
ATTN_UNROLL = 16
NEG_INF = float("-inf")
LOG2E = math.log2(math.e)


def _params(sem, vmem=VMEM_LIMIT, flags=None):
    return pltpu.CompilerParams(dimension_semantics=sem, vmem_limit_bytes=vmem, flags=flags)


def _dot(a, b):
    return jnp.dot(a, b, preferred_element_type=f32)


def _dot_nt(a, b):
    return lax.dot_general(a, b, (((1,), (1,)), ((), ())), preferred_element_type=f32)


def _dot_tn(a, b):
    return lax.dot_general(a, b, (((0,), (0,)), ((), ())), preferred_element_type=f32)


def _split(x):
    hi = x.astype(bf16)
    lo = (x - hi.astype(f32)).astype(bf16)
    return hi, lo


def _dot3(a, b, dot=_dot):
    ah, al = _split(a)
    bh, bl = _split(b)
    return dot(ah, bh) + (dot(ah, bl) + dot(al, bh))


def _head_sum(x, bd):
    hi, lo = _split(x)
    sh, sl = _split(_dot_nt(hi, bd) + _dot_nt(lo, bd))
    return _dot(sh, bd) + _dot(sl, bd)


def _gelu(x):
    return 0.5 * x * (1.0 + lax.erf(x * 0.7071067811865476))


def _mm_kernel(*refs, has_norm, has_res, emit_h, transpose_h):
    it = iter(refs)
    x_ref, w_ref = next(it), next(it)
    g_ref = next(it) if has_norm else None
    res_ref = next(it) if has_res else None
    o_ref = next(it)
    h_ref = next(it) if emit_h else None
    xb_ref = next(it)

    @pl.when(pl.program_id(1) == 0)
    def _():
        x = x_ref[...]
        if has_norm:
            ms = jnp.mean(x * x, axis=-1, keepdims=True)
            x = (x * lax.rsqrt(ms + NORM_EPS)) * g_ref[...]
            if emit_h:
                h_ref[...] = (x.T if transpose_h else x).astype(h_ref.dtype)
        xb_ref[...] = x.astype(bf16)

    acc = _dot(xb_ref[...], w_ref[...])
    if has_res:
        acc = acc + res_ref[...]
    o_ref[...] = acc


def _mm(x, w, *, norm_w=None, res=None, emit_h=None, transpose_h=False, tm, tn, row0=0, rows=None, name="proj"):
    K = x.shape[1]
    N = w.shape[1]
    rows = x.shape[0] if rows is None else rows
    assert rows % tm == 0 and N % tn == 0
    w_mode = dict(pipeline_mode=pl.Buffered(1)) if tn == N else {}
    in_specs = [pl.BlockSpec((tm, K), lambda i, j: (i + row0, 0)),
                pl.BlockSpec((K, tn), lambda i, j: (0, j), **w_mode)]
    args = [x, w]
    if norm_w is not None:
        in_specs.append(pl.BlockSpec((1, K), lambda i, j: (0, 0)))
        args.append(norm_w.reshape(1, K))
    if res is not None:
        in_specs.append(pl.BlockSpec((tm, tn), lambda i, j: (i + row0, j)))
        args.append(res)
    out_shape = [jax.ShapeDtypeStruct((rows, N), f32)]
    out_specs = [pl.BlockSpec((tm, tn), lambda i, j: (i, j))]
    if emit_h is not None and transpose_h:
        out_shape.append(jax.ShapeDtypeStruct((K, rows), emit_h))
        out_specs.append(pl.BlockSpec((K, tm), lambda i, j: (0, i)))
    elif emit_h is not None:
        out_shape.append(jax.ShapeDtypeStruct((rows, K), emit_h))
        out_specs.append(pl.BlockSpec((tm, K), lambda i, j: (i, 0)))
    outs = pl.pallas_call(
        functools.partial(_mm_kernel, has_norm=norm_w is not None, has_res=res is not None,
                          emit_h=emit_h is not None, transpose_h=transpose_h),
        grid=(rows // tm, N // tn),
        in_specs=in_specs, out_specs=out_specs, out_shape=out_shape,
        scratch_shapes=[pltpu.VMEM((tm, K), bf16)],
        compiler_params=_params(("parallel", "arbitrary")),
        name=name,
    )(*args)
    return outs if emit_h is not None else outs[0]


def _rms_kernel(x_ref, g_ref, o_ref):
    x = x_ref[...]
    ms = jnp.mean(x * x, axis=-1, keepdims=True)
    o_ref[...] = (x * lax.rsqrt(ms + NORM_EPS)) * g_ref[...]


def _rmsnorm(x, g, *, tm, row0, rows):
    K = x.shape[1]
    return pl.pallas_call(
        _rms_kernel, grid=(rows // tm,),
        in_specs=[pl.BlockSpec((tm, K), lambda i: (i + row0, 0)), pl.BlockSpec((1, K), lambda i: (0, 0))],
        out_specs=pl.BlockSpec((tm, K), lambda i: (i, 0)),
        out_shape=jax.ShapeDtypeStruct((rows, K), f32),
        compiler_params=_params(("parallel",)),
        name="rmsnorm",
    )(x, g.reshape(1, K))


def _even_token_math(h, hs, zc, zs, P):
    dh = hs - h
    mu = P["mu_wag"]
    xw = (h + dh * mu[0:1]).astype(bf16)
    xa = (h + dh * mu[1:2]).astype(bf16)
    xg = (h + dh * mu[2:3]).astype(bf16)
    tw = jnp.tanh(_dot(xw, P["w1"])).astype(bf16)
    wl = P["w0"] + _dot(tw, P["w2"])
    w_log = -jax.nn.softplus(-wl) - 0.5
    lw = -jnp.exp(w_log)
    a = jax.nn.sigmoid(P["a0"] + _dot(_dot(xa, P["a1"]).astype(bf16), P["a2"]))
    g = _dot(jax.nn.sigmoid(_dot(xg, P["g1"])).astype(bf16), P["g2"])
    rkv = zc + (zs - zc) * P["mu_rkv"]
    r, k, v = rkv[:, :D_A], rkv[:, D_A:2 * D_A], rkv[:, 2 * D_A:]
    kk = k * P["k_k"]
    nrm = jnp.sqrt(_head_sum(kk * kk, P["bd"]))
    kkn = kk / jnp.maximum(nrm, 1e-12)
    k2 = k * (1.0 + (a - 1.0) * P["k_a"])
    return r, lw, k2, v, kkn, a, g


def _pool_project(pm, P):
    outs = []
    for gi in range(len(POOL_WINDOWS)):
        c = slice(gi * POOL_GD, (gi + 1) * POOL_GD)
        outs.append(_dot(pm[:, c].astype(bf16), P["w_pool"][gi]))
    return jnp.concatenate(outs, axis=-1) * P["pool_scale"]


_EVEN_PARAM_NAMES = ("mu_wag", "mu_rkv", "w0", "w1", "w2", "a0", "a1", "a2", "g1", "g2", "k_k", "k_a",
                     "w_pool", "pool_scale", "bd")


def _load_params(refs):
    return {n: r[...] for n, r in zip(_EVEN_PARAM_NAMES, refs)}


def _even_mid_prompt_kernel(h_ref, hp_ref, z_ref, zp_ref, *rest, tm):
    prefs, outs = rest[:len(_EVEN_PARAM_NAMES)], rest[len(_EVEN_PARAM_NAMES):]
    P = _load_params(prefs)
    first = pl.program_id(0) == 0
    h = h_ref[...]
    z = z_ref[...]
    zc, u = z[:, :3 * D_A], z[:, 3 * D_A:]
    hprev = jnp.where(first, 0.0, hp_ref[15:16, :])
    zprev = jnp.where(first, 0.0, zp_ref[15:16, :3 * D_A])
    row = lax.broadcasted_iota(jnp.int32, (tm, 1), 0)
    hs = jnp.where(row == 0, hprev, pltpu.roll(h, 1, axis=0))
    zs = jnp.where(row == 0, zprev, pltpu.roll(zc, 1, axis=0))
    prow = lax.broadcasted_iota(jnp.int32, (tm, POOL_GD), 0)
    r, lw, k2, v, kkn, a, g = _even_token_math(h, hs, zc, zs, P)

    uprev = jnp.where(first, 0.0, zp_ref[:, 3 * D_A:])
    pos = pl.program_id(0) * tm + prow
    means = []
    for gi, win in enumerate(POOL_WINDOWS):
        c = slice(gi * POOL_GD, (gi + 1) * POOL_GD)
        s = jnp.concatenate([uprev[:, c], u[:, c]], axis=0)
        sh = 1
        while sh < win:
            s = s + pltpu.roll(s, sh, axis=0)
            sh *= 2
        cnt = jnp.minimum(pos + 1, win).astype(f32)
        means.append(s[16:, :] / cnt)
    ob = _pool_project(jnp.concatenate(means, axis=-1) - u, P)
    for o_ref, val in zip(outs, (r, lw, k2, v, kkn, a, g, ob)):
        o_ref[...] = val


def _even_mid_sample_kernel(h_ref, hs_ref, z_ref, zs_ref, buf_ref, *rest):
    prefs, outs = rest[:len(_EVEN_PARAM_NAMES)], rest[len(_EVEN_PARAM_NAMES):]
    P = _load_params(prefs)
    h = h_ref[...]
    z = z_ref[...]
    zc, u = z[:, :3 * D_A], z[:, 3 * D_A:]
    r, lw, k2, v, kkn, a, g = _even_token_math(h, hs_ref[...], zc, zs_ref[...], P)
    means = []
    for gi, win in enumerate(POOL_WINDOWS):
        c = slice(gi * POOL_GD, (gi + 1) * POOL_GD)
        s = u[:, c]
        for j in range(POOL_BUF - (win - 1), POOL_BUF):
            s = s + buf_ref[j, :, c]
        means.append(s / float(win))
    ob = _pool_project(jnp.concatenate(means, axis=-1) - u, P)
    for o_ref, val in zip(outs, (r, lw, k2, v, kkn, a, g, ob)):
        o_ref[...] = val


def _const_spec(shape):
    nd = len(shape)
    return pl.BlockSpec(shape, lambda i, _nd=nd: (0,) * _nd, pipeline_mode=pl.Buffered(1))


def _even_mid_prompt(h_all, z_all, plist, *, T, tm):
    nb = tm // 16
    in_specs = [pl.BlockSpec((tm, D_MODEL), lambda i: (i, 0)),
                pl.BlockSpec((16, D_MODEL), lambda i: (jnp.maximum(i * nb - 1, 0), 0)),
                pl.BlockSpec((tm, 4 * D_A), lambda i: (i, 0)),
                pl.BlockSpec((16, 4 * D_A), lambda i: (jnp.maximum(i * nb - 1, 0), 0))]
    in_specs += [_const_spec(p.shape) for p in plist]
    return pl.pallas_call(
        functools.partial(_even_mid_prompt_kernel, tm=tm),
        grid=(T // tm,), in_specs=in_specs,
        out_specs=[pl.BlockSpec((tm, D_A), lambda i: (i, 0))] * 8,
        out_shape=[jax.ShapeDtypeStruct((T, D_A), f32)] * 8,
        compiler_params=_params(("parallel",)),
        name="even_mid_prompt",
    )(h_all, h_all, z_all, z_all, *plist)


def _even_mid_sample(h_all, hs, z_all, zs, buf_t, plist, *, row0, n):
    in_specs = [pl.BlockSpec((n, D_MODEL), lambda i: (row0, 0)),
                pl.BlockSpec((n, D_MODEL), lambda i: (0, 0)),
                pl.BlockSpec((n, 4 * D_A), lambda i: (row0, 0)),
                pl.BlockSpec((n, 3 * D_A), lambda i: (0, 0)),
                pl.BlockSpec((POOL_BUF, n, D_POOL), lambda i: (0, 0, 0))]
    in_specs += [_const_spec(p.shape) for p in plist]
    return pl.pallas_call(
        _even_mid_sample_kernel, grid=(1,), in_specs=in_specs,
        out_specs=[pl.BlockSpec((n, D_A), lambda i: (0, 0))] * 8,
        out_shape=[jax.ShapeDtypeStruct((n, D_A), f32)] * 8,
        compiler_params=_params(("arbitrary",)),
        name="even_mid_sample",
    )(h_all, hs, z_all, zs, buf_t, *plist)


def _wkv_prompt_kernel(r_ref, lw_ref, k_ref, v_ref, kk_ref, a_ref, o_ref, s_out_ref, S_ref, *, tb, npair):
    C = WKV_CHUNK
    t = pl.program_id(1)

    @pl.when(t == 0)
    def _():
        S_ref[...] = jnp.zeros_like(S_ref)

    ri = lax.broadcasted_iota(jnp.int32, (C, C), 0)
    ci = lax.broadcasted_iota(jnp.int32, (C, C), 1)
    strict = ri > ci
    incl = ri >= ci
    eye = (ri == ci).astype(f32)
    rows = lax.broadcasted_iota(jnp.int32, (C, 2 * HD_A), 0)

    nh = 2 * npair
    heads = range(nh)

    def prep(sl, lanes):
        r, lw, k, v, kk, a = (x[sl, lanes] for x in (r_ref, lw_ref, k_ref, v_ref, kk_ref, a_ref))
        cum = lw
        sh = 1
        while sh < C:
            cum = cum + jnp.where(rows >= sh, pltpu.roll(cum, sh, axis=0), 0.0)
            sh *= 2
        cum_c = cum[C - 1:C, :]
        b = kk * a
        e_neg = jnp.exp(-cum)
        e_rem = jnp.exp(cum_c - cum)
        return dict(at=-kk * jnp.exp(cum - lw), rt=r * jnp.exp(cum), bt=b * e_neg, kt=k * e_neg,
                    bh=b * e_rem, kh=k * e_rem, w_c=jnp.exp(cum_c), v=v)

    def chunk(c, carry):
        off = pl.multiple_of(c * C, C)
        sl = pl.ds(off, C)
        pairs = [prep(sl, slice(p * 2 * HD_A, (p + 1) * 2 * HD_A)) for p in range(npair)]
        S0 = [S_ref[h] for h in heads]

        def head(name, h):
            return pairs[h // 2][name][:, (h % 2) * HD_A:(h % 2 + 1) * HD_A]

        ar = [jnp.concatenate([head("at", h), head("rt", h)], axis=0) for h in heads]
        bk = [jnp.concatenate([head("bt", h), head("kt", h)], axis=0) for h in heads]
        vh = [head("v", h) for h in heads]
        m4 = [_dot3(ar[h], bk[h], _dot_nt) for h in heads]
        sar = [_dot3(ar[h], S0[h], _dot_nt) for h in heads]
        a_ab = [jnp.where(strict, m4[h][:C, :C], 0.0) for h in heads]
        a_ak = [jnp.where(strict, m4[h][:C, C:], 0.0) for h in heads]
        a_rb = [jnp.where(incl, m4[h][C:, :C], 0.0) for h in heads]
        a_rk = [jnp.where(incl, m4[h][C:, C:], 0.0) for h in heads]
        av = [_dot3(jnp.concatenate([a_ak[h], a_rk[h]], axis=0), vh[h]) for h in heads]
        tinv = [eye + a_ab[h] for h in heads]
        pw = [_dot3(a_ab[h], a_ab[h]) for h in heads]
        n_sq = 2
        while n_sq * 2 < C:
            x = [_dot3(pw[h], jnp.concatenate([tinv[h], pw[h]], axis=1)) for h in heads]
            tinv = [tinv[h] + x[h][:, :C] for h in heads]
            pw = [x[h][:, C:] for h in heads]
            n_sq *= 2
        tinv = [tinv[h] + _dot3(pw[h], tinv[h]) for h in heads]
        u = [_dot3(tinv[h], sar[h][:C] + av[h][:C]) for h in heads]
        o = [sar[h][C:] + av[h][C:] + _dot3(a_rb[h], u[h]) for h in heads]
        s_new = []
        for h in heads:
            uv = jnp.concatenate([u[h], vh[h]], axis=0)
            bkh = jnp.concatenate([head("bh", h), head("kh", h)], axis=0)
            s_new.append(S0[h] * head("w_c", h) + _dot3(uv, bkh, _dot_tn))
        for h in heads:
            S_ref[h] = s_new[h]
        for p in range(npair):
            o_ref[sl, p * 2 * HD_A:(p + 1) * 2 * HD_A] = jnp.concatenate([o[2 * p], o[2 * p + 1]], axis=1)
        return carry

    lax.fori_loop(0, tb // C, chunk, 0)

    @pl.when(t == pl.num_programs(1) - 1)
    def _():
        s_out_ref[...] = S_ref[...]


def _wkv_prompt(r, lw, k2, v, kkn, a, *, T, tb, npair):
    spec = pl.BlockSpec((tb, 2 * HD_A * npair), lambda j, t: (t, j))
    nh = 2 * npair
    return pl.pallas_call(
        functools.partial(_wkv_prompt_kernel, tb=tb, npair=npair),
        grid=(H_A // nh, T // tb),
        in_specs=[spec] * 6,
        out_specs=[spec, pl.BlockSpec((nh, HD_A, HD_A), lambda j, t: (j, 0, 0))],
        out_shape=[jax.ShapeDtypeStruct((T, D_A), f32), jax.ShapeDtypeStruct((H_A, HD_A, HD_A), f32)],
        scratch_shapes=[pltpu.VMEM((nh, HD_A, HD_A), f32)],
        compiler_params=_params(("parallel", "arbitrary")),
        name="wkv_prompt",
    )(r, lw, k2, v, kkn, a)


def _wkv_sample_kernel(s_ref, r_ref, lw_ref, k_ref, kk_ref, a_ref, v_ref, o_ref, s_out_ref):
    S = s_ref[0]
    kk = kk_ref[0]
    sa = jnp.sum(S * (-kk), axis=-1, keepdims=True)
    S2 = S * jnp.exp(lw_ref[0]) + sa * (kk * a_ref[0]) + v_ref[0] * k_ref[0]
    s_out_ref[0] = S2
    o_ref[0] = jnp.sum(S2 * r_ref[0], axis=-1, keepdims=True)


def _wkv_sample(S0, r, lw, k2, v, kkn, a):
    n = S0.shape[0]
    row = lambda x: x.reshape(n, H_A, 1, HD_A)
    rspec = pl.BlockSpec((1, H_A, 1, HD_A), lambda b: (b, 0, 0, 0))
    cspec = pl.BlockSpec((1, H_A, HD_A, 1), lambda b: (b, 0, 0, 0))
    sspec = pl.BlockSpec((1, H_A, HD_A, HD_A), lambda b: (b, 0, 0, 0))
    o, S = pl.pallas_call(
        _wkv_sample_kernel, grid=(n,),
        in_specs=[sspec, rspec, rspec, rspec, rspec, rspec, cspec],
        out_specs=[cspec, sspec],
        out_shape=[jax.ShapeDtypeStruct((n, H_A, HD_A, 1), f32), jax.ShapeDtypeStruct(S0.shape, f32)],
        compiler_params=_params(("parallel",)),
        name="wkv_sample",
    )(S0, row(r), row(lw), row(k2), row(kkn), row(a), v.reshape(n, H_A, HD_A, 1))
    return o.reshape(n, D_A), S


def _even_post_kernel(x_ref, o_ref, r_ref, k_ref, v_ref, g_ref, ob_ref, wo_ref, rk_ref, gnw_ref, gnb_ref, bd_ref,
                      out_ref):
    bd = bd_ref[...]
    o = o_ref[...]
    inv = 1.0 / HD_A
    mu = _head_sum(o, bd) * inv
    d = o - mu
    var = _head_sum(d * d, bd) * inv
    on = d * lax.rsqrt(var + GN_EPS) * gnw_ref[...] + gnb_ref[...]
    v = v_ref[...]
    bonus = _head_sum(r_ref[...] * k_ref[...] * rk_ref[...], bd) * v
    oa = ((on + bonus) * g_ref[...]).astype(bf16)
    y = _dot(oa, wo_ref[:D_A, :]) + _dot(ob_ref[...].astype(bf16), wo_ref[D_A:, :])
    out_ref[...] = x_ref[...] + y


def _even_post(x_all, o, r, k2, v, g, ob, wo, rk, gnw, gnb, bd, *, row0, rows, tm):
    a_spec = pl.BlockSpec((tm, D_A), lambda i: (i, 0))
    x_spec = pl.BlockSpec((tm, D_MODEL), lambda i: (i + row0, 0))
    consts = [wo, rk, gnw, gnb, bd]
    return pl.pallas_call(
        _even_post_kernel, grid=(rows // tm,),
        in_specs=[x_spec] + [a_spec] * 6 + [_const_spec(c.shape) for c in consts],
        out_specs=x_spec,
        out_shape=jax.ShapeDtypeStruct(x_all.shape, f32),
        input_output_aliases={0: 0},
        compiler_params=_params(("parallel",)),
        name="even_post",
    )(x_all, o, r, k2, v, g, ob, *consts)


def _attn_prompt_kernel(*refs):
    in_refs, out_ref, og_ref, lg_ref = refs[:15], refs[15], refs[16], refs[17]
    n = pl.program_id(1)
    has_prev = n > 0
    scale = HD_C ** -0.5
    Q = 128
    ri = lax.broadcasted_iota(jnp.int32, (Q, Q), 0)
    ci = lax.broadcasted_iota(jnp.int32, (Q, Q), 1)
    mask_prev0 = ci >= ri
    mask_cur = ci <= ri

    for g, (_, dil) in enumerate(ATTN_GROUPS):
        q_ref, kp_ref, k_ref, vp_ref, v_ref = in_refs[5 * g:5 * g + 5]
        nblk = ATTN_BLK // (Q * dil)

        def units(it, carry, q_ref=q_ref, kp_ref=kp_ref, k_ref=k_ref, vp_ref=vp_ref, v_ref=v_ref,
                  dil=dil, nblk=nblk, g=g):
            us = [it * ATTN_UNROLL + k for k in range(ATTN_UNROLL)]
            rows, q, kc, vc, kp, vp, valid_prev = [], [], [], [], [], [], []
            for u in us:
                c = u // nblk
                m = u % nblk
                r = pl.ds(m * (Q * dil) + c, Q, stride=dil)
                in_blk = m > 0
                r_a = pl.ds(jnp.maximum(m - 1, 0) * (Q * dil) + c, Q, stride=dil)
                r_b = pl.ds((nblk - 1) * (Q * dil) + c, Q, stride=dil)
                rows.append(r)
                q.append(q_ref[r, :].astype(bf16))
                kc.append(k_ref[r, :].astype(bf16))
                vc.append(v_ref[r, :].astype(bf16))
                kp.append(jnp.where(in_blk, k_ref[r_a, :], kp_ref[r_b, :]).astype(bf16))
                vp.append(jnp.where(in_blk, v_ref[r_a, :], vp_ref[r_b, :]).astype(bf16))
                valid_prev.append(jnp.logical_or(in_blk, has_prev))
            n = range(ATTN_UNROLL)
            s_p = [_dot_nt(q[i], kp[i]) * scale for i in n]
            s_c = [_dot_nt(q[i], kc[i]) * scale for i in n]
            s_p = [jnp.where(jnp.logical_and(mask_prev0, valid_prev[i]), s_p[i], NEG_INF) for i in n]
            s_c = [jnp.where(mask_cur, s_c[i], NEG_INF) for i in n]
            mx = [jnp.maximum(jnp.max(s_p[i], axis=1, keepdims=True), jnp.max(s_c[i], axis=1, keepdims=True))
                  for i in n]
            p_p = [jnp.exp(s_p[i] - mx[i]) for i in n]
            p_c = [jnp.exp(s_c[i] - mx[i]) for i in n]
            den = [jnp.sum(p_p[i], axis=1, keepdims=True) + jnp.sum(p_c[i], axis=1, keepdims=True) for i in n]
            o = [(_dot(p_p[i].astype(bf16), vp[i]) + _dot(p_c[i].astype(bf16), vc[i])) / den[i] for i in n]
            lse = [mx[i] + jnp.log(den[i]) for i in n]
            for i in n:
                og_ref[g, rows[i], :] = o[i]
                lg_ref[g, rows[i], :] = jnp.broadcast_to(lse[i], (Q, HD_C))
            return carry

        lax.fori_loop(0, dil * nblk // ATTN_UNROLL, units, 0)

    l0, l1, l2 = lg_ref[0], lg_ref[1], lg_ref[2]
    mx = jnp.maximum(jnp.maximum(l0, l1), l2)
    e0, e1, e2 = jnp.exp(l0 - mx), jnp.exp(l1 - mx), jnp.exp(l2 - mx)
    out_ref[...] = (e0 * og_ref[0] + e1 * og_ref[1] + e2 * og_ref[2]) / (e0 + e1 + e2)


def _attn_prompt(z, *, T):
    nb = T // ATTN_BLK
    in_specs = []
    for g in range(len(ATTN_GROUPS)):
        def col(j, g=g):
            return lambda h, n: (n, g * 3 * H_C + j * H_C + h)

        def col_prev(j, g=g):
            return lambda h, n: (jnp.maximum(n - 1, 0), g * 3 * H_C + j * H_C + h)
        blk = (ATTN_BLK, HD_C)
        in_specs += [pl.BlockSpec(blk, col(0)), pl.BlockSpec(blk, col_prev(1)), pl.BlockSpec(blk, col(1)),
                     pl.BlockSpec(blk, col_prev(2)), pl.BlockSpec(blk, col(2))]
    return pl.pallas_call(
        _attn_prompt_kernel, grid=(H_C, nb),
        in_specs=in_specs,
        out_specs=pl.BlockSpec((ATTN_BLK, HD_C), lambda h, n: (n, h)),
        out_shape=jax.ShapeDtypeStruct((T, D_C), f32),
        scratch_shapes=[pltpu.VMEM((3, ATTN_BLK, HD_C), f32), pltpu.VMEM((3, ATTN_BLK, HD_C), f32)],
        compiler_params=_params(("parallel", "arbitrary")),
        name="attn_prompt",
    )(*([z] * 15))


ATTN_SAMPLE_NB = 4


def _attn_sample_kernel(z_ref, c0_ref, c1_ref, c2_ref, out_ref):
    scale = HD_C ** -0.5
    crefs = (c0_ref, c1_ref, c2_ref)
    for b in range(ATTN_SAMPLE_NB):
        for h in range(H_C):
            outs, lses = [], []
            for g in range(len(ATTN_GROUPS)):
                base = g * 3 * D_C + h * HD_C
                q = z_ref[b, :, base:base + HD_C]
                kn = z_ref[b, :, base + D_C:base + D_C + HD_C]
                vn = z_ref[b, :, base + 2 * D_C:base + 2 * D_C + HD_C]
                K = crefs[g][b, :, 0, 0, h, :]
                V = crefs[g][b, :, 0, 1, h, :]
                s_c = jnp.sum(K * q, axis=1, keepdims=True) * scale
                s_n = jnp.sum(kn * q, axis=1, keepdims=True) * scale
                mx = jnp.maximum(jnp.max(s_c, axis=0, keepdims=True), s_n)
                p_c = jnp.exp(s_c - mx)
                p_n = jnp.exp(s_n - mx)
                den = jnp.sum(p_c, axis=0, keepdims=True) + p_n
                outs.append((jnp.sum(p_c * V, axis=0, keepdims=True) + p_n * vn) / den)
                lses.append(mx + jnp.log(den))
            mx = jnp.maximum(jnp.maximum(lses[0], lses[1]), lses[2])
            es = [jnp.exp(l - mx) for l in lses]
            out_ref[b, :, h * HD_C:(h + 1) * HD_C] = (es[0] * outs[0] + es[1] * outs[1] + es[2] * outs[2]) / (
                es[0] + es[1] + es[2])


def _attn_sample(z_s, caches):
    n = z_s.shape[0]
    nb = ATTN_SAMPLE_NB
    assert n % nb == 0
    in_specs = [pl.BlockSpec((nb, 1, 9 * D_C), lambda b: (b, 0, 0))]
    args = [z_s.reshape(n, 1, 9 * D_C)]
    for (win, dil), c in zip(ATTN_GROUPS, caches):
        assert c.shape[1] == win
        args.append(c.reshape(n, win // dil, dil, 2, H_C, HD_C))
        in_specs.append(pl.BlockSpec((nb, win // dil, 1, 2, H_C, HD_C), lambda b: (b, 0, 0, 0, 0, 0)))
    out = pl.pallas_call(
        _attn_sample_kernel, grid=(n // nb,), in_specs=in_specs,
        out_specs=pl.BlockSpec((nb, 1, D_C), lambda b: (b, 0, 0)),
        out_shape=jax.ShapeDtypeStruct((n, 1, D_C), f32),
        compiler_params=_params(("parallel",)),
        name="attn_sample",
    )(*args)
    return out.reshape(n, D_C)


def _top16_rows(s):
    rows = lax.broadcasted_iota(jnp.int32, s.shape, 0)
    work = s
    vals = []
    for _ in range(PEER_TOPK):
        m = jnp.max(work, axis=0, keepdims=True)
        idx = jnp.min(jnp.where(work == m, rows, s.shape[0]), axis=0, keepdims=True)
        work = jnp.where(rows == idx, NEG_INF, work)
        vals.append(m)
    return vals, jnp.logical_and(work == NEG_INF, s != NEG_INF)


def _oddeven_mergesort_pairs(n):
    pairs = []

    def merge(lo, hi, r):
        step = r * 2
        if step < hi - lo:
            merge(lo, hi, step)
            merge(lo + r, hi, step)
            pairs.extend((i, i + r) for i in range(lo + r, hi - r, step))
        else:
            pairs.append((lo, lo + r))

    def sort(lo, hi):
        if hi - lo >= 1:
            mid = lo + (hi - lo) // 2
            sort(lo, mid)
            sort(mid + 1, hi)
            merge(lo, hi, 1)

    sort(0, n - 1)
    return pairs


SUBLANES = 8


def _top16_sorted(s):
    assert s.shape[0] == PEER_TOPK * SUBLANES
    t = [s[SUBLANES * k:SUBLANES * (k + 1), :] for k in range(PEER_TOPK)]
    for i, j in _oddeven_mergesort_pairs(PEER_TOPK):
        t[i], t[j] = jnp.maximum(t[i], t[j]), jnp.minimum(t[i], t[j])
    shift = 1
    while shift < SUBLANES:
        other = [pltpu.roll(x, shift, axis=0) for x in t]
        t = [jnp.maximum(t[i], other[PEER_TOPK - 1 - i]) for i in range(PEER_TOPK)]
        d = PEER_TOPK // 2
        while d >= 1:
            for i in range(PEER_TOPK):
                if i & d == 0:
                    t[i], t[i + d] = jnp.maximum(t[i], t[i + d]), jnp.minimum(t[i], t[i + d])
            d //= 2
        shift *= 2
    thr = t[PEER_TOPK - 1][0:1, :]
    member = s >= thr
    count = jnp.sum(member.astype(f32), axis=0, keepdims=True)
    return [x[0:1, :] for x in t], member, count == float(PEER_TOPK)


def _top16_rows_distinct(ss):
    work = list(ss)
    vals = [[] for _ in ss]
    for _ in range(PEER_TOPK):
        for i in range(len(ss)):
            m = jnp.max(work[i], axis=0, keepdims=True)
            work[i] = jnp.where(work[i] == m, NEG_INF, work[i])
            vals[i].append(m)
    out = []
    for s, w, v in zip(ss, work, vals):
        member = jnp.logical_and(w == NEG_INF, s != NEG_INF)
        count = jnp.sum(member.astype(f32), axis=0, keepdims=True)
        out.append((v, member, count == float(PEER_TOPK)))
    return out


def _peer_route_kernel(q_ref, keys_ref, s1_ref, s2_ref, tau_ref):
    n = q_ref.shape[0]

    def route_head(h, exact):
        if exact:
            top16 = lambda ss: [_top16_rows(s) + (None,) for s in ss]
            top16_scores = top16
        else:
            top16 = _top16_rows_distinct
            top16_scores = lambda ss: [_top16_sorted(s) for s in ss]
        scores = []
        for p in range(2):
            hp = 2 * h + p
            qs = q_ref[:, hp * N_KEYS:(hp + 1) * N_KEYS].astype(bf16)
            scores.append(_dot_nt(keys_ref[hp], qs) * LOG2E)
        sm, vals, oks = [], [], []
        for s, (v, member, ok) in zip(scores, top16_scores(scores)):
            sm.append(jnp.where(member, s, NEG_INF))
            vals.append(v)
            oks.append(ok)

        def pair_sums(z1, z2):
            pieces = []
            for r1 in range(PEER_TOPK):
                cnt = PEER_TOPK // (r1 + 1)
                pieces.append(z1[r1] + jnp.concatenate(z2[:cnt], axis=0))
            npad = (-sum(p.shape[0] for p in pieces)) % 8
            pieces.append(jnp.full((npad, n), NEG_INF, f32))
            return jnp.concatenate(pieces, axis=0)

        z1 = [v - vals[0][0] for v in vals[0]]
        z2 = [v - vals[1][0] for v in vals[1]]
        (top, chosen, ok), = top16([pair_sums(z1, z2)])
        oks.append(ok)
        log_z = jnp.log2(sum(jnp.exp2(t) for t in top))
        z1 = [z - log_z for z in z1]
        sums = pair_sums(z1, z2)
        if exact:
            (top, _, _), = top16([sums])
            tau = top[-1]
        else:
            tau = jnp.min(jnp.where(chosen, sums, jnp.inf), axis=0, keepdims=True)
            count = jnp.sum((sums >= tau).astype(f32), axis=0, keepdims=True)
            oks.append(count == float(PEER_TOPK))
        rows = slice(h * N_KEYS, (h + 1) * N_KEYS)
        s1_ref[rows, :] = (sm[0] - vals[0][0]) - log_z
        s2_ref[rows, :] = sm[1] - vals[1][0]
        tau_ref[h:h + 1, :] = tau
        if exact:
            return None
        all_ok = functools.reduce(jnp.logical_and, oks)
        return jnp.min(all_ok.astype(f32)) > 0.5

    tie_free = [route_head(h, exact=False) for h in range(PEER_HEADS)]
    for h in range(PEER_HEADS):
        pl.when(jnp.logical_not(tie_free[h]))(functools.partial(route_head, h, True))


def _peer_route(q, keys, *, tmr):
    M = q.shape[0]
    big = jax.ShapeDtypeStruct((PEER_HEADS * N_KEYS, M), f32)
    bspec = pl.BlockSpec((PEER_HEADS * N_KEYS, tmr), lambda i: (0, i))
    return pl.pallas_call(
        _peer_route_kernel, grid=(M // tmr,),
        in_specs=[pl.BlockSpec((tmr, 2 * PEER_HEADS * N_KEYS), lambda i: (i, 0)),
                  _const_spec(keys.shape)],
        out_specs=[bspec] * 2 + [pl.BlockSpec((PEER_HEADS, tmr), lambda i: (0, i))],
        out_shape=[big] * 2 + [jax.ShapeDtypeStruct((PEER_HEADS, M), f32)],
        compiler_params=_params(("parallel",)),
        name="peer_route",
    )(q, keys)


def _peer_prep_kernel(u_ref, v_ref, ub_ref, vt_ref):
    ub_ref[...] = u_ref[...].astype(bf16)
    vt_ref[...] = v_ref[...].T.astype(bf16)


def _peer_prep(u_tab, v_tab, *, te):
    L, E, D = u_tab.shape
    return pl.pallas_call(
        _peer_prep_kernel, grid=(L, E // te),
        in_specs=[pl.BlockSpec((None, te, D), lambda l, j: (l, j, 0))] * 2,
        out_specs=[pl.BlockSpec((None, te, D), lambda l, j: (l, j, 0)),
                   pl.BlockSpec((None, D, te), lambda l, j: (l, 0, j))],
        out_shape=[jax.ShapeDtypeStruct((L, E, D), bf16), jax.ShapeDtypeStruct((L, D, E), bf16)],
        compiler_params=_params(("parallel", "parallel")),
        name="peer_prep",
    )(u_tab, v_tab)


PEER_RB = 32
PEER_LC = 256
PEER_I1 = 4


def _peer_dense_kernel(xt_ref, u_ref, vt_ref, s1_ref, s2_ref, tau_ref, res_ref, out_ref, acc_ref, *, tm, te):
    j = pl.program_id(1)
    n_i1 = te // N_KEYS
    n_rb = N_KEYS // PEER_RB

    @pl.when(j == 0)
    def _():
        acc_ref[...] = jnp.zeros_like(acc_ref)

    chunks = [slice(c * PEER_LC, (c + 1) * PEER_LC) for c in range(tm // PEER_LC)]
    assert n_i1 == SUBLANES
    s1_rows = [pl.ds(pl.multiple_of(h * N_KEYS + j * n_i1, n_i1), n_i1) for h in range(PEER_HEADS)]
    act = _dot(u_ref[...], xt_ref[...])
    cols = []
    for lanes in chunks:
        s1 = [s1_ref[s1_rows[h], lanes] for h in range(PEER_HEADS)]
        pieces = [[None] * n_rb for _ in range(n_i1)]
        for rb in range(n_rb):
            for i0 in range(0, n_i1, PEER_I1):
                group = range(i0, i0 + PEER_I1)
                gates = {ii: jnp.zeros((PEER_RB, PEER_LC), f32) for ii in group}
                for h in range(PEER_HEADS):
                    s2 = s2_ref[h * N_KEYS + rb * PEER_RB:h * N_KEYS + (rb + 1) * PEER_RB, lanes]
                    tau = tau_ref[h:h + 1, lanes]
                    for ii in group:
                        c = s2 + s1[h][ii:ii + 1, :]
                        gates[ii] = gates[ii] + jnp.where(c >= tau, jnp.exp2(c), 0.0)
                for ii in group:
                    a = act[ii * N_KEYS + rb * PEER_RB:ii * N_KEYS + (rb + 1) * PEER_RB, lanes]
                    pieces[ii][rb] = (_gelu(a) * gates[ii]).astype(bf16)
        cols.append(jnp.concatenate([pc for row in pieces for pc in row], axis=0))
    acc_ref[...] += _dot(vt_ref[...], jnp.concatenate(cols, axis=1))

    @pl.when(j == pl.num_programs(1) - 1)
    def _():
        out_ref[...] = acc_ref[...].T + res_ref[...]


def _peer_dense(xt, ub, vt, layer, s1, s2, tau, res, *, tm, te):
    D, M = xt.shape
    E = ub.shape[1]
    once = dict(pipeline_mode=pl.Buffered(1))
    rspec = pl.BlockSpec((PEER_HEADS * N_KEYS, tm), lambda i, j: (0, i), **once)
    return pl.pallas_call(
        functools.partial(_peer_dense_kernel, tm=tm, te=te),
        grid=(M // tm, E // te),
        in_specs=[pl.BlockSpec((D, tm), lambda i, j: (0, i)),
                  pl.BlockSpec((None, te, D), lambda i, j: (layer, j, 0)),
                  pl.BlockSpec((None, D, te), lambda i, j: (layer, 0, j)),
                  rspec, rspec,
                  pl.BlockSpec((PEER_HEADS, tm), lambda i, j: (0, i)),
                  pl.BlockSpec((tm, D), lambda i, j: (i, 0), **once)],
        out_specs=pl.BlockSpec((tm, D), lambda i, j: (i, 0), **once),
        out_shape=jax.ShapeDtypeStruct((M, D), f32),
        scratch_shapes=[pltpu.VMEM((D, tm), f32)],
        compiler_params=_params(("parallel", "arbitrary")),
        name="peer_dense",
    )(xt, ub, vt, s1, s2, tau, res)


def _peer(x_all, norm_w, wq, keys, ub, vt, layer, *, tm_mm, tmr, tm, te):
    q, ht = _mm(x_all, wq, norm_w=norm_w, emit_h=bf16, transpose_h=True, tm=tm_mm, tn=wq.shape[1],
                name="peer_query")
    s1, s2, tau = _peer_route(q, keys, tmr=tmr)
    return _peer_dense(ht, ub, vt, layer, s1, s2, tau, x_all, tm=tm, te=te)


LANE = 128


def _block_diag_ones():
    head_of_lane = jnp.arange(D_A) // HD_A
    return (jnp.arange(LANE)[:, None] == head_of_lane[None, :]).astype(bf16)


def _pad_lanes(w):
    return jnp.pad(w, ((0, 0), (0, (-w.shape[1]) % LANE)))


def _pad_rows(w):
    return jnp.pad(w, ((0, (-w.shape[0]) % LANE), (0, 0)))


def kernel(x_prompt, x_sample, state_shift, state_wkv, state_pool, cache_kv_w128, cache_kv_w512, cache_kv_w2048,
           norm_mix, norm_ffn, norm_final, a_w_in, a_w_out, a_mu_rkv, a_mu_wag, a_w0, a_w1, a_w2, a_a0, a_a1,
           a_a2, a_g1, a_g2, a_k_k, a_k_a, a_r_k, a_gn_w, a_gn_b, b_w_pool, b_scale, c_w_in, c_w_out, p_w_q,
           p_sub_keys, p_u, p_v):
    T = x_prompt.shape[1]
    NS = x_sample.shape[0]
    TM = 768
    M = -(-(T + NS) // TM) * TM
    S_BLK = T // NS
    assert T % NS == 0 and T % ATTN_BLK == 0
    cb = lambda w: w.astype(bf16)
    row = lambda w: w.reshape(1, -1)

    x0 = jnp.concatenate([x_prompt.reshape(T, D_MODEL), x_sample.reshape(NS, D_MODEL),
                          jnp.zeros((M - T - NS, D_MODEL), f32)], axis=0)
    ub, vt = _peer_prep(p_u, p_v, te=512)
    bd = _block_diag_ones()

    w_in = cb(a_w_in[0])
    z0, h0 = _mm(x0, w_in, norm_w=norm_mix[0], emit_h=f32, tm=TM, tn=1024, name="even_in_proj")
    zs_prev = _mm(state_shift[0], w_in[:, :3 * D_A], tm=NS, tn=1024, name="even_in_proj_state")
    plist = [a_mu_wag[0], row(a_mu_rkv[0]), row(a_w0[0]), _pad_lanes(cb(a_w1[0])), _pad_rows(cb(a_w2[0])),
             row(a_a0[0]), _pad_lanes(cb(a_a1[0])), _pad_rows(cb(a_a2[0])), cb(a_g1[0]), cb(a_g2[0]),
             row(a_k_k[0]), row(a_k_a[0]), cb(b_w_pool[0]), row(b_scale[0]), bd]
    rp, lwp, kp, vp, kkp, ap, gp, obp = _even_mid_prompt(h0, z0, plist, T=T, tm=256)
    rs, lws, ks, vs, kks, as_, gs, obs = _even_mid_sample(
        h0, state_shift[0], z0, zs_prev, jnp.swapaxes(state_pool[0], 0, 1), plist, row0=S_BLK, n=NS)
    o_p, wkv_p = _wkv_prompt(rp, lwp, kp, vp, kkp, ap, T=T, tb=512, npair=8)
    o_s, wkv_s = _wkv_sample(state_wkv[0], rs, lws, ks, vs, kks, as_)
    post_c = (cb(a_w_out[0]), row(a_r_k[0]), row(a_gn_w[0]), row(a_gn_b[0]), bd)
    x1 = _even_post(x0, o_p, rp, kp, vp, gp, obp, *post_c, row0=0, rows=T, tm=256)
    x1 = _even_post(x1, o_s, rs, ks, vs, gs, obs, *post_c, row0=S_BLK, rows=NS, tm=NS)
    x1 = _peer(x1, norm_ffn[0], cb(p_w_q[0]), cb(p_sub_keys[0].reshape(2 * PEER_HEADS, N_KEYS, N_KEYS)),
               ub, vt, 0, tm_mm=TM, tmr=256, tm=TM, te=1024)

    z1 = _mm(x1, cb(c_w_in[0]), norm_w=norm_mix[1], tm=TM, tn=1024, name="odd_in_proj")
    att_p = _attn_prompt(z1, T=T)
    z1s = z1[T:T + NS]
    att_s = _attn_sample(z1s, (cache_kv_w128[0], cache_kv_w512[0], cache_kv_w2048[0]))
    att = jnp.concatenate([att_p, att_s, jnp.zeros((M - T - NS, D_C), f32)], axis=0)
    x2 = _mm(att, cb(c_w_out[0]), res=x1, tm=TM, tn=D_MODEL, name="odd_out_proj")
    x2 = _peer(x2, norm_ffn[1], cb(p_w_q[1]), cb(p_sub_keys[1].reshape(2 * PEER_HEADS, N_KEYS, N_KEYS)),
               ub, vt, 1, tm_mm=TM, tmr=256, tm=TM, te=1024)

    y_p = _rmsnorm(x2, norm_final, tm=512, row0=0, rows=T)
    y_s = _rmsnorm(x2, norm_final, tm=NS, row0=S_BLK, rows=NS)

    u_p = z0[T - POOL_BUF:T, 3 * D_A:]
    u_s = z0[T:T + NS, 3 * D_A:]
    pool_s = jnp.concatenate([state_pool[0][:, 1:], u_s[:, None, :]], axis=1)
    kv_p, kv_s = [], []
    for g, (win, _) in enumerate(ATTN_GROUPS):
        n = min(win, T)
        kv = z1[T - n:T + NS, g * 3 * D_C + D_C:g * 3 * D_C + 3 * D_C].reshape(n + NS, 2, H_C, HD_C)
        kv_p.append(kv[:n][None, None])
        kv_s.append(kv[n:][None, :, None])
    return (y_p[None], y_s[:, None, :],
            h0[T - 1][None, None], h0[T:T + NS][None],
            wkv_p[None, None], wkv_s[None],
            u_p[None, None], pool_s[None],
            kv_p[0], kv_s[0], kv_p[1], kv_s[1], kv_p[2], kv_s[2])
```

```python
import functools
import math

import jax
import jax.numpy as jnp
from jax import lax
from jax.experimental import pallas as pl
from jax.experimental.pallas import tpu as pltpu

f32 = jnp.float32
bf16 = jnp.bfloat16

D_MODEL = 2048
H_A, HD_A = 16, 64
D_A = H_A * HD_A
D_POOL = D_MODEL - D_A
POOL_WINDOWS = (2, 4, 8, 16)
POOL_GD = D_POOL // len(POOL_WINDOWS)
POOL_BUF = max(POOL_WINDOWS) - 1
GN_EPS = 64e-5
NORM_EPS = 1e-6
ATTN_GROUPS = ((128, 1), (512, 4), (2048, 16))
H_C, HD_C = 8, 128
D_C = H_C * HD_C
N_KEYS = 128
N_EXPERTS = N_KEYS * N_KEYS
PEER_HEADS = 8
PEER_TOPK = 16

VMEM_LIMIT = 56 * 1024 * 1024
WKV_CHUNK = 64
ATTN_BLK = 2048
ATTN_UNROLL = 16
NEG_INF = float("-inf")
LOG2E = math.log2(math.e)


def _params(sem, vmem=VMEM_LIMIT, flags=None):
    return pltpu.CompilerParams(dimension_semantics=sem, vmem_limit_bytes=vmem, flags=flags)


def _dot(a, b):
    return jnp.dot(a, b, preferred_element_type=f32)


def _dot_nt(a, b):
    return lax.dot_general(a, b, (((1,), (1,)), ((), ())), preferred_element_type=f32)


def _dot_tn(a, b):
    return lax.dot_general(a, b, (((0,), (0,)), ((), ())), preferred_element_type=f32)


def _split(x):
    hi = x.astype(bf16)
    lo = (x - hi.astype(f32)).astype(bf16)
    return hi, lo


def _dot3(a, b, dot=_dot):
    ah, al = _split(a)
    bh, bl = _split(b)
    return dot(ah, bh) + (dot(ah, bl) + dot(al, bh))


def _head_sum(x, bd):
    hi, lo = _split(x)
    sh, sl = _split(_dot_nt(hi, bd) + _dot_nt(lo, bd))
    return _dot(sh, bd) + _dot(sl, bd)


def _gelu(x):
    return 0.5 * x * (1.0 + lax.erf(x * 0.7071067811865476))


def _mm_kernel(*refs, has_norm, has_res, emit_h, transpose_h):
    it = iter(refs)
    x_ref, w_ref = next(it), next(it)
    g_ref = next(it) if has_norm else None
    res_ref = next(it) if has_res else None
    o_ref = next(it)
    h_ref = next(it) if emit_h else None
    xb_ref = next(it)

    @pl.when(pl.program_id(1) == 0)
    def _():
        x = x_ref[...]
        if has_norm:
            ms = jnp.mean(x * x, axis=-1, keepdims=True)
            x = (x * lax.rsqrt(ms + NORM_EPS)) * g_ref[...]
            if emit_h:
                h_ref[...] = (x.T if transpose_h else x).astype(h_ref.dtype)
        xb_ref[...] = x.astype(bf16)

    acc = _dot(xb_ref[...], w_ref[...])
    if has_res:
        acc = acc + res_ref[...]
    o_ref[...] = acc


def _mm(x, w, *, norm_w=None, res=None, emit_h=None, transpose_h=False, tm, tn, row0=0, rows=None, name="proj"):
    K = x.shape[1]
    N = w.shape[1]
    rows = x.shape[0] if rows is None else rows
    assert rows % tm == 0 and N % tn == 0
    w_mode = dict(pipeline_mode=pl.Buffered(1)) if tn == N else {}
    in_specs = [pl.BlockSpec((tm, K), lambda i, j: (i + row0, 0)),
                pl.BlockSpec((K, tn), lambda i, j: (0, j), **w_mode)]
    args = [x, w]
    if norm_w is not None:
        in_specs.append(pl.BlockSpec((1, K), lambda i, j: (0, 0)))
        args.append(norm_w.reshape(1, K))
    if res is not None:
        in_specs.append(pl.BlockSpec((tm, tn), lambda i, j: (i + row0, j)))
        args.append(res)
    out_shape = [jax.ShapeDtypeStruct((rows, N), f32)]
    out_specs = [pl.BlockSpec((tm, tn), lambda i, j: (i, j))]
    if emit_h is not None and transpose_h:
        out_shape.append(jax.ShapeDtypeStruct((K, rows), emit_h))
        out_specs.append(pl.BlockSpec((K, tm), lambda i, j: (0, i)))
    elif emit_h is not None:
        out_shape.append(jax.ShapeDtypeStruct((rows, K), emit_h))
        out_specs.append(pl.BlockSpec((tm, K), lambda i, j: (i, 0)))
    outs = pl.pallas_call(
        functools.partial(_mm_kernel, has_norm=norm_w is not None, has_res=res is not None,
                          emit_h=emit_h is not None, transpose_h=transpose_h),
        grid=(rows // tm, N // tn),
        in_specs=in_specs, out_specs=out_specs, out_shape=out_shape,
        scratch_shapes=[pltpu.VMEM((tm, K), bf16)],
        compiler_params=_params(("parallel", "arbitrary")),
        name=name,
    )(*args)
    return outs if emit_h is not None else outs[0]


def _rms_kernel(x_ref, g_ref, o_ref):
    x = x_ref[...]
    ms = jnp.mean(x * x, axis=-1, keepdims=True)
    o_ref[...] = (x * lax.rsqrt(ms + NORM_EPS)) * g_ref[...]


def _rmsnorm(x, g, *, tm, row0, rows):
    K = x.shape[1]
    return pl.pallas_call(
        _rms_kernel, grid=(rows // tm,),
        in_specs=[pl.BlockSpec((tm, K), lambda i: (i + row0, 0)), pl.BlockSpec((1, K), lambda i: (0, 0))],
        out_specs=pl.BlockSpec((tm, K), lambda i: (i, 0)),
        out_shape=jax.ShapeDtypeStruct((rows, K), f32),
        compiler_params=_params(("parallel",)),
        name="rmsnorm",
    )(x, g.reshape(1, K))


def _even_token_math(h, hs, zc, zs, P):
    dh = hs - h
    mu = P["mu_wag"]
    xw = (h + dh * mu[0:1]).astype(bf16)
    xa = (h + dh * mu[1:2]).astype(bf16)
    xg = (h + dh * mu[2:3]).astype(bf16)
    tw = jnp.tanh(_dot(xw, P["w1"])).astype(bf16)
    wl = P["w0"] + _dot(tw, P["w2"])
    w_log = -jax.nn.softplus(-wl) - 0.5
    lw = -jnp.exp(w_log)
    a = jax.nn.sigmoid(P["a0"] + _dot(_dot(xa, P["a1"]).astype(bf16), P["a2"]))
    g = _dot(jax.nn.sigmoid(_dot(xg, P["g1"])).astype(bf16), P["g2"])
    rkv = zc + (zs - zc) * P["mu_rkv"]
    r, k, v = rkv[:, :D_A], rkv[:, D_A:2 * D_A], rkv[:, 2 * D_A:]
    kk = k * P["k_k"]
    nrm = jnp.sqrt(_head_sum(kk * kk, P["bd"]))
    kkn = kk / jnp.maximum(nrm, 1e-12)
    k2 = k * (1.0 + (a - 1.0) * P["k_a"])
    return r, lw, k2, v, kkn, a, g


def _pool_project(pm, P):
    outs = []
    for gi in range(len(POOL_WINDOWS)):
        c = slice(gi * POOL_GD, (gi + 1) * POOL_GD)
        outs.append(_dot(pm[:, c].astype(bf16), P["w_pool"][gi]))
    return jnp.concatenate(outs, axis=-1) * P["pool_scale"]


_EVEN_PARAM_NAMES = ("mu_wag", "mu_rkv", "w0", "w1", "w2", "a0", "a1", "a2", "g1", "g2", "k_k", "k_a",
                     "w_pool", "pool_scale", "bd")


def _load_params(refs):
    return {n: r[...] for n, r in zip(_EVEN_PARAM_NAMES, refs)}


def _even_mid_prompt_kernel(h_ref, hp_ref, z_ref, zp_ref, *rest, tm):
    prefs, outs = rest[:len(_EVEN_PARAM_NAMES)], rest[len(_EVEN_PARAM_NAMES):]
    P = _load_params(prefs)
    first = pl.program_id(0) == 0
    h = h_ref[...]
    z = z_ref[...]
    zc, u = z[:, :3 * D_A], z[:, 3 * D_A:]
    hprev = jnp.where(first, 0.0, hp_ref[15:16, :])
    zprev = jnp.where(first, 0.0, zp_ref[15:16, :3 * D_A])
    row = lax.broadcasted_iota(jnp.int32, (tm, 1), 0)
    hs = jnp.where(row == 0, hprev, pltpu.roll(h, 1, axis=0))
    zs = jnp.where(row == 0, zprev, pltpu.roll(zc, 1, axis=0))
    prow = lax.broadcasted_iota(jnp.int32, (tm, POOL_GD), 0)
    r, lw, k2, v, kkn, a, g = _even_token_math(h, hs, zc, zs, P)

    uprev = jnp.where(first, 0.0, zp_ref[:, 3 * D_A:])
    pos = pl.program_id(0) * tm + prow
    means = []
    for gi, win in enumerate(POOL_WINDOWS):
        c = slice(gi * POOL_GD, (gi + 1) * POOL_GD)
        s = jnp.concatenate([uprev[:, c], u[:, c]], axis=0)
        sh = 1
        while sh < win:
            s = s + pltpu.roll(s, sh, axis=0)
            sh *= 2
        cnt = jnp.minimum(pos + 1, win).astype(f32)
        means.append(s[16:, :] / cnt)
    ob = _pool_project(jnp.concatenate(means, axis=-1) - u, P)
    for o_ref, val in zip(outs, (r, lw, k2, v, kkn, a, g, ob)):
        o_ref[...] = val


def _even_mid_sample_kernel(h_ref, hs_ref, z_ref, zs_ref, buf_ref, *rest):
    prefs, outs = rest[:len(_EVEN_PARAM_NAMES)], rest[len(_EVEN_PARAM_NAMES):]
    P = _load_params(prefs)
    h = h_ref[...]
    z = z_ref[...]
    zc, u = z[:, :3 * D_A], z[:, 3 * D_A:]
    r, lw, k2, v, kkn, a, g = _even_token_math(h, hs_ref[...], zc, zs_ref[...], P)
    means = []
    for gi, win in enumerate(POOL_WINDOWS):
        c = slice(gi * POOL_GD, (gi + 1) * POOL_GD)
        s = u[:, c]
        for j in range(POOL_BUF - (win - 1), POOL_BUF):
            s = s + buf_ref[j, :, c]
        means.append(s / float(win))
    ob = _pool_project(jnp.concatenate(means, axis=-1) - u, P)
    for o_ref, val in zip(outs, (r, lw, k2, v, kkn, a, g, ob)):
        o_ref[...] = val


def _const_spec(shape):
    nd = len(shape)
    return pl.BlockSpec(shape, lambda i, _nd=nd: (0,) * _nd, pipeline_mode=pl.Buffered(1))


def _even_mid_prompt(h_all, z_all, plist, *, T, tm):
    nb = tm // 16
    in_specs = [pl.BlockSpec((tm, D_MODEL), lambda i: (i, 0)),
                pl.BlockSpec((16, D_MODEL), lambda i: (jnp.maximum(i * nb - 1, 0), 0)),
                pl.BlockSpec((tm, 4 * D_A), lambda i: (i, 0)),
                pl.BlockSpec((16, 4 * D_A), lambda i: (jnp.maximum(i * nb - 1, 0), 0))]
    in_specs += [_const_spec(p.shape) for p in plist]
    return pl.pallas_call(
        functools.partial(_even_mid_prompt_kernel, tm=tm),
        grid=(T // tm,), in_specs=in_specs,
        out_specs=[pl.BlockSpec((tm, D_A), lambda i: (i, 0))] * 8,
        out_shape=[jax.ShapeDtypeStruct((T, D_A), f32)] * 8,
        compiler_params=_params(("parallel",)),
        name="even_mid_prompt",
    )(h_all, h_all, z_all, z_all, *plist)


def _even_mid_sample(h_all, hs, z_all, zs, buf_t, plist, *, row0, n):
    in_specs = [pl.BlockSpec((n, D_MODEL), lambda i: (row0, 0)),
                pl.BlockSpec((n, D_MODEL), lambda i: (0, 0)),
                pl.BlockSpec((n, 4 * D_A), lambda i: (row0, 0)),
                pl.BlockSpec((n, 3 * D_A), lambda i: (0, 0)),
                pl.BlockSpec((POOL_BUF, n, D_POOL), lambda i: (0, 0, 0))]
    in_specs += [_const_spec(p.shape) for p in plist]
    return pl.pallas_call(
        _even_mid_sample_kernel, grid=(1,), in_specs=in_specs,
        out_specs=[pl.BlockSpec((n, D_A), lambda i: (0, 0))] * 8,
        out_shape=[jax.ShapeDtypeStruct((n, D_A), f32)] * 8,
        compiler_params=_params(("arbitrary",)),
        name="even_mid_sample",
    )(h_all, hs, z_all, zs, buf_t, *plist)


def _wkv_prompt_kernel(r_ref, lw_ref, k_ref, v_ref, kk_ref, a_ref, o_ref, s_out_ref, S_ref, *, tb, npair):
    C = WKV_CHUNK
    t = pl.program_id(1)

    @pl.when(t == 0)
    def _():
        S_ref[...] = jnp.zeros_like(S_ref)

    ri = lax.broadcasted_iota(jnp.int32, (C, C), 0)
    ci = lax.broadcasted_iota(jnp.int32, (C, C), 1)
    strict = ri > ci
    incl = ri >= ci
    eye = (ri == ci).astype(f32)
    rows = lax.broadcasted_iota(jnp.int32, (C, 2 * HD_A), 0)

    nh = 2 * npair
    heads = range(nh)

    def prep(sl, lanes):
        r, lw, k, v, kk, a = (x[sl, lanes] for x in (r_ref, lw_ref, k_ref, v_ref, kk_ref, a_ref))
        cum = lw
        sh = 1
        while sh < C:
            cum = cum + jnp.where(rows >= sh, pltpu.roll(cum, sh, axis=0), 0.0)
            sh *= 2
        cum_c = cum[C - 1:C, :]
        b = kk * a
        e_neg = jnp.exp(-cum)
        e_rem = jnp.exp(cum_c - cum)
        return dict(at=-kk * jnp.exp(cum - lw), rt=r * jnp.exp(cum), bt=b * e_neg, kt=k * e_neg,
                    bh=b * e_rem, kh=k * e_rem, w_c=jnp.exp(cum_c), v=v)

    def chunk(c, carry):
        off = pl.multiple_of(c * C, C)
        sl = pl.ds(off, C)
        pairs = [prep(sl, slice(p * 2 * HD_A, (p + 1) * 2 * HD_A)) for p in range(npair)]
        S0 = [S_ref[h] for h in heads]

        def head(name, h):
            return pairs[h // 2][name][:, (h % 2) * HD_A:(h % 2 + 1) * HD_A]

        ar = [jnp.concatenate([head("at", h), head("rt", h)], axis=0) for h in heads]
        bk = [jnp.concatenate([head("bt", h), head("kt", h)], axis=0) for h in heads]
        vh = [head("v", h) for h in heads]
        m4 = [_dot3(ar[h], bk[h], _dot_nt) for h in heads]
        sar = [_dot3(ar[h], S0[h], _dot_nt) for h in heads]
        a_ab = [jnp.where(strict, m4[h][:C, :C], 0.0) for h in heads]
        a_ak = [jnp.where(strict, m4[h][:C, C:], 0.0) for h in heads]
        a_rb = [jnp.where(incl, m4[h][C:, :C], 0.0) for h in heads]
        a_rk = [jnp.where(incl, m4[h][C:, C:], 0.0) for h in heads]
        av = [_dot3(jnp.concatenate([a_ak[h], a_rk[h]], axis=0), vh[h]) for h in heads]
        tinv = [eye + a_ab[h] for h in heads]
        pw = [_dot3(a_ab[h], a_ab[h]) for h in heads]
        n_sq = 2
        while n_sq * 2 < C:
            x = [_dot3(pw[h], jnp.concatenate([tinv[h], pw[h]], axis=1)) for h in heads]
            tinv = [tinv[h] + x[h][:, :C] for h in heads]
            pw = [x[h][:, C:] for h in heads]
            n_sq *= 2
        tinv = [tinv[h] + _dot3(pw[h], tinv[h]) for h in heads]
        u = [_dot3(tinv[h], sar[h][:C] + av[h][:C]) for h in heads]
        o = [sar[h][C:] + av[h][C:] + _dot3(a_rb[h], u[h]) for h in heads]
        s_new = []
        for h in heads:
            uv = jnp.concatenate([u[h], vh[h]], axis=0)
            bkh = jnp.concatenate([head("bh", h), head("kh", h)], axis=0)
            s_new.append(S0[h] * head("w_c", h) + _dot3(uv, bkh, _dot_tn))
        for h in heads:
            S_ref[h] = s_new[h]
        for p in range(npair):
            o_ref[sl, p * 2 * HD_A:(p + 1) * 2 * HD_A] = jnp.concatenate([o[2 * p], o[2 * p + 1]], axis=1)
        return carry

    lax.fori_loop(0, tb // C, chunk, 0)

    @pl.when(t == pl.num_programs(1) - 1)
    def _():
        s_out_ref[...] = S_ref[...]


def _wkv_prompt(r, lw, k2, v, kkn, a, *, T, tb, npair):
    spec = pl.BlockSpec((tb, 2 * HD_A * npair), lambda j, t: (t, j))
    nh = 2 * npair
    return pl.pallas_call(
        functools.partial(_wkv_prompt_kernel, tb=tb, npair=npair),
        grid=(H_A // nh, T // tb),
        in_specs=[spec] * 6,
        out_specs=[spec, pl.BlockSpec((nh, HD_A, HD_A), lambda j, t: (j, 0, 0))],
        out_shape=[jax.ShapeDtypeStruct((T, D_A), f32), jax.ShapeDtypeStruct((H_A, HD_A, HD_A), f32)],
        scratch_shapes=[pltpu.VMEM((nh, HD_A, HD_A), f32)],
        compiler_params=_params(("parallel", "arbitrary")),
        name="wkv_prompt",
    )(r, lw, k2, v, kkn, a)


def _wkv_sample_kernel(s_ref, r_ref, lw_ref, k_ref, kk_ref, a_ref, v_ref, o_ref, s_out_ref):
    S = s_ref[0]
    kk = kk_ref[0]
    sa = jnp.sum(S * (-kk), axis=-1, keepdims=True)
    S2 = S * jnp.exp(lw_ref[0]) + sa * (kk * a_ref[0]) + v_ref[0] * k_ref[0]
    s_out_ref[0] = S2
    o_ref[0] = jnp.sum(S2 * r_ref[0], axis=-1, keepdims=True)


def _wkv_sample(S0, r, lw, k2, v, kkn, a):
    n = S0.shape[0]
    row = lambda x: x.reshape(n, H_A, 1, HD_A)
    rspec = pl.BlockSpec((1, H_A, 1, HD_A), lambda b: (b, 0, 0, 0))
    cspec = pl.BlockSpec((1, H_A, HD_A, 1), lambda b: (b, 0, 0, 0))
    sspec = pl.BlockSpec((1, H_A, HD_A, HD_A), lambda b: (b, 0, 0, 0))
    o, S = pl.pallas_call(
        _wkv_sample_kernel, grid=(n,),
        in_specs=[sspec, rspec, rspec, rspec, rspec, rspec, cspec],
        out_specs=[cspec, sspec],
        out_shape=[jax.ShapeDtypeStruct((n, H_A, HD_A, 1), f32), jax.ShapeDtypeStruct(S0.shape, f32)],
        compiler_params=_params(("parallel",)),
        name="wkv_sample",
    )(S0, row(r), row(lw), row(k2), row(kkn), row(a), v.reshape(n, H_A, HD_A, 1))
    return o.reshape(n, D_A), S


def _even_post_kernel(x_ref, o_ref, r_ref, k_ref, v_ref, g_ref, ob_ref, wo_ref, rk_ref, gnw_ref, gnb_ref, bd_ref,
                      out_ref):
    bd = bd_ref[...]
    o = o_ref[...]
    inv = 1.0 / HD_A
    mu = _head_sum(o, bd) * inv
    d = o - mu
    var = _head_sum(d * d, bd) * inv
    on = d * lax.rsqrt(var + GN_EPS) * gnw_ref[...] + gnb_ref[...]
    v = v_ref[...]
    bonus = _head_sum(r_ref[...] * k_ref[...] * rk_ref[...], bd) * v
    oa = ((on + bonus) * g_ref[...]).astype(bf16)
    y = _dot(oa, wo_ref[:D_A, :]) + _dot(ob_ref[...].astype(bf16), wo_ref[D_A:, :])
    out_ref[...] = x_ref[...] + y


def _even_post(x_all, o, r, k2, v, g, ob, wo, rk, gnw, gnb, bd, *, row0, rows, tm):
    a_spec = pl.BlockSpec((tm, D_A), lambda i: (i, 0))
    x_spec = pl.BlockSpec((tm, D_MODEL), lambda i: (i + row0, 0))
    consts = [wo, rk, gnw, gnb, bd]
    return pl.pallas_call(
        _even_post_kernel, grid=(rows // tm,),
        in_specs=[x_spec] + [a_spec] * 6 + [_const_spec(c.shape) for c in consts],
        out_specs=x_spec,
        out_shape=jax.ShapeDtypeStruct(x_all.shape, f32),
        input_output_aliases={0: 0},
        compiler_params=_params(("parallel",)),
        name="even_post",
    )(x_all, o, r, k2, v, g, ob, *consts)


def _attn_prompt_kernel(*refs):
    in_refs, out_ref, og_ref, lg_ref = refs[:15], refs[15], refs[16], refs[17]
    n = pl.program_id(1)
    has_prev = n > 0
    scale = HD_C ** -0.5
    Q = 128
    ri = lax.broadcasted_iota(jnp.int32, (Q, Q), 0)
    ci = lax.broadcasted_iota(jnp.int32, (Q, Q), 1)
    mask_prev0 = ci >= ri
    mask_cur = ci <= ri

    for g, (_, dil) in enumerate(ATTN_GROUPS):
        q_ref, kp_ref, k_ref, vp_ref, v_ref = in_refs[5 * g:5 * g + 5]
        nblk = ATTN_BLK // (Q * dil)

        def units(it, carry, q_ref=q_ref, kp_ref=kp_ref, k_ref=k_ref, vp_ref=vp_ref, v_ref=v_ref,
                  dil=dil, nblk=nblk, g=g):
            us = [it * ATTN_UNROLL + k for k in range(ATTN_UNROLL)]
            rows, q, kc, vc, kp, vp, valid_prev = [], [], [], [], [], [], []
            for u in us:
                c = u // nblk
                m = u % nblk
                r = pl.ds(m * (Q * dil) + c, Q, stride=dil)
                in_blk = m > 0
                r_a = pl.ds(jnp.maximum(m - 1, 0) * (Q * dil) + c, Q, stride=dil)
                r_b = pl.ds((nblk - 1) * (Q * dil) + c, Q, stride=dil)
                rows.append(r)
                q.append(q_ref[r, :].astype(bf16))
                kc.append(k_ref[r, :].astype(bf16))
                vc.append(v_ref[r, :].astype(bf16))
                kp.append(jnp.where(in_blk, k_ref[r_a, :], kp_ref[r_b, :]).astype(bf16))
                vp.append(jnp.where(in_blk, v_ref[r_a, :], vp_ref[r_b, :]).astype(bf16))
                valid_prev.append(jnp.logical_or(in_blk, has_prev))
            n = range(ATTN_UNROLL)
            s_p = [_dot_nt(q[i], kp[i]) * scale for i in n]
            s_c = [_dot_nt(q[i], kc[i]) * scale for i in n]
            s_p = [jnp.where(jnp.logical_and(mask_prev0, valid_prev[i]), s_p[i], NEG_INF) for i in n]
            s_c = [jnp.where(mask_cur, s_c[i], NEG_INF) for i in n]
            mx = [jnp.maximum(jnp.max(s_p[i], axis=1, keepdims=True), jnp.max(s_c[i], axis=1, keepdims=True))
                  for i in n]
            p_p = [jnp.exp(s_p[i] - mx[i]) for i in n]
            p_c = [jnp.exp(s_c[i] - mx[i]) for i in n]
            den = [jnp.sum(p_p[i], axis=1, keepdims=True) + jnp.sum(p_c[i], axis=1, keepdims=True) for i in n]
            o = [(_dot(p_p[i].astype(bf16), vp[i]) + _dot(p_c[i].astype(bf16), vc[i])) / den[i] for i in n]
            lse = [mx[i] + jnp.log(den[i]) for i in n]
            for i in n:
                og_ref[g, rows[i], :] = o[i]
                lg_ref[g, rows[i], :] = jnp.broadcast_to(lse[i], (Q, HD_C))
            return carry

        lax.fori_loop(0, dil * nblk // ATTN_UNROLL, units, 0)

    l0, l1, l2 = lg_ref[0], lg_ref[1], lg_ref[2]
    mx = jnp.maximum(jnp.maximum(l0, l1), l2)
    e0, e1, e2 = jnp.exp(l0 - mx), jnp.exp(l1 - mx), jnp.exp(l2 - mx)
    out_ref[...] = (e0 * og_ref[0] + e1 * og_ref[1] + e2 * og_ref[2]) / (e0 + e1 + e2)


def _attn_prompt(z, *, T):
    nb = T // ATTN_BLK
    in_specs = []
    for g in range(len(ATTN_GROUPS)):
        def col(j, g=g):
            return lambda h, n: (n, g * 3 * H_C + j * H_C + h)

        def col_prev(j, g=g):
            return lambda h, n: (jnp.maximum(n - 1, 0), g * 3 * H_C + j * H_C + h)
        blk = (ATTN_BLK, HD_C)
        in_specs += [pl.BlockSpec(blk, col(0)), pl.BlockSpec(blk, col_prev(1)), pl.BlockSpec(blk, col(1)),
                     pl.BlockSpec(blk, col_prev(2)), pl.BlockSpec(blk, col(2))]
    return pl.pallas_call(
        _attn_prompt_kernel, grid=(H_C, nb),
        in_specs=in_specs,
        out_specs=pl.BlockSpec((ATTN_BLK, HD_C), lambda h, n: (n, h)),
        out_shape=jax.ShapeDtypeStruct((T, D_C), f32),
        scratch_shapes=[pltpu.VMEM((3, ATTN_BLK, HD_C), f32), pltpu.VMEM((3, ATTN_BLK, HD_C), f32)],
        compiler_params=_params(("parallel", "arbitrary")),
        name="attn_prompt",
    )(*([z] * 15))


ATTN_SAMPLE_NB = 4


def _attn_sample_kernel(z_ref, c0_ref, c1_ref, c2_ref, out_ref):
    scale = HD_C ** -0.5
    crefs = (c0_ref, c1_ref, c2_ref)
    for b in range(ATTN_SAMPLE_NB):
        for h in range(H_C):
            outs, lses = [], []
            for g in range(len(ATTN_GROUPS)):
                base = g * 3 * D_C + h * HD_C
                q = z_ref[b, :, base:base + HD_C]
                kn = z_ref[b, :, base + D_C:base + D_C + HD_C]
                vn = z_ref[b, :, base + 2 * D_C:base + 2 * D_C + HD_C]
                K = crefs[g][b, :, 0, 0, h, :]
                V = crefs[g][b, :, 0, 1, h, :]
                s_c = jnp.sum(K * q, axis=1, keepdims=True) * scale
                s_n = jnp.sum(kn * q, axis=1, keepdims=True) * scale
                mx = jnp.maximum(jnp.max(s_c, axis=0, keepdims=True), s_n)
                p_c = jnp.exp(s_c - mx)
                p_n = jnp.exp(s_n - mx)
                den = jnp.sum(p_c, axis=0, keepdims=True) + p_n
                outs.append((jnp.sum(p_c * V, axis=0, keepdims=True) + p_n * vn) / den)
                lses.append(mx + jnp.log(den))
            mx = jnp.maximum(jnp.maximum(lses[0], lses[1]), lses[2])
            es = [jnp.exp(l - mx) for l in lses]
            out_ref[b, :, h * HD_C:(h + 1) * HD_C] = (es[0] * outs[0] + es[1] * outs[1] + es[2] * outs[2]) / (
                es[0] + es[1] + es[2])


def _attn_sample(z_s, caches):
    n = z_s.shape[0]
    nb = ATTN_SAMPLE_NB
    assert n % nb == 0
    in_specs = [pl.BlockSpec((nb, 1, 9 * D_C), lambda b: (b, 0, 0))]
    args = [z_s.reshape(n, 1, 9 * D_C)]
    for (win, dil), c in zip(ATTN_GROUPS, caches):
        assert c.shape[1] == win
        args.append(c.reshape(n, win // dil, dil, 2, H_C, HD_C))
        in_specs.append(pl.BlockSpec((nb, win // dil, 1, 2, H_C, HD_C), lambda b: (b, 0, 0, 0, 0, 0)))
    out = pl.pallas_call(
        _attn_sample_kernel, grid=(n // nb,), in_specs=in_specs,
        out_specs=pl.BlockSpec((nb, 1, D_C), lambda b: (b, 0, 0)),
        out_shape=jax.ShapeDtypeStruct((n, 1, D_C), f32),
        compiler_params=_params(("parallel",)),
        name="attn_sample",
    )(*args)
    return out.reshape(n, D_C)


def _top16_rows(s):
    rows = lax.broadcasted_iota(jnp.int32, s.shape, 0)
    work = s
    vals = []
    for _ in range(PEER_TOPK):
        m = jnp.max(work, axis=0, keepdims=True)
        idx = jnp.min(jnp.where(work == m, rows, s.shape[0]), axis=0, keepdims=True)
        work = jnp.where(rows == idx, NEG_INF, work)
        vals.append(m)
    return vals, jnp.logical_and(work == NEG_INF, s != NEG_INF)


def _oddeven_mergesort_pairs(n):
    pairs = []

    def merge(lo, hi, r):
        step = r * 2
        if step < hi - lo:
            merge(lo, hi, step)
            merge(lo + r, hi, step)
            pairs.extend((i, i + r) for i in range(lo + r, hi - r, step))
        else:
            pairs.append((lo, lo + r))

    def sort(lo, hi):
        if hi - lo >= 1:
            mid = lo + (hi - lo) // 2
            sort(lo, mid)
            sort(mid + 1, hi)
            merge(lo, hi, 1)

    sort(0, n - 1)
    return pairs


SUBLANES = 8


def _top16_sorted(s):
    assert s.shape[0] == PEER_TOPK * SUBLANES
    t = [s[SUBLANES * k:SUBLANES * (k + 1), :] for k in range(PEER_TOPK)]
    for i, j in _oddeven_mergesort_pairs(PEER_TOPK):
        t[i], t[j] = jnp.maximum(t[i], t[j]), jnp.minimum(t[i], t[j])
    shift = 1
    while shift < SUBLANES:
        other = [pltpu.roll(x, shift, axis=0) for x in t]
        t = [jnp.maximum(t[i], other[PEER_TOPK - 1 - i]) for i in range(PEER_TOPK)]
        d = PEER_TOPK // 2
        while d >= 1:
            for i in range(PEER_TOPK):
                if i & d == 0:
                    t[i], t[i + d] = jnp.maximum(t[i], t[i + d]), jnp.minimum(t[i], t[i + d])
            d //= 2
        shift *= 2
    thr = t[PEER_TOPK - 1][0:1, :]
    member = s >= thr
    count = jnp.sum(member.astype(f32), axis=0, keepdims=True)
    return [x[0:1, :] for x in t], member, count == float(PEER_TOPK)


def _top16_rows_distinct(ss):
    work = list(ss)
    vals = [[] for _ in ss]
    for _ in range(PEER_TOPK):
        for i in range(len(ss)):
            m = jnp.max(work[i], axis=0, keepdims=True)
            work[i] = jnp.where(work[i] == m, NEG_INF, work[i])
            vals[i].append(m)
    out = []
    for s, w, v in zip(ss, work, vals):
        member = jnp.logical_and(w == NEG_INF, s != NEG_INF)
        count = jnp.sum(member.astype(f32), axis=0, keepdims=True)
        out.append((v, member, count == float(PEER_TOPK)))
    return out


def _peer_route_kernel(q_ref, keys_ref, s1_ref, s2_ref, tau_ref):
    n = q_ref.shape[0]

    def route_head(h, exact):
        if exact:
            top16 = lambda ss: [_top16_rows(s) + (None,) for s in ss]
            top16_scores = top16
        else:
            top16 = _top16_rows_distinct
            top16_scores = lambda ss: [_top16_sorted(s) for s in ss]
        scores = []
        for p in range(2):
            hp = 2 * h + p
            qs = q_ref[:, hp * N_KEYS:(hp + 1) * N_KEYS].astype(bf16)
            scores.append(_dot_nt(keys_ref[hp], qs) * LOG2E)
        sm, vals, oks = [], [], []
        for s, (v, member, ok) in zip(scores, top16_scores(scores)):
            sm.append(jnp.where(member, s, NEG_INF))
            vals.append(v)
            oks.append(ok)

        def pair_sums(z1, z2):
            pieces = []
            for r1 in range(PEER_TOPK):
                cnt = PEER_TOPK // (r1 + 1)
                pieces.append(z1[r1] + jnp.concatenate(z2[:cnt], axis=0))
            npad = (-sum(p.shape[0] for p in pieces)) % 8
            pieces.append(jnp.full((npad, n), NEG_INF, f32))
            return jnp.concatenate(pieces, axis=0)

        z1 = [v - vals[0][0] for v in vals[0]]
        z2 = [v - vals[1][0] for v in vals[1]]
        (top, chosen, ok), = top16([pair_sums(z1, z2)])
        oks.append(ok)
        log_z = jnp.log2(sum(jnp.exp2(t) for t in top))
        z1 = [z - log_z for z in z1]
        sums = pair_sums(z1, z2)
        if exact:
            (top, _, _), = top16([sums])
            tau = top[-1]
        else:
            tau = jnp.min(jnp.where(chosen, sums, jnp.inf), axis=0, keepdims=True)
            count = jnp.sum((sums >= tau).astype(f32), axis=0, keepdims=True)
            oks.append(count == float(PEER_TOPK))
        rows = slice(h * N_KEYS, (h + 1) * N_KEYS)
        l1 = (sm[0] - vals[0][0]) - log_z
        for t in range(N_KEYS // SUBLANES):
            s1_ref[t, h * SUBLANES:(h + 1) * SUBLANES, :] = l1[t * SUBLANES:(t + 1) * SUBLANES, :]
        s2_ref[rows, :] = sm[1] - vals[1][0]
        tau_ref[h:h + 1, :] = tau
        if exact:
            return None
        all_ok = functools.reduce(jnp.logical_and, oks)
        return jnp.min(all_ok.astype(f32)) > 0.5

    tie_free = [route_head(h, exact=False) for h in range(PEER_HEADS)]
    for h in range(PEER_HEADS):
        pl.when(jnp.logical_not(tie_free[h]))(functools.partial(route_head, h, True))


def _peer_route(q, keys, *, tmr):
    M = q.shape[0]
    big = jax.ShapeDtypeStruct((PEER_HEADS * N_KEYS, M), f32)
    bspec = pl.BlockSpec((PEER_HEADS * N_KEYS, tmr), lambda i: (0, i))
    n_tiles, tile_rows = N_KEYS // SUBLANES, PEER_HEADS * SUBLANES
    return pl.pallas_call(
        _peer_route_kernel, grid=(M // tmr,),
        in_specs=[pl.BlockSpec((tmr, 2 * PEER_HEADS * N_KEYS), lambda i: (i, 0)),
                  _const_spec(keys.shape)],
        out_specs=[pl.BlockSpec((n_tiles, tile_rows, tmr), lambda i: (0, 0, i)), bspec,
                   pl.BlockSpec((PEER_HEADS, tmr), lambda i: (0, i))],
        out_shape=[jax.ShapeDtypeStruct((n_tiles, tile_rows, M), f32), big,
                   jax.ShapeDtypeStruct((PEER_HEADS, M), f32)],
        compiler_params=_params(("parallel",)),
        name="peer_route",
    )(q, keys)


def _peer_prep_kernel(u_ref, v_ref, ub_ref, vt_ref):
    ub_ref[...] = u_ref[...].astype(bf16)
    vt_ref[...] = v_ref[...].T.astype(bf16)


def _peer_prep(u_tab, v_tab, *, te):
    L, E, D = u_tab.shape
    return pl.pallas_call(
        _peer_prep_kernel, grid=(L, E // te),
        in_specs=[pl.BlockSpec((None, te, D), lambda l, j: (l, j, 0))] * 2,
        out_specs=[pl.BlockSpec((None, te, D), lambda l, j: (l, j, 0)),
                   pl.BlockSpec((None, D, te), lambda l, j: (l, 0, j))],
        out_shape=[jax.ShapeDtypeStruct((L, E, D), bf16), jax.ShapeDtypeStruct((L, D, E), bf16)],
        compiler_params=_params(("parallel", "parallel")),
        name="peer_prep",
    )(u_tab, v_tab)


PEER_RB = 32
PEER_LC = 256
PEER_I1 = 4


def _peer_dense_kernel(xt_ref, u_ref, vt_ref, s1_ref, s2_ref, tau_ref, res_ref, out_ref, acc_ref, *, tm, te):
    j = pl.program_id(1)
    n_i1 = te // N_KEYS
    n_rb = N_KEYS // PEER_RB

    @pl.when(j == 0)
    def _():
        acc_ref[...] = jnp.zeros_like(acc_ref)

    chunks = [slice(c * PEER_LC, (c + 1) * PEER_LC) for c in range(tm // PEER_LC)]
    assert n_i1 == SUBLANES
    act = _dot(u_ref[...], xt_ref[...])
    cols = []
    for lanes in chunks:
        s1 = [s1_ref[h * SUBLANES:(h + 1) * SUBLANES, lanes] for h in range(PEER_HEADS)]
        pieces = [[None] * n_rb for _ in range(n_i1)]
        for rb in range(n_rb):
            for i0 in range(0, n_i1, PEER_I1):
                group = range(i0, i0 + PEER_I1)
                gates = {ii: jnp.zeros((PEER_RB, PEER_LC), f32) for ii in group}
                for h in range(PEER_HEADS):
                    s2 = s2_ref[h * N_KEYS + rb * PEER_RB:h * N_KEYS + (rb + 1) * PEER_RB, lanes]
                    tau = tau_ref[h:h + 1, lanes]
                    for ii in group:
                        c = s2 + s1[h][ii:ii + 1, :]
                        gates[ii] = gates[ii] + jnp.where(c >= tau, jnp.exp2(c), 0.0)
                for ii in group:
                    a = act[ii * N_KEYS + rb * PEER_RB:ii * N_KEYS + (rb + 1) * PEER_RB, lanes]
                    pieces[ii][rb] = (_gelu(a) * gates[ii]).astype(bf16)
        cols.append(jnp.concatenate([pc for row in pieces for pc in row], axis=0))
    acc_ref[...] += _dot(vt_ref[...], jnp.concatenate(cols, axis=1))

    @pl.when(j == pl.num_programs(1) - 1)
    def _():
        out_ref[...] = acc_ref[...].T + res_ref[...]


def _peer_dense(xt, ub, vt, layer, s1, s2, tau, res, *, tm, te):
    D, M = xt.shape
    E = ub.shape[1]
    once = dict(pipeline_mode=pl.Buffered(1))
    return pl.pallas_call(
        functools.partial(_peer_dense_kernel, tm=tm, te=te),
        grid=(M // tm, E // te),
        in_specs=[pl.BlockSpec((D, tm), lambda i, j: (0, i)),
                  pl.BlockSpec((None, te, D), lambda i, j: (layer, j, 0)),
                  pl.BlockSpec((None, D, te), lambda i, j: (layer, 0, j)),
                  pl.BlockSpec((None, PEER_HEADS * SUBLANES, tm), lambda i, j: (j, 0, i)),
                  pl.BlockSpec((PEER_HEADS * N_KEYS, tm), lambda i, j: (0, i)),
                  pl.BlockSpec((PEER_HEADS, tm), lambda i, j: (0, i)),
                  pl.BlockSpec((tm, D), lambda i, j: (i, 0), **once)],
        out_specs=pl.BlockSpec((tm, D), lambda i, j: (i, 0), **once),
        out_shape=jax.ShapeDtypeStruct((M, D), f32),
        scratch_shapes=[pltpu.VMEM((D, tm), f32)],
        compiler_params=_params(("parallel", "arbitrary")),
        name="peer_dense",
    )(xt, ub, vt, s1, s2, tau, res)


def _peer(x_all, norm_w, wq, keys, ub, vt, layer, *, tm_mm, tmr, tm, te):
    q, ht = _mm(x_all, wq, norm_w=norm_w, emit_h=bf16, transpose_h=True, tm=tm_mm, tn=wq.shape[1],
                name="peer_query")
    s1, s2, tau = _peer_route(q, keys, tmr=tmr)
    return _peer_dense(ht, ub, vt, layer, s1, s2, tau, x_all, tm=tm, te=te)


LANE = 128


def _block_diag_ones():
    head_of_lane = jnp.arange(D_A) // HD_A
    return (jnp.arange(LANE)[:, None] == head_of_lane[None, :]).astype(bf16)


def _pad_lanes(w):
    return jnp.pad(w, ((0, 0), (0, (-w.shape[1]) % LANE)))


def _pad_rows(w):
    return jnp.pad(w, ((0, (-w.shape[0]) % LANE), (0, 0)))


def kernel(x_prompt, x_sample, state_shift, state_wkv, state_pool, cache_kv_w128, cache_kv_w512, cache_kv_w2048,
           norm_mix, norm_ffn, norm_final, a_w_in, a_w_out, a_mu_rkv, a_mu_wag, a_w0, a_w1, a_w2, a_a0, a_a1,
           a_a2, a_g1, a_g2, a_k_k, a_k_a, a_r_k, a_gn_w, a_gn_b, b_w_pool, b_scale, c_w_in, c_w_out, p_w_q,
           p_sub_keys, p_u, p_v):
    T = x_prompt.shape[1]
    NS = x_sample.shape[0]
    TM = 768
    M = -(-(T + NS) // TM) * TM
    S_BLK = T // NS
    assert T % NS == 0 and T % ATTN_BLK == 0
    cb = lambda w: w.astype(bf16)
    row = lambda w: w.reshape(1, -1)

    x0 = jnp.concatenate([x_prompt.reshape(T, D_MODEL), x_sample.reshape(NS, D_MODEL),
                          jnp.zeros((M - T - NS, D_MODEL), f32)], axis=0)
    ub, vt = _peer_prep(p_u, p_v, te=512)
    bd = _block_diag_ones()

    w_in = cb(a_w_in[0])
    z0, h0 = _mm(x0, w_in, norm_w=norm_mix[0], emit_h=f32, tm=TM, tn=1024, name="even_in_proj")
    zs_prev = _mm(state_shift[0], w_in[:, :3 * D_A], tm=NS, tn=1024, name="even_in_proj_state")
    plist = [a_mu_wag[0], row(a_mu_rkv[0]), row(a_w0[0]), _pad_lanes(cb(a_w1[0])), _pad_rows(cb(a_w2[0])),
             row(a_a0[0]), _pad_lanes(cb(a_a1[0])), _pad_rows(cb(a_a2[0])), cb(a_g1[0]), cb(a_g2[0]),
             row(a_k_k[0]), row(a_k_a[0]), cb(b_w_pool[0]), row(b_scale[0]), bd]
    rp, lwp, kp, vp, kkp, ap, gp, obp = _even_mid_prompt(h0, z0, plist, T=T, tm=256)
    rs, lws, ks, vs, kks, as_, gs, obs = _even_mid_sample(
        h0, state_shift[0], z0, zs_prev, jnp.swapaxes(state_pool[0], 0, 1), plist, row0=S_BLK, n=NS)
    o_p, wkv_p = _wkv_prompt(rp, lwp, kp, vp, kkp, ap, T=T, tb=512, npair=8)
    o_s, wkv_s = _wkv_sample(state_wkv[0], rs, lws, ks, vs, kks, as_)
    post_c = (cb(a_w_out[0]), row(a_r_k[0]), row(a_gn_w[0]), row(a_gn_b[0]), bd)
    x1 = _even_post(x0, o_p, rp, kp, vp, gp, obp, *post_c, row0=0, rows=T, tm=256)
    x1 = _even_post(x1, o_s, rs, ks, vs, gs, obs, *post_c, row0=S_BLK, rows=NS, tm=NS)
    x1 = _peer(x1, norm_ffn[0], cb(p_w_q[0]), cb(p_sub_keys[0].reshape(2 * PEER_HEADS, N_KEYS, N_KEYS)),
               ub, vt, 0, tm_mm=TM, tmr=256, tm=TM, te=1024)

    z1 = _mm(x1, cb(c_w_in[0]), norm_w=norm_mix[1], tm=TM, tn=1024, name="odd_in_proj")
    att_p = _attn_prompt(z1, T=T)
    z1s = z1[T:T + NS]
    att_s = _attn_sample(z1s, (cache_kv_w128[0], cache_kv_w512[0], cache_kv_w2048[0]))
    att = jnp.concatenate([att_p, att_s, jnp.zeros((M - T - NS, D_C), f32)], axis=0)
    x2 = _mm(att, cb(c_w_out[0]), res=x1, tm=TM, tn=D_MODEL, name="odd_out_proj")
    x2 = _peer(x2, norm_ffn[1], cb(p_w_q[1]), cb(p_sub_keys[1].reshape(2 * PEER_HEADS, N_KEYS, N_KEYS)),
               ub, vt, 1, tm_mm=TM, tmr=256, tm=TM, te=1024)

    y_p = _rmsnorm(x2, norm_final, tm=512, row0=0, rows=T)
    y_s = _rmsnorm(x2, norm_final, tm=NS, row0=S_BLK, rows=NS)

    u_p = z0[T - POOL_BUF:T, 3 * D_A:]
    u_s = z0[T:T + NS, 3 * D_A:]
    pool_s = jnp.concatenate([state_pool[0][:, 1:], u_s[:, None, :]], axis=1)
    kv_p, kv_s = [], []
    for g, (win, _) in enumerate(ATTN_GROUPS):
        n = min(win, T)
        kv = z1[T - n:T + NS, g * 3 * D_C + D_C:g * 3 * D_C + 3 * D_C].reshape(n + NS, 2, H_C, HD_C)
        kv_p.append(kv[:n][None, None])
        kv_s.append(kv[n:][None, :, None])
    return (y_p[None], y_s[:, None, :],
            h0[T - 1][None, None], h0[T:T + NS][None],
            wkv_p[None, None], wkv_s[None],
            u_p[None, None], pool_s[None],
            kv_p[0], kv_s[0], kv_p[1], kv_s[1], kv_p[2], kv_s[2])
```

```python
import functools
import math

import jax
import jax.numpy as jnp
from jax import lax
from jax.experimental import pallas as pl
from jax.experimental.pallas import tpu as pltpu

f32 = jnp.float32
bf16 = jnp.bfloat16

D_MODEL = 2048
H_A, HD_A = 16, 64
D_A = H_A * HD_A
D_POOL = D_MODEL - D_A
POOL_WINDOWS = (2, 4, 8, 16)
POOL_GD = D_POOL // len(POOL_WINDOWS)
POOL_BUF = max(POOL_WINDOWS) - 1
GN_EPS = 64e-5
NORM_EPS = 1e-6
ATTN_GROUPS = ((128, 1), (512, 4), (2048, 16))
H_C, HD_C = 8, 128
D_C = H_C * HD_C
N_KEYS = 128
N_EXPERTS = N_KEYS * N_KEYS
PEER_HEADS = 8
PEER_TOPK = 16

VMEM_LIMIT = 56 * 1024 * 1024
WKV_CHUNK = 64
ATTN_BLK = 2048
ATTN_UNROLL = 16
NEG_INF = float("-inf")
LOG2E = math.log2(math.e)


def _params(sem, vmem=VMEM_LIMIT, flags=None):
    return pltpu.CompilerParams(dimension_semantics=sem, vmem_limit_bytes=vmem, flags=flags)


def _dot(a, b):
    return jnp.dot(a, b, preferred_element_type=f32)


def _dot_nt(a, b):
    return lax.dot_general(a, b, (((1,), (1,)), ((), ())), preferred_element_type=f32)


def _dot_tn(a, b):
    return lax.dot_general(a, b, (((0,), (0,)), ((), ())), preferred_element_type=f32)


def _split(x):
    hi = x.astype(bf16)
    lo = (x - hi.astype(f32)).astype(bf16)
    return hi, lo


def _dot3(a, b, dot=_dot):
    ah, al = _split(a)
    bh, bl = _split(b)
    return dot(ah, bh) + (dot(ah, bl) + dot(al, bh))


def _head_sum(x, bd):
    hi, lo = _split(x)
    sh, sl = _split(_dot_nt(hi, bd) + _dot_nt(lo, bd))
    return _dot(sh, bd) + _dot(sl, bd)


def _gelu(x):
    return 0.5 * x * (1.0 + lax.erf(x * 0.7071067811865476))


def _mm_kernel(*refs, has_norm, has_res, emit_h, transpose_h):
    it = iter(refs)
    x_ref, w_ref = next(it), next(it)
    g_ref = next(it) if has_norm else None
    res_ref = next(it) if has_res else None
    o_ref = next(it)
    h_ref = next(it) if emit_h else None
    xb_ref = next(it)

    @pl.when(pl.program_id(1) == 0)
    def _():
        x = x_ref[...]
        if has_norm:
            ms = jnp.mean(x * x, axis=-1, keepdims=True)
            x = (x * lax.rsqrt(ms + NORM_EPS)) * g_ref[...]
            if emit_h:
                h_ref[...] = (x.T if transpose_h else x).astype(h_ref.dtype)
        xb_ref[...] = x.astype(bf16)

    acc = _dot(xb_ref[...], w_ref[...])
    if has_res:
        acc = acc + res_ref[...]
    o_ref[...] = acc


def _mm(x, w, *, norm_w=None, res=None, emit_h=None, transpose_h=False, tm, tn, row0=0, rows=None, name="proj"):
    K = x.shape[1]
    N = w.shape[1]
    rows = x.shape[0] if rows is None else rows
    assert rows % tm == 0 and N % tn == 0
    w_mode = dict(pipeline_mode=pl.Buffered(1)) if tn == N else {}
    in_specs = [pl.BlockSpec((tm, K), lambda i, j: (i + row0, 0)),
                pl.BlockSpec((K, tn), lambda i, j: (0, j), **w_mode)]
    args = [x, w]
    if norm_w is not None:
        in_specs.append(pl.BlockSpec((1, K), lambda i, j: (0, 0)))
        args.append(norm_w.reshape(1, K))
    if res is not None:
        in_specs.append(pl.BlockSpec((tm, tn), lambda i, j: (i + row0, j)))
        args.append(res)
    out_shape = [jax.ShapeDtypeStruct((rows, N), f32)]
    out_specs = [pl.BlockSpec((tm, tn), lambda i, j: (i, j))]
    if emit_h is not None and transpose_h:
        out_shape.append(jax.ShapeDtypeStruct((K, rows), emit_h))
        out_specs.append(pl.BlockSpec((K, tm), lambda i, j: (0, i)))
    elif emit_h is not None:
        out_shape.append(jax.ShapeDtypeStruct((rows, K), emit_h))
        out_specs.append(pl.BlockSpec((tm, K), lambda i, j: (i, 0)))
    outs = pl.pallas_call(
        functools.partial(_mm_kernel, has_norm=norm_w is not None, has_res=res is not None,
                          emit_h=emit_h is not None, transpose_h=transpose_h),
        grid=(rows // tm, N // tn),
        in_specs=in_specs, out_specs=out_specs, out_shape=out_shape,
        scratch_shapes=[pltpu.VMEM((tm, K), bf16)],
        compiler_params=_params(("parallel", "arbitrary")),
        name=name,
    )(*args)
    return outs if emit_h is not None else outs[0]


def _rms_kernel(x_ref, g_ref, o_ref):
    x = x_ref[...]
    ms = jnp.mean(x * x, axis=-1, keepdims=True)
    o_ref[...] = (x * lax.rsqrt(ms + NORM_EPS)) * g_ref[...]


def _rmsnorm(x, g, *, tm, row0, rows):
    K = x.shape[1]
    return pl.pallas_call(
        _rms_kernel, grid=(rows // tm,),
        in_specs=[pl.BlockSpec((tm, K), lambda i: (i + row0, 0)), pl.BlockSpec((1, K), lambda i: (0, 0))],
        out_specs=pl.BlockSpec((tm, K), lambda i: (i, 0)),
        out_shape=jax.ShapeDtypeStruct((rows, K), f32),
        compiler_params=_params(("parallel",)),
        name="rmsnorm",
    )(x, g.reshape(1, K))


def _even_token_math(h, hs, zc, zs, P):
    dh = hs - h
    mu = P["mu_wag"]
    xw = (h + dh * mu[0:1]).astype(bf16)
    xa = (h + dh * mu[1:2]).astype(bf16)
    xg = (h + dh * mu[2:3]).astype(bf16)
    tw = jnp.tanh(_dot(xw, P["w1"])).astype(bf16)
    wl = P["w0"] + _dot(tw, P["w2"])
    w_log = -jax.nn.softplus(-wl) - 0.5
    lw = -jnp.exp(w_log)
    a = jax.nn.sigmoid(P["a0"] + _dot(_dot(xa, P["a1"]).astype(bf16), P["a2"]))
    g = _dot(jax.nn.sigmoid(_dot(xg, P["g1"])).astype(bf16), P["g2"])
    rkv = zc + (zs - zc) * P["mu_rkv"]
    r, k, v = rkv[:, :D_A], rkv[:, D_A:2 * D_A], rkv[:, 2 * D_A:]
    kk = k * P["k_k"]
    nrm = jnp.sqrt(_head_sum(kk * kk, P["bd"]))
    kkn = kk / jnp.maximum(nrm, 1e-12)
    k2 = k * (1.0 + (a - 1.0) * P["k_a"])
    return r, lw, k2, v, kkn, a, g


def _pool_project(pm, P):
    outs = []
    for gi in range(len(POOL_WINDOWS)):
        c = slice(gi * POOL_GD, (gi + 1) * POOL_GD)
        outs.append(_dot(pm[:, c].astype(bf16), P["w_pool"][gi]))
    return jnp.concatenate(outs, axis=-1) * P["pool_scale"]


_EVEN_PARAM_NAMES = ("mu_wag", "mu_rkv", "w0", "w1", "w2", "a0", "a1", "a2", "g1", "g2", "k_k", "k_a",
                     "w_pool", "pool_scale", "bd")


def _load_params(refs):
    return {n: r[...] for n, r in zip(_EVEN_PARAM_NAMES, refs)}


def _even_mid_prompt_kernel(h_ref, hp_ref, z_ref, zp_ref, *rest, tm):
    prefs, outs = rest[:len(_EVEN_PARAM_NAMES)], rest[len(_EVEN_PARAM_NAMES):]
    P = _load_params(prefs)
    first = pl.program_id(0) == 0
    h = h_ref[...]
    z = z_ref[...]
    zc, u = z[:, :3 * D_A], z[:, 3 * D_A:]
    hprev = jnp.where(first, 0.0, hp_ref[15:16, :])
    zprev = jnp.where(first, 0.0, zp_ref[15:16, :3 * D_A])
    row = lax.broadcasted_iota(jnp.int32, (tm, 1), 0)
    hs = jnp.where(row == 0, hprev, pltpu.roll(h, 1, axis=0))
    zs = jnp.where(row == 0, zprev, pltpu.roll(zc, 1, axis=0))
    prow = lax.broadcasted_iota(jnp.int32, (tm, POOL_GD), 0)
    r, lw, k2, v, kkn, a, g = _even_token_math(h, hs, zc, zs, P)

    uprev = jnp.where(first, 0.0, zp_ref[:, 3 * D_A:])
    pos = pl.program_id(0) * tm + prow
    means = []
    for gi, win in enumerate(POOL_WINDOWS):
        c = slice(gi * POOL_GD, (gi + 1) * POOL_GD)
        s = jnp.concatenate([uprev[:, c], u[:, c]], axis=0)
        sh = 1
        while sh < win:
            s = s + pltpu.roll(s, sh, axis=0)
            sh *= 2
        cnt = jnp.minimum(pos + 1, win).astype(f32)
        means.append(s[16:, :] / cnt)
    ob = _pool_project(jnp.concatenate(means, axis=-1) - u, P)
    for o_ref, val in zip(outs, (r, lw, k2, v, kkn, a, g, ob)):
        o_ref[...] = val


def _even_mid_sample_kernel(h_ref, hs_ref, z_ref, zs_ref, buf_ref, *rest):
    prefs, outs = rest[:len(_EVEN_PARAM_NAMES)], rest[len(_EVEN_PARAM_NAMES):]
    P = _load_params(prefs)
    h = h_ref[...]
    z = z_ref[...]
    zc, u = z[:, :3 * D_A], z[:, 3 * D_A:]
    r, lw, k2, v, kkn, a, g = _even_token_math(h, hs_ref[...], zc, zs_ref[...], P)
    means = []
    for gi, win in enumerate(POOL_WINDOWS):
        c = slice(gi * POOL_GD, (gi + 1) * POOL_GD)
        s = u[:, c]
        for j in range(POOL_BUF - (win - 1), POOL_BUF):
            s = s + buf_ref[j, :, c]
        means.append(s / float(win))
    ob = _pool_project(jnp.concatenate(means, axis=-1) - u, P)
    for o_ref, val in zip(outs, (r, lw, k2, v, kkn, a, g, ob)):
        o_ref[...] = val


def _const_spec(shape):
    nd = len(shape)
    return pl.BlockSpec(shape, lambda i, _nd=nd: (0,) * _nd, pipeline_mode=pl.Buffered(1))


def _even_mid_prompt(h_all, z_all, plist, *, T, tm):
    nb = tm // 16
    in_specs = [pl.BlockSpec((tm, D_MODEL), lambda i: (i, 0)),
                pl.BlockSpec((16, D_MODEL), lambda i: (jnp.maximum(i * nb - 1, 0), 0)),
                pl.BlockSpec((tm, 4 * D_A), lambda i: (i, 0)),
                pl.BlockSpec((16, 4 * D_A), lambda i: (jnp.maximum(i * nb - 1, 0), 0))]
    in_specs += [_const_spec(p.shape) for p in plist]
    return pl.pallas_call(
        functools.partial(_even_mid_prompt_kernel, tm=tm),
        grid=(T // tm,), in_specs=in_specs,
        out_specs=[pl.BlockSpec((tm, D_A), lambda i: (i, 0))] * 8,
        out_shape=[jax.ShapeDtypeStruct((T, D_A), f32)] * 8,
        compiler_params=_params(("parallel",)),
        name="even_mid_prompt",
    )(h_all, h_all, z_all, z_all, *plist)


def _even_mid_sample(h_all, hs, z_all, zs, buf_t, plist, *, row0, n):
    in_specs = [pl.BlockSpec((n, D_MODEL), lambda i: (row0, 0)),
                pl.BlockSpec((n, D_MODEL), lambda i: (0, 0)),
                pl.BlockSpec((n, 4 * D_A), lambda i: (row0, 0)),
                pl.BlockSpec((n, 3 * D_A), lambda i: (0, 0)),
                pl.BlockSpec((POOL_BUF, n, D_POOL), lambda i: (0, 0, 0))]
    in_specs += [_const_spec(p.shape) for p in plist]
    return pl.pallas_call(
        _even_mid_sample_kernel, grid=(1,), in_specs=in_specs,
        out_specs=[pl.BlockSpec((n, D_A), lambda i: (0, 0))] * 8,
        out_shape=[jax.ShapeDtypeStruct((n, D_A), f32)] * 8,
        compiler_params=_params(("arbitrary",)),
        name="even_mid_sample",
    )(h_all, hs, z_all, zs, buf_t, *plist)


def _wkv_prompt_kernel(r_ref, lw_ref, k_ref, v_ref, kk_ref, a_ref, o_ref, s_out_ref, S_ref, *, tb, npair):
    C = WKV_CHUNK
    t = pl.program_id(1)

    @pl.when(t == 0)
    def _():
        S_ref[...] = jnp.zeros_like(S_ref)

    ri = lax.broadcasted_iota(jnp.int32, (C, C), 0)
    ci = lax.broadcasted_iota(jnp.int32, (C, C), 1)
    strict = ri > ci
    incl = ri >= ci
    eye = (ri == ci).astype(f32)
    rows = lax.broadcasted_iota(jnp.int32, (C, 2 * HD_A), 0)

    nh = 2 * npair
    heads = range(nh)

    def prep(sl, lanes):
        r, lw, k, v, kk, a = (x[sl, lanes] for x in (r_ref, lw_ref, k_ref, v_ref, kk_ref, a_ref))
        cum = lw
        sh = 1
        while sh < C:
            cum = cum + jnp.where(rows >= sh, pltpu.roll(cum, sh, axis=0), 0.0)
            sh *= 2
        cum_c = cum[C - 1:C, :]
        b = kk * a
        e_neg = jnp.exp(-cum)
        e_rem = jnp.exp(cum_c - cum)
        return dict(at=-kk * jnp.exp(cum - lw), rt=r * jnp.exp(cum), bt=b * e_neg, kt=k * e_neg,
                    bh=b * e_rem, kh=k * e_rem, w_c=jnp.exp(cum_c), v=v)

    def chunk(c, carry):
        off = pl.multiple_of(c * C, C)
        sl = pl.ds(off, C)
        pairs = [prep(sl, slice(p * 2 * HD_A, (p + 1) * 2 * HD_A)) for p in range(npair)]
        S0 = [S_ref[h] for h in heads]

        def head(name, h):
            return pairs[h // 2][name][:, (h % 2) * HD_A:(h % 2 + 1) * HD_A]

        ar = [jnp.concatenate([head("at", h), head("rt", h)], axis=0) for h in heads]
        bk = [jnp.concatenate([head("bt", h), head("kt", h)], axis=0) for h in heads]
        vh = [head("v", h) for h in heads]
        m4 = [_dot3(ar[h], bk[h], _dot_nt) for h in heads]
        sar = [_dot3(ar[h], S0[h], _dot_nt) for h in heads]
        a_ab = [jnp.where(strict, m4[h][:C, :C], 0.0) for h in heads]
        a_ak = [jnp.where(strict, m4[h][:C, C:], 0.0) for h in heads]
        a_rb = [jnp.where(incl, m4[h][C:, :C], 0.0) for h in heads]
        a_rk = [jnp.where(incl, m4[h][C:, C:], 0.0) for h in heads]
        av = [_dot3(jnp.concatenate([a_ak[h], a_rk[h]], axis=0), vh[h]) for h in heads]
        tinv = [eye + a_ab[h] for h in heads]
        pw = [_dot3(a_ab[h], a_ab[h]) for h in heads]
        n_sq = 2
        while n_sq * 2 < C:
            x = [_dot3(pw[h], jnp.concatenate([tinv[h], pw[h]], axis=1)) for h in heads]
            tinv = [tinv[h] + x[h][:, :C] for h in heads]
            pw = [x[h][:, C:] for h in heads]
            n_sq *= 2
        tinv = [tinv[h] + _dot3(pw[h], tinv[h]) for h in heads]
        u = [_dot3(tinv[h], sar[h][:C] + av[h][:C]) for h in heads]
        o = [sar[h][C:] + av[h][C:] + _dot3(a_rb[h], u[h]) for h in heads]
        s_new = []
        for h in heads:
            uv = jnp.concatenate([u[h], vh[h]], axis=0)
            bkh = jnp.concatenate([head("bh", h), head("kh", h)], axis=0)
            s_new.append(S0[h] * head("w_c", h) + _dot3(uv, bkh, _dot_tn))
        for h in heads:
            S_ref[h] = s_new[h]
        for p in range(npair):
            o_ref[sl, p * 2 * HD_A:(p + 1) * 2 * HD_A] = jnp.concatenate([o[2 * p], o[2 * p + 1]], axis=1)
        return carry

    lax.fori_loop(0, tb // C, chunk, 0)

    @pl.when(t == pl.num_programs(1) - 1)
    def _():
        s_out_ref[...] = S_ref[...]


def _wkv_prompt(r, lw, k2, v, kkn, a, *, T, tb, npair):
    spec = pl.BlockSpec((tb, 2 * HD_A * npair), lambda j, t: (t, j))
    nh = 2 * npair
    return pl.pallas_call(
        functools.partial(_wkv_prompt_kernel, tb=tb, npair=npair),
        grid=(H_A // nh, T // tb),
        in_specs=[spec] * 6,
        out_specs=[spec, pl.BlockSpec((nh, HD_A, HD_A), lambda j, t: (j, 0, 0))],
        out_shape=[jax.ShapeDtypeStruct((T, D_A), f32), jax.ShapeDtypeStruct((H_A, HD_A, HD_A), f32)],
        scratch_shapes=[pltpu.VMEM((nh, HD_A, HD_A), f32)],
        compiler_params=_params(("parallel", "arbitrary")),
        name="wkv_prompt",
    )(r, lw, k2, v, kkn, a)


def _wkv_sample_kernel(s_ref, r_ref, lw_ref, k_ref, kk_ref, a_ref, v_ref, o_ref, s_out_ref):
    S = s_ref[0]
    kk = kk_ref[0]
    sa = jnp.sum(S * (-kk), axis=-1, keepdims=True)
    S2 = S * jnp.exp(lw_ref[0]) + sa * (kk * a_ref[0]) + v_ref[0] * k_ref[0]
    s_out_ref[0] = S2
    o_ref[0] = jnp.sum(S2 * r_ref[0], axis=-1, keepdims=True)


def _wkv_sample(S0, r, lw, k2, v, kkn, a):
    n = S0.shape[0]
    row = lambda x: x.reshape(n, H_A, 1, HD_A)
    rspec = pl.BlockSpec((1, H_A, 1, HD_A), lambda b: (b, 0, 0, 0))
    cspec = pl.BlockSpec((1, H_A, HD_A, 1), lambda b: (b, 0, 0, 0))
    sspec = pl.BlockSpec((1, H_A, HD_A, HD_A), lambda b: (b, 0, 0, 0))
    o, S = pl.pallas_call(
        _wkv_sample_kernel, grid=(n,),
        in_specs=[sspec, rspec, rspec, rspec, rspec, rspec, cspec],
        out_specs=[cspec, sspec],
        out_shape=[jax.ShapeDtypeStruct((n, H_A, HD_A, 1), f32), jax.ShapeDtypeStruct(S0.shape, f32)],
        compiler_params=_params(("parallel",)),
        name="wkv_sample",
    )(S0, row(r), row(lw), row(k2), row(kkn), row(a), v.reshape(n, H_A, HD_A, 1))
    return o.reshape(n, D_A), S


def _even_post_kernel(x_ref, o_ref, r_ref, k_ref, v_ref, g_ref, ob_ref, wo_ref, rk_ref, gnw_ref, gnb_ref, bd_ref,
                      out_ref):
    bd = bd_ref[...]
    o = o_ref[...]
    inv = 1.0 / HD_A
    mu = _head_sum(o, bd) * inv
    d = o - mu
    var = _head_sum(d * d, bd) * inv
    on = d * lax.rsqrt(var + GN_EPS) * gnw_ref[...] + gnb_ref[...]
    v = v_ref[...]
    bonus = _head_sum(r_ref[...] * k_ref[...] * rk_ref[...], bd) * v
    oa = ((on + bonus) * g_ref[...]).astype(bf16)
    y = _dot(oa, wo_ref[:D_A, :]) + _dot(ob_ref[...].astype(bf16), wo_ref[D_A:, :])
    out_ref[...] = x_ref[...] + y


def _even_post(x_all, o, r, k2, v, g, ob, wo, rk, gnw, gnb, bd, *, row0, rows, tm):
    a_spec = pl.BlockSpec((tm, D_A), lambda i: (i, 0))
    x_spec = pl.BlockSpec((tm, D_MODEL), lambda i: (i + row0, 0))
    consts = [wo, rk, gnw, gnb, bd]
    return pl.pallas_call(
        _even_post_kernel, grid=(rows // tm,),
        in_specs=[x_spec] + [a_spec] * 6 + [_const_spec(c.shape) for c in consts],
        out_specs=x_spec,
        out_shape=jax.ShapeDtypeStruct(x_all.shape, f32),
        input_output_aliases={0: 0},
        compiler_params=_params(("parallel",)),
        name="even_post",
    )(x_all, o, r, k2, v, g, ob, *consts)


def _attn_prompt_kernel(*refs):
    in_refs, out_ref, og_ref, lg_ref = refs[:15], refs[15], refs[16], refs[17]
    n = pl.program_id(1)
    has_prev = n > 0
    scale = HD_C ** -0.5
    Q = 128
    ri = lax.broadcasted_iota(jnp.int32, (Q, Q), 0)
    ci = lax.broadcasted_iota(jnp.int32, (Q, Q), 1)
    mask_prev0 = ci >= ri
    mask_cur = ci <= ri

    for g, (_, dil) in enumerate(ATTN_GROUPS):
        q_ref, kp_ref, k_ref, vp_ref, v_ref = in_refs[5 * g:5 * g + 5]
        nblk = ATTN_BLK // (Q * dil)

        def units(it, carry, q_ref=q_ref, kp_ref=kp_ref, k_ref=k_ref, vp_ref=vp_ref, v_ref=v_ref,
                  dil=dil, nblk=nblk, g=g):
            us = [it * ATTN_UNROLL + k for k in range(ATTN_UNROLL)]
            rows, q, kc, vc, kp, vp, valid_prev = [], [], [], [], [], [], []
            for u in us:
                c = u // nblk
                m = u % nblk
                r = pl.ds(m * (Q * dil) + c, Q, stride=dil)
                in_blk = m > 0
                r_a = pl.ds(jnp.maximum(m - 1, 0) * (Q * dil) + c, Q, stride=dil)
                r_b = pl.ds((nblk - 1) * (Q * dil) + c, Q, stride=dil)
                rows.append(r)
                q.append(q_ref[r, :].astype(bf16))
                kc.append(k_ref[r, :].astype(bf16))
                vc.append(v_ref[r, :].astype(bf16))
                kp.append(jnp.where(in_blk, k_ref[r_a, :], kp_ref[r_b, :]).astype(bf16))
                vp.append(jnp.where(in_blk, v_ref[r_a, :], vp_ref[r_b, :]).astype(bf16))
                valid_prev.append(jnp.logical_or(in_blk, has_prev))
            n = range(ATTN_UNROLL)
            s_p = [_dot_nt(q[i], kp[i]) * scale for i in n]
            s_c = [_dot_nt(q[i], kc[i]) * scale for i in n]
            s_p = [jnp.where(jnp.logical_and(mask_prev0, valid_prev[i]), s_p[i], NEG_INF) for i in n]
            s_c = [jnp.where(mask_cur, s_c[i], NEG_INF) for i in n]
            mx = [jnp.maximum(jnp.max(s_p[i], axis=1, keepdims=True), jnp.max(s_c[i], axis=1, keepdims=True))
                  for i in n]
            p_p = [jnp.exp(s_p[i] - mx[i]) for i in n]
            p_c = [jnp.exp(s_c[i] - mx[i]) for i in n]
            den = [jnp.sum(p_p[i], axis=1, keepdims=True) + jnp.sum(p_c[i], axis=1, keepdims=True) for i in n]
            o = [(_dot(p_p[i].astype(bf16), vp[i]) + _dot(p_c[i].astype(bf16), vc[i])) / den[i] for i in n]
            lse = [mx[i] + jnp.log(den[i]) for i in n]
            for i in n:
                og_ref[g, rows[i], :] = o[i]
                lg_ref[g, rows[i], :] = jnp.broadcast_to(lse[i], (Q, HD_C))
            return carry

        lax.fori_loop(0, dil * nblk // ATTN_UNROLL, units, 0)

    l0, l1, l2 = lg_ref[0], lg_ref[1], lg_ref[2]
    mx = jnp.maximum(jnp.maximum(l0, l1), l2)
    e0, e1, e2 = jnp.exp(l0 - mx), jnp.exp(l1 - mx), jnp.exp(l2 - mx)
    out_ref[...] = (e0 * og_ref[0] + e1 * og_ref[1] + e2 * og_ref[2]) / (e0 + e1 + e2)


def _attn_prompt(z, *, T):
    nb = T // ATTN_BLK
    in_specs = []
    for g in range(len(ATTN_GROUPS)):
        def col(j, g=g):
            return lambda h, n: (n, g * 3 * H_C + j * H_C + h)

        def col_prev(j, g=g):
            return lambda h, n: (jnp.maximum(n - 1, 0), g * 3 * H_C + j * H_C + h)
        blk = (ATTN_BLK, HD_C)
        in_specs += [pl.BlockSpec(blk, col(0)), pl.BlockSpec(blk, col_prev(1)), pl.BlockSpec(blk, col(1)),
                     pl.BlockSpec(blk, col_prev(2)), pl.BlockSpec(blk, col(2))]
    return pl.pallas_call(
        _attn_prompt_kernel, grid=(H_C, nb),
        in_specs=in_specs,
        out_specs=pl.BlockSpec((ATTN_BLK, HD_C), lambda h, n: (n, h)),
        out_shape=jax.ShapeDtypeStruct((T, D_C), f32),
        scratch_shapes=[pltpu.VMEM((3, ATTN_BLK, HD_C), f32), pltpu.VMEM((3, ATTN_BLK, HD_C), f32)],
        compiler_params=_params(("parallel", "arbitrary")),
        name="attn_prompt",
    )(*([z] * 15))


ATTN_SAMPLE_NB = 4


def _attn_sample_kernel(z_ref, c0_ref, c1_ref, c2_ref, out_ref):
    scale = HD_C ** -0.5
    crefs = (c0_ref, c1_ref, c2_ref)
    for b in range(ATTN_SAMPLE_NB):
        for h in range(H_C):
            outs, lses = [], []
            for g in range(len(ATTN_GROUPS)):
                base = g * 3 * D_C + h * HD_C
                q = z_ref[b, :, base:base + HD_C]
                kn = z_ref[b, :, base + D_C:base + D_C + HD_C]
                vn = z_ref[b, :, base + 2 * D_C:base + 2 * D_C + HD_C]
                K = crefs[g][b, :, 0, 0, h, :]
                V = crefs[g][b, :, 0, 1, h, :]
                s_c = jnp.sum(K * q, axis=1, keepdims=True) * scale
                s_n = jnp.sum(kn * q, axis=1, keepdims=True) * scale
                mx = jnp.maximum(jnp.max(s_c, axis=0, keepdims=True), s_n)
                p_c = jnp.exp(s_c - mx)
                p_n = jnp.exp(s_n - mx)
                den = jnp.sum(p_c, axis=0, keepdims=True) + p_n
                outs.append((jnp.sum(p_c * V, axis=0, keepdims=True) + p_n * vn) / den)
                lses.append(mx + jnp.log(den))
            mx = jnp.maximum(jnp.maximum(lses[0], lses[1]), lses[2])
            es = [jnp.exp(l - mx) for l in lses]
            out_ref[b, :, h * HD_C:(h + 1) * HD_C] = (es[0] * outs[0] + es[1] * outs[1] + es[2] * outs[2]) / (
                es[0] + es[1] + es[2])


def _attn_sample(z_s, caches):
    n = z_s.shape[0]
    nb = ATTN_SAMPLE_NB
    assert n % nb == 0
    in_specs = [pl.BlockSpec((nb, 1, 9 * D_C), lambda b: (b, 0, 0))]
    args = [z_s.reshape(n, 1, 9 * D_C)]
    for (win, dil), c in zip(ATTN_GROUPS, caches):
        assert c.shape[1] == win
        args.append(c.reshape(n, win // dil, dil, 2, H_C, HD_C))
        in_specs.append(pl.BlockSpec((nb, win // dil, 1, 2, H_C, HD_C), lambda b: (b, 0, 0, 0, 0, 0)))
    out = pl.pallas_call(
        _attn_sample_kernel, grid=(n // nb,), in_specs=in_specs,
        out_specs=pl.BlockSpec((nb, 1, D_C), lambda b: (b, 0, 0)),
        out_shape=jax.ShapeDtypeStruct((n, 1, D_C), f32),
        compiler_params=_params(("parallel",)),
        name="attn_sample",
    )(*args)
    return out.reshape(n, D_C)


def _top16_rows(s):
    rows = lax.broadcasted_iota(jnp.int32, s.shape, 0)
    work = s
    vals = []
    for _ in range(PEER_TOPK):
        m = jnp.max(work, axis=0, keepdims=True)
        idx = jnp.min(jnp.where(work == m, rows, s.shape[0]), axis=0, keepdims=True)
        work = jnp.where(rows == idx, NEG_INF, work)
        vals.append(m)
    return vals, jnp.logical_and(work == NEG_INF, s != NEG_INF)


def _oddeven_mergesort_pairs(n):
    pairs = []

    def merge(lo, hi, r):
        step = r * 2
        if step < hi - lo:
            merge(lo, hi, step)
            merge(lo + r, hi, step)
            pairs.extend((i, i + r) for i in range(lo + r, hi - r, step))
        else:
            pairs.append((lo, lo + r))

    def sort(lo, hi):
        if hi - lo >= 1:
            mid = lo + (hi - lo) // 2
            sort(lo, mid)
            sort(mid + 1, hi)
            merge(lo, hi, 1)

    sort(0, n - 1)
    return pairs


SUBLANES = 8


def _top16_sorted(s):
    assert s.shape[0] == PEER_TOPK * SUBLANES
    t = [s[SUBLANES * k:SUBLANES * (k + 1), :] for k in range(PEER_TOPK)]
    for i, j in _oddeven_mergesort_pairs(PEER_TOPK):
        t[i], t[j] = jnp.maximum(t[i], t[j]), jnp.minimum(t[i], t[j])
    shift = 1
    while shift < SUBLANES:
        other = [pltpu.roll(x, shift, axis=0) for x in t]
        t = [jnp.maximum(t[i], other[PEER_TOPK - 1 - i]) for i in range(PEER_TOPK)]
        d = PEER_TOPK // 2
        while d >= 1:
            for i in range(PEER_TOPK):
                if i & d == 0:
                    t[i], t[i + d] = jnp.maximum(t[i], t[i + d]), jnp.minimum(t[i], t[i + d])
            d //= 2
        shift *= 2
    thr = t[PEER_TOPK - 1][0:1, :]
    member = s >= thr
    count = jnp.sum(member.astype(f32), axis=0, keepdims=True)
    return [x[0:1, :] for x in t], member, count == float(PEER_TOPK)


def _top16_rows_distinct(ss):
    work = list(ss)
    vals = [[] for _ in ss]
    for _ in range(PEER_TOPK):
        for i in range(len(ss)):
            m = jnp.max(work[i], axis=0, keepdims=True)
            work[i] = jnp.where(work[i] == m, NEG_INF, work[i])
            vals[i].append(m)
    out = []
    for s, w, v in zip(ss, work, vals):
        member = jnp.logical_and(w == NEG_INF, s != NEG_INF)
        count = jnp.sum(member.astype(f32), axis=0, keepdims=True)
        out.append((v, member, count == float(PEER_TOPK)))
    return out


def _peer_route_kernel(q_ref, keys_ref, s1_ref, s2_ref, tau_ref):
    n = q_ref.shape[0]

    def route_head(h, exact):
        if exact:
            top16 = lambda ss: [_top16_rows(s) + (None,) for s in ss]
            top16_scores = top16
        else:
            top16 = _top16_rows_distinct
            top16_scores = lambda ss: [_top16_sorted(s) for s in ss]
        scores = []
        for p in range(2):
            hp = 2 * h + p
            qs = q_ref[:, hp * N_KEYS:(hp + 1) * N_KEYS].astype(bf16)
            scores.append(_dot_nt(keys_ref[hp], qs) * LOG2E)
        sm, vals, oks = [], [], []
        for s, (v, member, ok) in zip(scores, top16_scores(scores)):
            sm.append(jnp.where(member, s, NEG_INF))
            vals.append(v)
            oks.append(ok)

        def pair_sums(z1, z2):
            pieces = []
            for r1 in range(PEER_TOPK):
                cnt = PEER_TOPK // (r1 + 1)
                pieces.append(z1[r1] + jnp.concatenate(z2[:cnt], axis=0))
            npad = (-sum(p.shape[0] for p in pieces)) % 8
            pieces.append(jnp.full((npad, n), NEG_INF, f32))
            return jnp.concatenate(pieces, axis=0)

        z1 = [v - vals[0][0] for v in vals[0]]
        z2 = [v - vals[1][0] for v in vals[1]]
        (top, chosen, ok), = top16([pair_sums(z1, z2)])
        oks.append(ok)
        log_z = jnp.log2(sum(jnp.exp2(t) for t in top))
        z1 = [z - log_z for z in z1]
        sums = pair_sums(z1, z2)
        if exact:
            (top, _, _), = top16([sums])
            tau = top[-1]
        else:
            tau = jnp.min(jnp.where(chosen, sums, jnp.inf), axis=0, keepdims=True)
            count = jnp.sum((sums >= tau).astype(f32), axis=0, keepdims=True)
            oks.append(count == float(PEER_TOPK))
        rows = slice(h * N_KEYS, (h + 1) * N_KEYS)
        l1 = (sm[0] - vals[0][0]) - log_z
        for t in range(N_KEYS // SUBLANES):
            s1_ref[t, h * SUBLANES:(h + 1) * SUBLANES, :] = l1[t * SUBLANES:(t + 1) * SUBLANES, :]
        s2_ref[rows, :] = sm[1] - vals[1][0]
        tau_ref[h:h + 1, :] = tau
        if exact:
            return None
        all_ok = functools.reduce(jnp.logical_and, oks)
        return jnp.min(all_ok.astype(f32)) > 0.5

    tie_free = [route_head(h, exact=False) for h in range(PEER_HEADS)]
    for h in range(PEER_HEADS):
        pl.when(jnp.logical_not(tie_free[h]))(functools.partial(route_head, h, True))


def _peer_route(q, keys, *, tmr):
    M = q.shape[0]
    big = jax.ShapeDtypeStruct((PEER_HEADS * N_KEYS, M), f32)
    bspec = pl.BlockSpec((PEER_HEADS * N_KEYS, tmr), lambda i: (0, i))
    n_tiles, tile_rows = N_KEYS // SUBLANES, PEER_HEADS * SUBLANES
    return pl.pallas_call(
        _peer_route_kernel, grid=(M // tmr,),
        in_specs=[pl.BlockSpec((tmr, 2 * PEER_HEADS * N_KEYS), lambda i: (i, 0)),
                  _const_spec(keys.shape)],
        out_specs=[pl.BlockSpec((n_tiles, tile_rows, tmr), lambda i: (0, 0, i)), bspec,
                   pl.BlockSpec((PEER_HEADS, tmr), lambda i: (0, i))],
        out_shape=[jax.ShapeDtypeStruct((n_tiles, tile_rows, M), f32), big,
                   jax.ShapeDtypeStruct((PEER_HEADS, M), f32)],
        compiler_params=_params(("parallel",)),
        name="peer_route",
    )(q, keys)


def _peer_prep_kernel(u_ref, v_ref, ub_ref, vt_ref):
    ub_ref[...] = u_ref[...].astype(bf16)
    vt_ref[...] = v_ref[...].T.astype(bf16)


def _peer_prep(u_tab, v_tab, *, te):
    L, E, D = u_tab.shape
    return pl.pallas_call(
        _peer_prep_kernel, grid=(L, E // te),
        in_specs=[pl.BlockSpec((None, te, D), lambda l, j: (l, j, 0))] * 2,
        out_specs=[pl.BlockSpec((None, te, D), lambda l, j: (l, j, 0)),
                   pl.BlockSpec((None, D, te), lambda l, j: (l, 0, j))],
        out_shape=[jax.ShapeDtypeStruct((L, E, D), bf16), jax.ShapeDtypeStruct((L, D, E), bf16)],
        compiler_params=_params(("parallel", "parallel")),
        name="peer_prep",
    )(u_tab, v_tab)


PEER_RB = 32
PEER_LC = 256
PEER_I1 = 4


def _peer_dense_kernel(xt_ref, u_ref, vt_ref, s1_ref, s2_ref, tau_ref, res_ref, out_ref, acc_ref, *, tm, te):
    j = pl.program_id(1)
    n_i1 = te // N_KEYS
    n_rb = N_KEYS // PEER_RB

    @pl.when(j == 0)
    def _():
        acc_ref[...] = jnp.zeros_like(acc_ref)

    chunks = [slice(c * PEER_LC, (c + 1) * PEER_LC) for c in range(tm // PEER_LC)]
    assert n_i1 == SUBLANES
    act = _dot(u_ref[...], xt_ref[...])
    cols = []
    for lanes in chunks:
        s1 = [s1_ref[h * SUBLANES:(h + 1) * SUBLANES, lanes] for h in range(PEER_HEADS)]
        pieces = [[None] * n_rb for _ in range(n_i1)]
        for rb in range(n_rb):
            for i0 in range(0, n_i1, PEER_I1):
                group = range(i0, i0 + PEER_I1)
                gates = {ii: jnp.zeros((PEER_RB, PEER_LC), f32) for ii in group}
                for h in range(PEER_HEADS):
                    s2 = s2_ref[h * N_KEYS + rb * PEER_RB:h * N_KEYS + (rb + 1) * PEER_RB, lanes]
                    tau = tau_ref[h:h + 1, lanes]
                    for ii in group:
                        c = s2 + s1[h][ii:ii + 1, :]
                        gates[ii] = gates[ii] + jnp.where(c >= tau, jnp.exp2(c), 0.0)
                for ii in group:
                    a = act[ii * N_KEYS + rb * PEER_RB:ii * N_KEYS + (rb + 1) * PEER_RB, lanes]
                    pieces[ii][rb] = (_gelu(a) * gates[ii]).astype(bf16)
        cols.append(jnp.concatenate([pc for row in pieces for pc in row], axis=0))
    acc_ref[...] += _dot(vt_ref[...], jnp.concatenate(cols, axis=1))

    rw = res_ref.shape[1]
    out_ref[:, pl.ds(pl.multiple_of(j * rw, rw), rw)] = res_ref[...]

    @pl.when(j == pl.num_programs(1) - 1)
    def _():
        out_ref[...] += acc_ref[...].T


def _peer_dense(xt, ub, vt, layer, s1, s2, tau, res, *, tm, te):
    D, M = xt.shape
    E = ub.shape[1]
    nj = E // te
    assert D % nj == 0 and (D // nj) % LANE == 0
    return pl.pallas_call(
        functools.partial(_peer_dense_kernel, tm=tm, te=te),
        grid=(M // tm, nj),
        in_specs=[pl.BlockSpec((D, tm), lambda i, j: (0, i)),
                  pl.BlockSpec((None, te, D), lambda i, j: (layer, j, 0)),
                  pl.BlockSpec((None, D, te), lambda i, j: (layer, 0, j)),
                  pl.BlockSpec((None, PEER_HEADS * SUBLANES, tm), lambda i, j: (j, 0, i)),
                  pl.BlockSpec((PEER_HEADS * N_KEYS, tm), lambda i, j: (0, i)),
                  pl.BlockSpec((PEER_HEADS, tm), lambda i, j: (0, i)),
                  pl.BlockSpec((tm, D // nj), lambda i, j: (i, j))],
        out_specs=pl.BlockSpec((tm, D), lambda i, j: (i, 0)),
        out_shape=jax.ShapeDtypeStruct((M, D), f32),
        scratch_shapes=[pltpu.VMEM((D, tm), f32)],
        compiler_params=_params(("parallel", "arbitrary")),
        name="peer_dense",
    )(xt, ub, vt, s1, s2, tau, res)


def _peer(x_all, norm_w, wq, keys, ub, vt, layer, *, tm_mm, tmr, tm, te):
    q, ht = _mm(x_all, wq, norm_w=norm_w, emit_h=bf16, transpose_h=True, tm=tm_mm, tn=wq.shape[1],
                name="peer_query")
    s1, s2, tau = _peer_route(q, keys, tmr=tmr)
    return _peer_dense(ht, ub, vt, layer, s1, s2, tau, x_all, tm=tm, te=te)


LANE = 128


def _block_diag_ones():
    head_of_lane = jnp.arange(D_A) // HD_A
    return (jnp.arange(LANE)[:, None] == head_of_lane[None, :]).astype(bf16)


def _pad_lanes(w):
    return jnp.pad(w, ((0, 0), (0, (-w.shape[1]) % LANE)))


def _pad_rows(w):
    return jnp.pad(w, ((0, (-w.shape[0]) % LANE), (0, 0)))


def kernel(x_prompt, x_sample, state_shift, state_wkv, state_pool, cache_kv_w128, cache_kv_w512, cache_kv_w2048,
           norm_mix, norm_ffn, norm_final, a_w_in, a_w_out, a_mu_rkv, a_mu_wag, a_w0, a_w1, a_w2, a_a0, a_a1,
           a_a2, a_g1, a_g2, a_k_k, a_k_a, a_r_k, a_gn_w, a_gn_b, b_w_pool, b_scale, c_w_in, c_w_out, p_w_q,
           p_sub_keys, p_u, p_v):
    T = x_prompt.shape[1]
    NS = x_sample.shape[0]
    TM = 768
    M = -(-(T + NS) // TM) * TM
    S_BLK = T // NS
    assert T % NS == 0 and T % ATTN_BLK == 0
    cb = lambda w: w.astype(bf16)
    row = lambda w: w.reshape(1, -1)

    x0 = jnp.concatenate([x_prompt.reshape(T, D_MODEL), x_sample.reshape(NS, D_MODEL),
                          jnp.zeros((M - T - NS, D_MODEL), f32)], axis=0)
    ub, vt = _peer_prep(p_u, p_v, te=512)
    bd = _block_diag_ones()

    w_in = cb(a_w_in[0])
    z0, h0 = _mm(x0, w_in, norm_w=norm_mix[0], emit_h=f32, tm=TM, tn=1024, name="even_in_proj")
    zs_prev = _mm(state_shift[0], w_in[:, :3 * D_A], tm=NS, tn=1024, name="even_in_proj_state")
    plist = [a_mu_wag[0], row(a_mu_rkv[0]), row(a_w0[0]), _pad_lanes(cb(a_w1[0])), _pad_rows(cb(a_w2[0])),
             row(a_a0[0]), _pad_lanes(cb(a_a1[0])), _pad_rows(cb(a_a2[0])), cb(a_g1[0]), cb(a_g2[0]),
             row(a_k_k[0]), row(a_k_a[0]), cb(b_w_pool[0]), row(b_scale[0]), bd]
    rp, lwp, kp, vp, kkp, ap, gp, obp = _even_mid_prompt(h0, z0, plist, T=T, tm=256)
    rs, lws, ks, vs, kks, as_, gs, obs = _even_mid_sample(
        h0, state_shift[0], z0, zs_prev, jnp.swapaxes(state_pool[0], 0, 1), plist, row0=S_BLK, n=NS)
    o_p, wkv_p = _wkv_prompt(rp, lwp, kp, vp, kkp, ap, T=T, tb=512, npair=8)
    o_s, wkv_s = _wkv_sample(state_wkv[0], rs, lws, ks, vs, kks, as_)
    post_c = (cb(a_w_out[0]), row(a_r_k[0]), row(a_gn_w[0]), row(a_gn_b[0]), bd)
    x1 = _even_post(x0, o_p, rp, kp, vp, gp, obp, *post_c, row0=0, rows=T, tm=256)
    x1 = _even_post(x1, o_s, rs, ks, vs, gs, obs, *post_c, row0=S_BLK, rows=NS, tm=NS)
    x1 = _peer(x1, norm_ffn[0], cb(p_w_q[0]), cb(p_sub_keys[0].reshape(2 * PEER_HEADS, N_KEYS, N_KEYS)),
               ub, vt, 0, tm_mm=TM, tmr=256, tm=TM, te=1024)

    z1 = _mm(x1, cb(c_w_in[0]), norm_w=norm_mix[1], tm=TM, tn=1024, name="odd_in_proj")
    att_p = _attn_prompt(z1, T=T)
    z1s = z1[T:T + NS]
    att_s = _attn_sample(z1s, (cache_kv_w128[0], cache_kv_w512[0], cache_kv_w2048[0]))
    att = jnp.concatenate([att_p, att_s, jnp.zeros((M - T - NS, D_C), f32)], axis=0)
    x2 = _mm(att, cb(c_w_out[0]), res=x1, tm=TM, tn=D_MODEL, name="odd_out_proj")
    x2 = _peer(x2, norm_ffn[1], cb(p_w_q[1]), cb(p_sub_keys[1].reshape(2 * PEER_HEADS, N_KEYS, N_KEYS)),
               ub, vt, 1, tm_mm=TM, tmr=256, tm=TM, te=1024)

    y_p = _rmsnorm(x2, norm_final, tm=512, row0=0, rows=T)
    y_s = _rmsnorm(x2, norm_final, tm=NS, row0=S_BLK, rows=NS)

    u_p = z0[T - POOL_BUF:T, 3 * D_A:]
    u_s = z0[T:T + NS, 3 * D_A:]
    pool_s = jnp.concatenate([state_pool[0][:, 1:], u_s[:, None, :]], axis=1)
    kv_p, kv_s = [], []
    for g, (win, _) in enumerate(ATTN_GROUPS):
        n = min(win, T)
        kv = z1[T - n:T + NS, g * 3 * D_C + D_C:g * 3 * D_C + 3 * D_C].reshape(n + NS, 2, H_C, HD_C)
        kv_p.append(kv[:n][None, None])
        kv_s.append(kv[n:][None, :, None])
    return (y_p[None], y_s[:, None, :],
            h0[T - 1][None, None], h0[T:T + NS][None],
            wkv_p[None, None], wkv_s[None],
            u_p[None, None], pool_s[None],
            kv_p[0], kv_s[0], kv_p[1], kv_s[1], kv_p[2], kv_s[2])
```
